```python
import math
import jax, jax.numpy as jnp
from jax import lax
import numpy as np

D_MODEL = 1024
BATCH = 16
SEQ = 4096
DEPTH = 2
DEC_BATCH = 8
DEC_SEQ = 16
PAST_LEN = 1024

CHUNK = 64
CONV_W = 4
RMS_EPS = 1e-6
SSM_INNER = 2 * D_MODEL
SSM_HEAD_DIM = 64
SSM_HEADS = SSM_INNER // SSM_HEAD_DIM
SSM_GROUPS = 4
SSM_HPG = SSM_HEADS // SSM_GROUPS
SSM_STATE = 128
SSM_CONV_DIM = SSM_INNER + 2 * SSM_GROUPS * SSM_STATE
SSM_IN = SSM_INNER + SSM_CONV_DIM + SSM_HEADS
ML_INNER = 2 * D_MODEL
ML_HEADS = 8
ML_HEAD_DIM = ML_INNER // ML_HEADS
D_FF = ((8 * D_MODEL // 3) + 255) // 256 * 256
N_EXPERTS = 8
TOP_K = 2
D_FF_EXPERT = D_FF
N_SSM_LAYERS = (DEPTH + 1) // 2
N_MLSTM_LAYERS = DEPTH // 2

kernel_name = 'hybrid_ssd_mlstm_adaln_stream_step'


def _rms_norm(x, g):
    x32 = x.astype(jnp.float32)
    y = x32 * lax.rsqrt(jnp.mean(x32 * x32, axis=-1, keepdims=True) + RMS_EPS)
    return (y * g.astype(jnp.float32)).astype(x.dtype)


def _causal_conv(x, state, w, b):
    t = x.shape[1]
    xp = jnp.concatenate([state.astype(x.dtype), x], axis=1)
    y = b
    for k in range(CONV_W):
        y = y + xp[:, k:k + t] * w[k]
    return y, xp[:, t:]


def _chunk_len(t):
    return CHUNK if t % CHUNK == 0 else t


def _to_chunks(a, L):
    bsz, t = a.shape[:2]
    return a.reshape((bsz, t // L, L) + a.shape[2:]).swapaxes(0, 1)


def _from_chunks(a):
    nc, bsz, L = a.shape[:3]
    return a.swapaxes(0, 1).reshape((bsz, nc * L) + a.shape[3:])


def _ssd_scan(xs, dt, a, bm, cm, h0):
    L = _chunk_len(xs.shape[1])
    tri = jnp.tril(jnp.ones((L, L), dtype=bool))[None, :, :, None, None]

    def step(h, inp):
        x_c, dt_c, b_c, c_c = inp
        cum = jnp.cumsum(dt_c * a, axis=1)
        decay = jnp.exp(jnp.where(tri, cum[:, :, None] - cum[:, None], -jnp.inf))
        cb = jnp.einsum('btgn,bsgn->btsg', c_c, b_c)
        w = cb[..., None] * decay * dt_c[:, None]
        y = jnp.einsum('btsgr,bsgrp->btgrp', w, x_c)
        y = y + jnp.einsum('btgn,bgrpn->btgrp', c_c, h) * jnp.exp(cum)[..., None]
        w_end = jnp.exp(cum[:, -1:] - cum) * dt_c
        h = h * jnp.exp(cum[:, -1])[..., None, None] + jnp.einsum('bsgr,bsgrp,bsgn->bgrpn', w_end, x_c, b_c)
        return h, y

    h, ys = lax.scan(step, h0, (_to_chunks(xs, L), _to_chunks(dt, L), _to_chunks(bm, L), _to_chunks(cm, L)))
    return _from_chunks(ys), h


def _mamba2(h, conv_state, ssm_state, w_in, conv_w, conv_b, dt_bias, a_log, d_skip, norm_w, w_out):
    bsz, t, _ = h.shape
    f32 = jnp.float32
    proj = h @ w_in
    z = proj[..., :SSM_INNER]
    xbc = proj[..., SSM_INNER:SSM_INNER + SSM_CONV_DIM]
    dt_raw = proj[..., SSM_INNER + SSM_CONV_DIM:]
    xbc, new_conv = _causal_conv(xbc, conv_state, conv_w, conv_b)
    xbc = jax.nn.silu(xbc).astype(f32)
    gn = SSM_GROUPS * SSM_STATE
    xs = xbc[..., :SSM_INNER].reshape(bsz, t, SSM_GROUPS, SSM_HPG, SSM_HEAD_DIM)
    bm = xbc[..., SSM_INNER:SSM_INNER + gn].reshape(bsz, t, SSM_GROUPS, SSM_STATE)
    cm = xbc[..., SSM_INNER + gn:].reshape(bsz, t, SSM_GROUPS, SSM_STATE)
    dt = jax.nn.softplus(dt_raw.astype(f32) + dt_bias.astype(f32)).reshape(bsz, t, SSM_GROUPS, SSM_HPG)
    a = -jnp.exp(a_log.astype(f32)).reshape(SSM_GROUPS, SSM_HPG)
    h0 = ssm_state.astype(f32).reshape(bsz, SSM_GROUPS, SSM_HPG, SSM_HEAD_DIM, SSM_STATE)
    y, h_new = _ssd_scan(xs, dt, a, bm, cm, h0)
    y = y + d_skip.astype(f32).reshape(SSM_GROUPS, SSM_HPG)[:, :, None] * xs
    gw = SSM_INNER // SSM_GROUPS
    y = y.reshape(bsz, t, SSM_GROUPS, gw) * jax.nn.silu(z.astype(f32)).reshape(bsz, t, SSM_GROUPS, gw)
    y = _rms_norm(y, norm_w.reshape(SSM_GROUPS, gw)).reshape(bsz, t, SSM_INNER).astype(h.dtype)
    return y @ w_out, new_conv, h_new.reshape(bsz, SSM_HEADS, SSM_HEAD_DIM, SSM_STATE)


def _mlstm_scan(q, k, v, li, lf, c0, n0, m0):
    L = _chunk_len(q.shape[1])
    tri = jnp.tril(jnp.ones((L, L), dtype=bool))[None, :, :, None]

    def step(carry, inp):
        C, n, m = carry
        q_c, k_c, v_c, li_c, lf_c = inp
        b = jnp.cumsum(lf_c, axis=1)
        dmat = jnp.where(tri, b[:, :, None] - b[:, None] + li_c[:, None], -jnp.inf)
        a_inter = b + m[:, None]
        m_t = jnp.maximum(a_inter, jnp.max(dmat, axis=2))
        s_mat = jnp.exp(dmat - m_t[:, :, None]) * jnp.einsum('bthd,bshd->btsh', q_c, k_c)
        w_inter = jnp.exp(a_inter - m_t)
        num = jnp.einsum('btsh,bshe->bthe', s_mat, v_c) + w_inter[..., None] * jnp.einsum('bthd,bhde->bthe', q_c, C)
        den = jnp.sum(s_mat, axis=2) + w_inter * jnp.einsum('bthd,bhd->bth', q_c, n)
        h_c = num / jnp.maximum(jnp.abs(den), jnp.exp(-m_t))[..., None]
        m_new = m_t[:, -1]
        w_end = jnp.exp(b[:, -1:] - b + li_c - m_new[:, None])
        w_old = jnp.exp(b[:, -1] + m - m_new)
        C = w_old[..., None, None] * C + jnp.einsum('bsh,bshd,bshe->bhde', w_end, k_c, v_c)
        n = w_old[..., None] * n + jnp.einsum('bsh,bshd->bhd', w_end, k_c)
        return (C, n, m_new), h_c

    inps = tuple(_to_chunks(a_, L) for a_ in (q, k, v, li, lf))
    (C, n, m), hs = lax.scan(step, (c0, n0, m0), inps)
    return _from_chunks(hs), C, n, m


def _mlstm(h, conv_state, c0, n0, m0, w_in, conv_w, conv_b, w_q, w_k, w_v,
           w_igate, b_igate, w_fgate, b_fgate, norm_w, skip, w_out):
    bsz, t, _ = h.shape
    f32 = jnp.float32
    proj = h @ w_in
    xm = proj[..., :ML_INNER]
    o_pre = proj[..., ML_INNER:]
    xc, new_conv = _causal_conv(xm, conv_state, conv_w, conv_b)
    xc = jax.nn.silu(xc)
    xc_h = xc.reshape(bsz, t, ML_HEADS, ML_HEAD_DIM)
    xm_h = xm.reshape(bsz, t, ML_HEADS, ML_HEAD_DIM)
    q = jnp.einsum('bthd,hde->bthe', xc_h, w_q)
    k = jnp.einsum('bthd,hde->bthe', xc_h, w_k)
    v = jnp.einsum('bthd,hde->bthe', xm_h, w_v)
    qkv = jnp.concatenate([q, k, v], axis=-1).reshape(bsz, t, 3 * ML_INNER)
    li = (qkv @ w_igate).astype(f32) + b_igate.astype(f32)
    lf = jax.nn.log_sigmoid((qkv @ w_fgate).astype(f32) + b_fgate.astype(f32))
    k_s = k.astype(f32) * (ML_HEAD_DIM ** -0.5)
    hh, C, n, m = _mlstm_scan(q.astype(f32), k_s, v.astype(f32), li, lf,
                              c0.astype(f32), n0.astype(f32), m0.astype(f32))
    hh = _rms_norm(hh, norm_w.reshape(ML_HEADS, ML_HEAD_DIM)).reshape(bsz, t, ML_INNER)
    out = (hh + skip.astype(f32) * xc.astype(f32)) * jax.nn.sigmoid(o_pre.astype(f32))
    return out.astype(h.dtype) @ w_out, new_conv, C, n, m


def _swiglu(h, w_gate, w_up, w_down):
    return (jax.nn.silu(h @ w_gate) * (h @ w_up)) @ w_down


def _moe(h, w_router, b_router, w_gate, w_up, w_down):
    logits = (h @ w_router).astype(jnp.float32) + b_router.astype(jnp.float32)
    top_logit, top_idx = lax.top_k(logits, TOP_K)
    top_w = jax.nn.softmax(top_logit, axis=-1)
    combine = jnp.sum(jax.nn.one_hot(top_idx, N_EXPERTS, dtype=jnp.float32) * top_w[..., None], axis=-2)
    combine = combine.astype(h.dtype)
    y = jnp.zeros_like(h)
    for e in range(N_EXPERTS):
        y = y + combine[..., e:e + 1] * _swiglu(h, w_gate[e], w_up[e], w_down[e])
    return y


def _trunk(x, c, ssm_conv, ssm_state, ml_conv, ml_c, ml_n, ml_m, p):
    n_ssm_conv, n_ssm, n_ml_conv, n_ml_c, n_ml_n, n_ml_m = [], [], [], [], [], []
    c_act = jax.nn.silu(c)
    for i in range(DEPTH):
        j = i // 2
        mod = (c_act @ p['w_ada'][i] + p['b_ada'][i])[:, None, :]
        sh_m, sc_m, g_m, sh_f, sc_f, g_f = jnp.split(mod, 6, axis=-1)
        h = _rms_norm(x, p['norm_mix'][i]) * (1 + sc_m) + sh_m
        if i % 2 == 0:
            mix, cs, ss = _mamba2(h, ssm_conv[j], ssm_state[j], p['ssm_w_in'][j], p['ssm_conv_w'][j],
                                  p['ssm_conv_b'][j], p['ssm_dt_bias'][j], p['ssm_a_log'][j], p['ssm_d'][j],
                                  p['ssm_norm'][j], p['ssm_w_out'][j])
            n_ssm_conv.append(cs)
            n_ssm.append(ss)
        else:
            mix, cs, cc, nn_, mm = _mlstm(h, ml_conv[j], ml_c[j], ml_n[j], ml_m[j], p['ml_w_in'][j],
                                          p['ml_conv_w'][j], p['ml_conv_b'][j], p['ml_w_q'][j], p['ml_w_k'][j],
                                          p['ml_w_v'][j], p['ml_w_igate'][j], p['ml_b_igate'][j],
                                          p['ml_w_fgate'][j], p['ml_b_fgate'][j], p['ml_norm'][j],
                                          p['ml_skip'][j], p['ml_w_out'][j])
            n_ml_conv.append(cs)
            n_ml_c.append(cc)
            n_ml_n.append(nn_)
            n_ml_m.append(mm)
        x = x + g_m * mix
        h = _rms_norm(x, p['norm_ffn'][i]) * (1 + sc_f) + sh_f
        if i % 2 == 0:
            ffn = _swiglu(h, p['ffn_w_gate'][j], p['ffn_w_up'][j], p['ffn_w_down'][j])
        else:
            ffn = _moe(h, p['moe_w_router'][j], p['moe_b_router'][j], p['moe_w_gate'][j],
                       p['moe_w_up'][j], p['moe_w_down'][j])
        x = x + g_f * ffn
    y = _rms_norm(x, p['norm_final'])
    return (y, jnp.stack(n_ssm_conv), jnp.stack(n_ssm), jnp.stack(n_ml_conv),
            jnp.stack(n_ml_c), jnp.stack(n_ml_n), jnp.stack(n_ml_m))


def setup_inputs(seed: int = 0) -> dict:
    key = jax.random.key(seed)
    kit = iter(jax.random.split(key, 64))
    f32 = jnp.float32

    def nrm(shape, scale):
        return jax.random.normal(next(kit), shape, f32) * scale

    ns, nm = N_SSM_LAYERS, N_MLSTM_LAYERS
    D = D_MODEL
    inp = {}
    inp['x_prompt'] = nrm((BATCH, SEQ, D), 1.0)
    inp['x_sample'] = nrm((DEC_BATCH, DEC_SEQ, D), 1.0)
    inp['c_prompt'] = nrm((BATCH, D), 1.0)
    inp['c_sample'] = nrm((DEC_BATCH, D), 1.0)
    inp['state_ssm_conv'] = nrm((ns, DEC_BATCH, CONV_W - 1, SSM_CONV_DIM), 1.0)
    inp['state_ssm'] = nrm((ns, DEC_BATCH, SSM_HEADS, SSM_HEAD_DIM, SSM_STATE), 0.5)
    inp['state_mlstm_conv'] = nrm((nm, DEC_BATCH, CONV_W - 1, ML_INNER), 1.0)
    inp['state_mlstm_C'] = nrm((nm, DEC_BATCH, ML_HEADS, ML_HEAD_DIM, ML_HEAD_DIM), 0.5)
    inp['state_mlstm_n'] = nrm((nm, DEC_BATCH, ML_HEADS, ML_HEAD_DIM), 0.5)
    inp['state_mlstm_m'] = nrm((nm, DEC_BATCH, ML_HEADS), 1.0)
    inp['w_ada'] = nrm((DEPTH, D, 6 * D), D ** -0.5)
    inp['b_ada'] = nrm((DEPTH, 6 * D), 0.02)
    inp['norm_mix'] = 1.0 + nrm((DEPTH, D), 0.02)
    inp['norm_ffn'] = 1.0 + nrm((DEPTH, D), 0.02)
    inp['norm_final'] = 1.0 + nrm((D,), 0.02)
    inp['ssm_w_in'] = nrm((ns, D, SSM_IN), D ** -0.5)
    inp['ssm_conv_w'] = nrm((ns, CONV_W, SSM_CONV_DIM), CONV_W ** -0.5)
    inp['ssm_conv_b'] = nrm((ns, SSM_CONV_DIM), 0.02)
    dt0 = jnp.exp(jax.random.uniform(next(kit), (ns, SSM_HEADS), f32, math.log(1e-3), math.log(1e-1)))
    inp['ssm_dt_bias'] = dt0 + jnp.log(-jnp.expm1(-dt0))
    inp['ssm_a_log'] = jnp.log(jax.random.uniform(next(kit), (ns, SSM_HEADS), f32, 1.0, 16.0))
    inp['ssm_d'] = 1.0 + nrm((ns, SSM_HEADS), 0.1)
    inp['ssm_norm'] = 1.0 + nrm((ns, SSM_INNER), 0.02)
    inp['ssm_w_out'] = nrm((ns, SSM_INNER, D), SSM_INNER ** -0.5)
    inp['ml_w_in'] = nrm((nm, D, 2 * ML_INNER), D ** -0.5)
    inp['ml_conv_w'] = nrm((nm, CONV_W, ML_INNER), CONV_W ** -0.5)
    inp['ml_conv_b'] = nrm((nm, ML_INNER), 0.02)
    inp['ml_w_q'] = nrm((nm, ML_HEADS, ML_HEAD_DIM, ML_HEAD_DIM), ML_HEAD_DIM ** -0.5)
    inp['ml_w_k'] = nrm((nm, ML_HEADS, ML_HEAD_DIM, ML_HEAD_DIM), ML_HEAD_DIM ** -0.5)
    inp['ml_w_v'] = nrm((nm, ML_HEADS, ML_HEAD_DIM, ML_HEAD_DIM), ML_HEAD_DIM ** -0.5)
    inp['ml_w_igate'] = nrm((nm, 3 * ML_INNER, ML_HEADS), (3 * ML_INNER) ** -0.5)
    inp['ml_b_igate'] = nrm((nm, ML_HEADS), 0.1)
    inp['ml_w_fgate'] = nrm((nm, 3 * ML_INNER, ML_HEADS), (3 * ML_INNER) ** -0.5)
    inp['ml_b_fgate'] = 3.0 + jnp.linspace(0.0, 3.0, ML_HEADS, dtype=f32)[None] + nrm((nm, ML_HEADS), 0.1)
    inp['ml_norm'] = 1.0 + nrm((nm, ML_INNER), 0.02)
    inp['ml_skip'] = 1.0 + nrm((nm, ML_INNER), 0.02)
    inp['ml_w_out'] = nrm((nm, ML_INNER, D), ML_INNER ** -0.5)
    inp['ffn_w_gate'] = nrm((ns, D, D_FF), D ** -0.5)
    inp['ffn_w_up'] = nrm((ns, D, D_FF), D ** -0.5)
    inp['ffn_w_down'] = nrm((ns, D_FF, D), D_FF ** -0.5)
    inp['moe_w_router'] = nrm((nm, D, N_EXPERTS), D ** -0.5)
    inp['moe_b_router'] = nrm((nm, N_EXPERTS), 0.01)
    inp['moe_w_gate'] = nrm((nm, N_EXPERTS, D, D_FF_EXPERT), D ** -0.5)
    inp['moe_w_up'] = nrm((nm, N_EXPERTS, D, D_FF_EXPERT), D ** -0.5)
    inp['moe_w_down'] = nrm((nm, N_EXPERTS, D_FF_EXPERT, D), D_FF_EXPERT ** -0.5)
    return inp


def reference(x_prompt, x_sample, c_prompt, c_sample, state_ssm_conv, state_ssm, state_mlstm_conv,
              state_mlstm_C, state_mlstm_n, state_mlstm_m, w_ada, b_ada, norm_mix, norm_ffn, norm_final,
              ssm_w_in, ssm_conv_w, ssm_conv_b, ssm_dt_bias, ssm_a_log, ssm_d, ssm_norm, ssm_w_out,
              ml_w_in, ml_conv_w, ml_conv_b, ml_w_q, ml_w_k, ml_w_v, ml_w_igate, ml_b_igate, ml_w_fgate,
              ml_b_fgate, ml_norm, ml_skip, ml_w_out, ffn_w_gate, ffn_w_up, ffn_w_down,
              moe_w_router, moe_b_router, moe_w_gate, moe_w_up, moe_w_down):
    p = {'w_ada': w_ada, 'b_ada': b_ada, 'norm_mix': norm_mix, 'norm_ffn': norm_ffn, 'norm_final': norm_final,
         'ssm_w_in': ssm_w_in, 'ssm_conv_w': ssm_conv_w, 'ssm_conv_b': ssm_conv_b, 'ssm_dt_bias': ssm_dt_bias,
         'ssm_a_log': ssm_a_log, 'ssm_d': ssm_d, 'ssm_norm': ssm_norm, 'ssm_w_out': ssm_w_out,
         'ml_w_in': ml_w_in, 'ml_conv_w': ml_conv_w, 'ml_conv_b': ml_conv_b, 'ml_w_q': ml_w_q,
         'ml_w_k': ml_w_k, 'ml_w_v': ml_w_v, 'ml_w_igate': ml_w_igate, 'ml_b_igate': ml_b_igate,
         'ml_w_fgate': ml_w_fgate, 'ml_b_fgate': ml_b_fgate, 'ml_norm': ml_norm, 'ml_skip': ml_skip,
         'ml_w_out': ml_w_out, 'ffn_w_gate': ffn_w_gate, 'ffn_w_up': ffn_w_up, 'ffn_w_down': ffn_w_down,
         'moe_w_router': moe_w_router, 'moe_b_router': moe_b_router, 'moe_w_gate': moe_w_gate,
         'moe_w_up': moe_w_up, 'moe_w_down': moe_w_down}
    f32 = jnp.float32
    bsz = x_prompt.shape[0]
    dt = x_prompt.dtype
    z_ssm_conv = jnp.zeros((N_SSM_LAYERS, bsz, CONV_W - 1, SSM_CONV_DIM), dt)
    z_ssm = jnp.zeros((N_SSM_LAYERS, bsz, SSM_HEADS, SSM_HEAD_DIM, SSM_STATE), f32)
    z_ml_conv = jnp.zeros((N_MLSTM_LAYERS, bsz, CONV_W - 1, ML_INNER), dt)
    z_ml_c = jnp.zeros((N_MLSTM_LAYERS, bsz, ML_HEADS, ML_HEAD_DIM, ML_HEAD_DIM), f32)
    z_ml_n = jnp.zeros((N_MLSTM_LAYERS, bsz, ML_HEADS, ML_HEAD_DIM), f32)
    z_ml_m = jnp.zeros((N_MLSTM_LAYERS, bsz, ML_HEADS), f32)
    y_prompt, p_ssm_conv, p_ssm, p_ml_conv, p_ml_c, p_ml_n, p_ml_m = _trunk(
        x_prompt, c_prompt, z_ssm_conv, z_ssm, z_ml_conv, z_ml_c, z_ml_n, z_ml_m, p)
    y_sample, s_ssm_conv, s_ssm, s_ml_conv, s_ml_c, s_ml_n, s_ml_m = _trunk(
        x_sample, c_sample, state_ssm_conv, state_ssm, state_mlstm_conv, state_mlstm_C, state_mlstm_n,
        state_mlstm_m, p)
    return (y_prompt, y_sample, p_ssm_conv, p_ssm, p_ml_conv, p_ml_c, p_ml_n, p_ml_m,
            s_ssm_conv, s_ssm, s_ml_conv, s_ml_c, s_ml_n, s_ml_m)
```

```python
import functools

import jax
import jax.numpy as jnp
from jax import lax
from jax.experimental import pallas as pl
from jax.experimental.pallas import tpu as pltpu

F32 = jnp.float32
BF16 = jnp.bfloat16
RMS_EPS = 1e-6
CONV_W = 4
CHUNK = 64
LANES = 128
TAIL0 = 8 - (CONV_W - 1)
VMEM_LIMIT = 60 * 1024 * 1024

SSM_GROUPS = 4
SSM_HPG = 8
SSM_HEAD_DIM = 64
SSM_STATE = 128
ML_HEADS = 8
N_EXPERTS = 8


def _dot(a, b):
    return jnp.dot(a, b, preferred_element_type=F32)


def _dot_nt(a, b):
    return lax.dot_general(a, b, (((1,), (1,)), ((), ())), preferred_element_type=F32)


def _dot_tn(a, b):
    return lax.dot_general(a, b, (((0,), (0,)), ((), ())), preferred_element_type=F32)


def _split3(x):
    h1 = x.astype(BF16)
    r = x - h1.astype(F32)
    h2 = r.astype(BF16)
    r = r - h2.astype(F32)
    return h1, h2, r.astype(BF16)


def _dot3_l(sel, x):
    return sum(_dot(sel, p) for p in _split3(x))


def _dot3_r(x, sel):
    return sum(_dot(p, sel) for p in _split3(x))


def _transpose_rows(x, eye):
    return sum(_dot_nt(eye, p) for p in _split3(x))


def _sigmoid(x):
    return 1.0 / (1.0 + jnp.exp(-x))


def _silu(x):
    return x * _sigmoid(x)


def _softplus(x):
    return jnp.maximum(x, 0.0) + jnp.log1p(jnp.exp(-jnp.abs(x)))


def _rms(x, g):
    return x * lax.rsqrt(jnp.mean(x * x, axis=-1, keepdims=True) + RMS_EPS) * g


def _tri_mask(n):
    r = lax.broadcasted_iota(jnp.int32, (n, n), 0)
    c = lax.broadcasted_iota(jnp.int32, (n, n), 1)
    return r >= c


def _causal_conv(buf, tt, cw_ref, cb_ref):
    y = cb_ref[...]
    for k in range(CONV_W):
        y = y + buf[TAIL0 + k:TAIL0 + k + tt, :] * cw_ref[k:k + 1, :]
    return y


def _ada_kernel(c_ref, w_ref, b_ref, o_ref):
    ca = _silu(c_ref[...]).astype(BF16)
    o_ref[0] = _dot(ca, w_ref[0].astype(BF16)) + b_ref[0]


def _ada(c_all, w_ada, b_ada):
    depth, d, n = w_ada.shape
    bt = c_all.shape[0]
    tn = n // 4
    return pl.pallas_call(
        _ada_kernel,
        out_shape=jax.ShapeDtypeStruct((depth, bt, n), F32),
        grid=(depth, n // tn),
        in_specs=[pl.BlockSpec((bt, d), lambda i, j: (0, 0)),
                  pl.BlockSpec((1, d, tn), lambda i, j: (i, 0, j)),
                  pl.BlockSpec((1, 1, tn), lambda i, j: (i, 0, j))],
        out_specs=pl.BlockSpec((1, bt, tn), lambda i, j: (i, 0, j)),
        compiler_params=pltpu.CompilerParams(
            dimension_semantics=("arbitrary", "arbitrary"), vmem_limit_bytes=VMEM_LIMIT),
        name="ada",
    )(c_all, w_ada, b_ada.reshape(depth, 1, n))


def _const_spec(shape):
    nd = len(shape)
    return pl.BlockSpec(shape, lambda *_: (0,) * nd, pipeline_mode=pl.Buffered(1))


def _ssd_kernel(x_ref, mod_ref, nw_ref, conv0_ref, h0_ref, wz_ref, wx_ref, wdt_ref, cw_ref, cb_ref,
                dtb_ref, alog_ref, dfull_ref, gnw_ref, wout_ref, eye_ref, expand_ref,
                xo_ref, convo_ref, ho_ref,
                xbc_buf, act_buf, z_buf, dt_buf, y_buf, xw_buf, h_scr, *, tt, L):
    t = pl.program_id(1)
    inner = SSM_HPG * SSM_HEAD_DIM
    d_in = SSM_GROUPS * inner
    gn = SSM_GROUPS * SSM_STATE

    @pl.when(t == 0)
    def _():
        xbc_buf[TAIL0:8, :] = conv0_ref[0]
        h_scr[...] = h0_ref[0]

    x = x_ref[0]
    mod = mod_ref[0]
    hn = (_rms(x, nw_ref[...]) * (1.0 + mod[1:2]) + mod[0:1]).astype(BF16)
    z_buf[...] = _dot(hn, wz_ref[...])
    xbc_buf[8:8 + tt, :] = _dot(hn, wx_ref[...])
    dtr = _dot(hn, wdt_ref[...])
    act_buf[...] = _silu(_causal_conv(xbc_buf, tt, cw_ref, cb_ref))
    tail = xbc_buf[tt + TAIL0:tt + 8, :]
    xbc_buf[TAIL0:8, :] = tail
    convo_ref[0] = tail
    dt_buf[...] = _softplus(dtr + dtb_ref[...])
    a = -jnp.exp(alog_ref[...])
    tri = _tri_mask(L)
    tri_b = jnp.where(tri, 1.0, 0.0).astype(BF16)
    eye = eye_ref[...]

    def chunk(c, carry):
        r0 = pl.multiple_of(c * L, L)
        rows = pl.ds(r0, L)
        dtc = dt_buf[rows, :]
        cum = _dot3_l(tri_b, dtc * a)
        cum_t = _transpose_rows(cum, eye)
        dt_t = _transpose_rows(dtc, eye)
        cum_last = cum[L - 1:L, :]
        wend = jnp.exp(cum_last - cum) * dtc
        dec_last = jnp.exp(_dot3_r(cum[L - 8:L, :], expand_ref[...])[7:8, :])
        for g in range(SSM_GROUPS):
            bg = act_buf[rows, d_in + g * SSM_STATE:d_in + (g + 1) * SSM_STATE].astype(BF16)
            cg = act_buf[rows, d_in + gn + g * SSM_STATE:d_in + gn + (g + 1) * SSM_STATE].astype(BF16)
            cb = _dot_nt(cg, bg)
            hg = h_scr[g]
            y_int = _dot(cg, hg.astype(BF16))
            for r in range(SSM_HPG):
                hd = g * SSM_HPG + r
                lo, hi = hd * SSM_HEAD_DIM, (hd + 1) * SSM_HEAD_DIM
                ccol = cum[:, hd:hd + 1]
                dec = jnp.exp(jnp.where(tri, ccol - cum_t[hd:hd + 1, :], -jnp.inf))
                wm = cb * dec * dt_t[hd:hd + 1, :]
                xr = act_buf[rows, lo:hi]
                yr = _dot(wm.astype(BF16), xr.astype(BF16))
                yr = yr + y_int[:, r * SSM_HEAD_DIM:(r + 1) * SSM_HEAD_DIM] * jnp.exp(ccol)
                y_buf[rows, lo:hi] = yr
                xw_buf[:, r * SSM_HEAD_DIM:(r + 1) * SSM_HEAD_DIM] = xr * wend[:, hd:hd + 1]
            h_scr[g] = hg * dec_last[:, g * inner:(g + 1) * inner] + _dot_tn(bg, xw_buf[...].astype(BF16))
        return carry

    lax.fori_loop(0, tt // L, chunk, 0)

    y = y_buf[...] + dfull_ref[...] * act_buf[:, :d_in]
    y = y * _silu(z_buf[...])
    gnw = gnw_ref[...]
    parts = []
    for g in range(SSM_GROUPS):
        parts.append(_rms(y[:, g * inner:(g + 1) * inner], gnw[:, g * inner:(g + 1) * inner]).astype(BF16))
    yn = jnp.concatenate(parts, axis=-1)
    xo_ref[0] = x + mod[2:3] * _dot(yn, wout_ref[...])

    @pl.when(t == pl.num_programs(1) - 1)
    def _():
        ho_ref[0] = h_scr[...]


def _ssd_layer(x, mod, nw, conv0, h0, p, tt, L):
    bsz, t, d = x.shape
    d_in = SSM_GROUPS * SSM_HPG * SSM_HEAD_DIM
    cdim = p['conv_w'].shape[1]
    nt = t // tt
    blk = lambda b, i: (b, 0, 0)
    in_specs = [
        pl.BlockSpec((1, tt, d), lambda b, i: (b, i, 0)),
        pl.BlockSpec((1, 6, d), blk),
        _const_spec((1, d)),
        pl.BlockSpec((1, CONV_W - 1, cdim), blk),
        pl.BlockSpec((1, SSM_GROUPS, SSM_STATE, SSM_HPG * SSM_HEAD_DIM), lambda b, i: (b, 0, 0, 0)),
        _const_spec(p['wz'].shape), _const_spec(p['wx'].shape), _const_spec(p['wdt'].shape),
        _const_spec(p['conv_w'].shape), _const_spec(p['conv_b'].shape),
        _const_spec(p['dt_bias'].shape), _const_spec(p['a_log'].shape), _const_spec(p['d_full'].shape),
        _const_spec(p['norm'].shape), _const_spec(p['w_out'].shape),
        _const_spec(p['eye'].shape), _const_spec(p['expand'].shape),
    ]
    out_shape = (jax.ShapeDtypeStruct((bsz, t, d), F32),
                 jax.ShapeDtypeStruct((bsz, CONV_W - 1, cdim), F32),
                 jax.ShapeDtypeStruct(h0.shape, F32))
    out_specs = (pl.BlockSpec((1, tt, d), lambda b, i: (b, i, 0)),
                 pl.BlockSpec((1, CONV_W - 1, cdim), blk),
                 pl.BlockSpec((1, SSM_GROUPS, SSM_STATE, SSM_HPG * SSM_HEAD_DIM), lambda b, i: (b, 0, 0, 0)))
    scratch = [pltpu.VMEM((8 + tt, cdim), F32), pltpu.VMEM((tt, cdim), F32), pltpu.VMEM((tt, d_in), F32),
               pltpu.VMEM((tt, LANES), F32), pltpu.VMEM((tt, d_in), F32),
               pltpu.VMEM((L, SSM_HPG * SSM_HEAD_DIM), F32),
               pltpu.VMEM((SSM_GROUPS, SSM_STATE, SSM_HPG * SSM_HEAD_DIM), F32)]
    return pl.pallas_call(
        functools.partial(_ssd_kernel, tt=tt, L=L),
        out_shape=out_shape, grid=(bsz, nt), in_specs=in_specs, out_specs=out_specs,
        scratch_shapes=scratch,
        compiler_params=pltpu.CompilerParams(
            dimension_semantics=("arbitrary", "arbitrary"), vmem_limit_bytes=VMEM_LIMIT),
        name="ssd_mixer",
    )(x, mod, nw, conv0, h0, p['wz'], p['wx'], p['wdt'], p['conv_w'], p['conv_b'], p['dt_bias'],
      p['a_log'], p['d_full'], p['norm'], p['w_out'], p['eye'], p['expand'])


def _ffn_kernel(x_ref, mod_ref, nw_ref, wg_ref, wu_ref, wd_ref, o_ref, *, nchunk):
    x = x_ref[0]
    mod = mod_ref[0]
    hn = (_rms(x, nw_ref[...]) * (1.0 + mod[4:5]) + mod[3:4]).astype(BF16)
    fc = wg_ref.shape[1] // nchunk
    acc = None
    for j in range(nchunk):
        g = _dot(hn, wg_ref[:, j * fc:(j + 1) * fc])
        u = _dot(hn, wu_ref[:, j * fc:(j + 1) * fc])
        part = _dot((_silu(g) * u).astype(BF16), wd_ref[j * fc:(j + 1) * fc, :])
        acc = part if acc is None else acc + part
    o_ref[0] = x + mod[5:6] * acc


def _ffn_layer(x, mod, nw, p, tm):
    bsz, t, d = x.shape
    return pl.pallas_call(
        functools.partial(_ffn_kernel, nchunk=2),
        out_shape=jax.ShapeDtypeStruct(x.shape, F32),
        grid=(bsz, t // tm),
        in_specs=[pl.BlockSpec((1, tm, d), lambda b, i: (b, i, 0)),
                  pl.BlockSpec((1, 6, d), lambda b, i: (b, 0, 0)),
                  _const_spec((1, d)),
                  _const_spec(p['w_gate'].shape), _const_spec(p['w_up'].shape), _const_spec(p['w_down'].shape)],
        out_specs=pl.BlockSpec((1, tm, d), lambda b, i: (b, i, 0)),
        compiler_params=pltpu.CompilerParams(
            dimension_semantics=("arbitrary", "arbitrary"), vmem_limit_bytes=VMEM_LIMIT),
        name="ffn",
    )(x, mod, nw, p['w_gate'], p['w_up'], p['w_down'])


def _mlstm_kernel(x_ref, mod_ref, nw_ref, conv0_ref, c0_ref, n0_ref, m0_ref,
                  wxm_ref, wo_ref, cw_ref, cb_ref, wq_ref, wk_ref, wv_ref, wgq_ref, wgk_ref, wgv_ref, bg_ref,
                  gnw_ref, skip_ref, wout_ref, eyei_ref, eyef_ref,
                  xo_ref, convo_ref, co_ref, no_ref, mo_ref,
                  xm_buf, xc_buf, q_buf, k_buf, v_buf, op_buf, gi_buf, lf_buf, hh_buf, c_scr, n_scr, m_scr,
                  *, tt, L):
    t = pl.program_id(1)
    hd_dim = wq_ref.shape[1]
    k_scale = hd_dim ** -0.5

    @pl.when(t == 0)
    def _():
        xm_buf[TAIL0:8, :] = conv0_ref[0]
        c_scr[...] = c0_ref[0]
        n_scr[...] = n0_ref[0]
        m_scr[...] = m0_ref[0]

    x = x_ref[0]
    mod = mod_ref[0]
    hn = (_rms(x, nw_ref[...]) * (1.0 + mod[1:2]) + mod[0:1]).astype(BF16)
    xm_buf[8:8 + tt, :] = _dot(hn, wxm_ref[...])
    op_buf[...] = _dot(hn, wo_ref[...])
    xc_buf[...] = _silu(_causal_conv(xm_buf, tt, cw_ref, cb_ref))
    tail = xm_buf[tt + TAIL0:tt + 8, :]
    convo_ref[0] = tail

    gates = bg_ref[...]
    for h in range(ML_HEADS):
        sl = slice(h * hd_dim, (h + 1) * hd_dim)
        xc_h = xc_buf[:, sl].astype(BF16)
        xm_h = xm_buf[8:8 + tt, sl].astype(BF16)
        q = _dot(xc_h, wq_ref[h])
        k = _dot(xc_h, wk_ref[h])
        v = _dot(xm_h, wv_ref[h])
        gates = gates + _dot(q.astype(BF16), wgq_ref[sl, :]) + _dot(k.astype(BF16), wgk_ref[sl, :]) \
            + _dot(v.astype(BF16), wgv_ref[sl, :])
        q_buf[:, sl] = q
        k_buf[:, sl] = k * k_scale
        v_buf[:, sl] = v
    xm_buf[TAIL0:8, :] = tail
    gi_buf[...] = gates
    lf_buf[...] = jnp.minimum(gates, 0.0) - jnp.log1p(jnp.exp(-jnp.abs(gates)))
    tri = _tri_mask(L)
    tri_b = jnp.where(tri, 1.0, 0.0).astype(BF16)
    lane = lax.broadcasted_iota(jnp.int32, (1, LANES), 1)

    def chunk(c, carry):
        r0 = pl.multiple_of(c * L, L)
        rows = pl.ds(r0, L)
        gi = gi_buf[rows, :]
        bcum = _dot3_l(tri_b, lf_buf[rows, :])
        li_t = _transpose_rows(gi, eyei_ref[...])
        b_t = _transpose_rows(bcum, eyef_ref[...])
        m_prev = m_scr[...]
        m_next = m_prev
        for h in range(ML_HEADS):
            sl = slice(h * hd_dim, (h + 1) * hd_dim)
            bcol = bcum[:, ML_HEADS + h:ML_HEADS + h + 1]
            licol = gi[:, h:h + 1]
            mp = m_prev[:, h:h + 1]
            dmat = jnp.where(tri, bcol - b_t[h:h + 1, :] + li_t[h:h + 1, :], -jnp.inf)
            a_inter = bcol + mp
            m_t = jnp.maximum(a_inter, jnp.max(dmat, axis=1, keepdims=True))
            qh = q_buf[rows, sl]
            kh = k_buf[rows, sl]
            vh = v_buf[rows, sl].astype(BF16)
            qb = qh.astype(BF16)
            s_mat = jnp.exp(dmat - m_t) * _dot_nt(qb, kh.astype(BF16))
            w_inter = jnp.exp(a_inter - m_t)
            ch = c_scr[h]
            nh = n_scr[h:h + 1, :]
            num = _dot(s_mat.astype(BF16), vh) + w_inter * _dot(qb, ch.astype(BF16))
            den = jnp.sum(s_mat, axis=1, keepdims=True) + w_inter * jnp.sum(qh * nh, axis=1, keepdims=True)
            hh_buf[rows, sl] = num / jnp.maximum(jnp.abs(den), jnp.exp(-m_t))
            m_new = m_t[L - 1:L, :]
            b_last = bcol[L - 1:L, :]
            w_end = jnp.exp(b_last - bcol + licol - m_new)
            w_old = jnp.exp(b_last + mp - m_new)
            kw = kh * w_end
            c_scr[h] = w_old * ch + _dot_tn(kw.astype(BF16), vh)
            n_scr[h:h + 1, :] = w_old * nh + jnp.sum(kw, axis=0, keepdims=True)
            m_next = jnp.where(lane == h, m_new, m_next)
        m_scr[...] = m_next
        return carry

    lax.fori_loop(0, tt // L, chunk, 0)

    gnw = gnw_ref[...]
    skip = skip_ref[...]
    parts = []
    for h in range(ML_HEADS):
        sl = slice(h * hd_dim, (h + 1) * hd_dim)
        hn_h = _rms(hh_buf[:, sl], gnw[:, sl])
        parts.append(((hn_h + skip[:, sl] * xc_buf[:, sl]) * _sigmoid(op_buf[:, sl])).astype(BF16))
    out = jnp.concatenate(parts, axis=-1)
    xo_ref[0] = x + mod[2:3] * _dot(out, wout_ref[...])

    @pl.when(t == pl.num_programs(1) - 1)
    def _():
        co_ref[0] = c_scr[...]
        no_ref[0] = n_scr[...]
        mo_ref[0] = m_scr[...]


def _mlstm_layer(x, mod, nw, conv0, c0, n0, m0, p, tt, L):
    bsz, t, d = x.shape
    inner = p['conv_w'].shape[1]
    hd = inner // ML_HEADS
    nt = t // tt
    blk = lambda b, i: (b, 0, 0)
    blk4 = lambda b, i: (b, 0, 0, 0)
    names = ['wxm', 'wo', 'conv_w', 'conv_b', 'w_q', 'w_k', 'w_v', 'wgq', 'wgk', 'wgv', 'bg',
             'norm', 'skip', 'w_out', 'eye_i', 'eye_f']
    in_specs = [
        pl.BlockSpec((1, tt, d), lambda b, i: (b, i, 0)),
        pl.BlockSpec((1, 6, d), blk),
        _const_spec((1, d)),
        pl.BlockSpec((1, CONV_W - 1, inner), blk),
        pl.BlockSpec((1, ML_HEADS, hd, hd), blk4),
        pl.BlockSpec((1, ML_HEADS, hd), blk),
        pl.BlockSpec((1, 1, LANES), blk),
    ] + [_const_spec(p[n].shape) for n in names]
    out_shape = (jax.ShapeDtypeStruct((bsz, t, d), F32),
                 jax.ShapeDtypeStruct((bsz, CONV_W - 1, inner), F32),
                 jax.ShapeDtypeStruct((bsz, ML_HEADS, hd, hd), F32),
                 jax.ShapeDtypeStruct((bsz, ML_HEADS, hd), F32),
                 jax.ShapeDtypeStruct((bsz, 1, LANES), F32))
    out_specs = (pl.BlockSpec((1, tt, d), lambda b, i: (b, i, 0)),
                 pl.BlockSpec((1, CONV_W - 1, inner), blk),
                 pl.BlockSpec((1, ML_HEADS, hd, hd), blk4),
                 pl.BlockSpec((1, ML_HEADS, hd), blk),
                 pl.BlockSpec((1, 1, LANES), blk))
    scratch = ([pltpu.VMEM((8 + tt, inner), F32)] + [pltpu.VMEM((tt, inner), F32)] * 5
               + [pltpu.VMEM((tt, LANES), F32)] * 2 + [pltpu.VMEM((tt, inner), F32)]
               + [pltpu.VMEM((ML_HEADS, hd, hd), F32), pltpu.VMEM((ML_HEADS, hd), F32),
                  pltpu.VMEM((1, LANES), F32)])
    return pl.pallas_call(
        functools.partial(_mlstm_kernel, tt=tt, L=L),
        out_shape=out_shape, grid=(bsz, nt), in_specs=in_specs, out_specs=out_specs,
        scratch_shapes=scratch,
        compiler_params=pltpu.CompilerParams(
            dimension_semantics=("arbitrary", "arbitrary"), vmem_limit_bytes=VMEM_LIMIT),
        name="mlstm_mixer",
    )(x, mod, nw, conv0, c0, n0, m0, *[p[n] for n in names])


def _router_kernel(x_ref, mod_ref, nw_ref, wr_ref, br_ref, eye_ref,
                   h_ref, rc_ref, rr_ref, comb_ref, cnt_ref, *, tm):
    x = x_ref[0]
    mod = mod_ref[0]
    h = _rms(x, nw_ref[...]) * (1.0 + mod[4:5]) + mod[3:4]
    h_ref[0] = h.astype(BF16)
    h1, h2, h3 = _split3(h)
    w1, w2, w3 = _split3(wr_ref[...])
    logits = (_dot(h1, w1) + (_dot(h1, w2) + _dot(h2, w1))
              + (_dot(h1, w3) + _dot(h2, w2) + _dot(h3, w1))) + br_ref[...]
    lane = lax.broadcasted_iota(jnp.int32, (tm, LANES), 1)
    lg = jnp.where(lane < N_EXPERTS, logits, -jnp.inf)
    m1 = jnp.max(lg, axis=1, keepdims=True)
    i1 = jnp.min(jnp.where(lg == m1, lane, LANES), axis=1, keepdims=True)
    lg2 = jnp.where(lane == i1, -jnp.inf, lg)
    m2 = jnp.max(lg2, axis=1, keepdims=True)
    i2 = jnp.min(jnp.where(lg2 == m2, lane, LANES), axis=1, keepdims=True)
    e2 = jnp.exp(m2 - m1)
    w_top1 = 1.0 / (1.0 + e2)
    w_top2 = e2 / (1.0 + e2)
    comb_ref[0] = jnp.where(lane == i1, w_top1, 0.0) + jnp.where(lane == i2, w_top2, 0.0)
    sel = jnp.logical_or(lane == i1, lane == i2)
    mask = jnp.where(sel, 1.0, 0.0).astype(BF16)
    r = lax.broadcasted_iota(jnp.int32, (tm, tm), 0)
    c = lax.broadcasted_iota(jnp.int32, (tm, tm), 1)
    before_col = jnp.where(c < r, 1.0, 0.0).astype(BF16)
    before_row = jnp.where(r < c, 1.0, 0.0).astype(BF16)
    rank_c = _dot(before_col, mask)
    rc_ref[0] = jnp.where(sel, rank_c, -1.0)
    mask_t = _dot_nt(eye_ref[...], mask)
    rank_r = _dot(mask_t.astype(BF16), before_row)
    rr_ref[0, 0] = jnp.where(mask_t > 0.5, rank_r, -1.0)
    cnt = jnp.sum(mask.astype(F32), axis=0, keepdims=True)
    cnt_ref[0, 0] = jnp.broadcast_to(cnt, (8, LANES)).astype(jnp.int32)


def _router(x, mod, nw, p, tm):
    bsz, t, d = x.shape
    nt = t // tm
    ti = lambda b, i: (b, i, 0)
    t4 = lambda b, i: (b, i, 0, 0)
    out_shape = (jax.ShapeDtypeStruct((bsz, t, d), BF16),
                 jax.ShapeDtypeStruct((bsz, t, LANES), F32),
                 jax.ShapeDtypeStruct((bsz, nt, 16, tm), F32),
                 jax.ShapeDtypeStruct((bsz, t, LANES), F32),
                 jax.ShapeDtypeStruct((bsz, nt, 8, LANES), jnp.int32))
    out_specs = (pl.BlockSpec((1, tm, d), ti), pl.BlockSpec((1, tm, LANES), ti),
                 pl.BlockSpec((1, 1, 16, tm), t4), pl.BlockSpec((1, tm, LANES), ti),
                 pl.BlockSpec((1, 1, 8, LANES), t4))
    return pl.pallas_call(
        functools.partial(_router_kernel, tm=tm),
        out_shape=out_shape, grid=(bsz, nt),
        in_specs=[pl.BlockSpec((1, tm, d), ti), pl.BlockSpec((1, 6, d), lambda b, i: (b, 0, 0)),
                  _const_spec((1, d)), _const_spec(p['w_router'].shape), _const_spec(p['b_router'].shape),
                  _const_spec(p['eye16'].shape)],
        out_specs=out_specs,
        compiler_params=pltpu.CompilerParams(
            dimension_semantics=("arbitrary", "arbitrary"), vmem_limit_bytes=VMEM_LIMIT),
        name="router",
    )(x, mod, nw, p['w_router'], p['b_router'], p['eye16'])


def _expert_kernel(cnt_ref, acc_ref, h_ref, rc_ref, rr_ref, comb_ref, mod_ref, wg_ref, wu_ref, wd_ref, fw_ref,
                   o_ref, *, e, tm, rows, final):
    idx = pl.program_id(0) * pl.num_programs(1) + pl.program_id(1)
    cnt = cnt_ref[idx]
    o_ref[0] = acc_ref[0]
    scale = mod_ref[0][5:6] * comb_ref[0][:, e:e + 1]

    for c in range(tm // rows):
        @pl.when(cnt > c * rows)
        def _():
            rr = rr_ref[0, 0][e:e + 1, :]
            slot_r = lax.broadcasted_iota(jnp.int32, (rows, tm), 0).astype(F32) + float(c * rows)
            gather = jnp.where(slot_r == rr, 1.0, 0.0).astype(BF16)
            xc = _dot(gather, h_ref[0]).astype(BF16)
            g = _dot(xc, wg_ref[...])
            u = _dot(xc, wu_ref[...])
            yc = _dot((_silu(g) * u).astype(BF16), wd_ref[...])
            rc = rc_ref[0][:, e:e + 1]
            slot_c = lax.broadcasted_iota(jnp.int32, (tm, rows), 1).astype(F32) + float(c * rows)
            scatter = jnp.where(slot_c == rc, 1.0, 0.0).astype(BF16)
            o_ref[0] = o_ref[0] + scale * _dot(scatter, yc.astype(BF16))

    if final:
        o_ref[0] = _rms(o_ref[0], fw_ref[...])


def _expert(e, counts_e, acc, h, rc, rr, comb, mod, wg, wu, wd, fw, tm, final):
    bsz, t, d = acc.shape
    nt = t // tm
    rows = min(tm, 128)
    ti = lambda b, i, cnt: (b, i, 0)
    grid_spec = pltpu.PrefetchScalarGridSpec(
        num_scalar_prefetch=1, grid=(bsz, nt),
        in_specs=[pl.BlockSpec((1, tm, d), ti), pl.BlockSpec((1, tm, d), ti),
                  pl.BlockSpec((1, tm, LANES), ti), pl.BlockSpec((1, 1, 16, tm), lambda b, i, cnt: (b, i, 0, 0)),
                  pl.BlockSpec((1, tm, LANES), ti), pl.BlockSpec((1, 6, d), lambda b, i, cnt: (b, 0, 0)),
                  _const_spec(wg.shape), _const_spec(wu.shape), _const_spec(wd.shape), _const_spec(fw.shape)],
        out_specs=pl.BlockSpec((1, tm, d), ti))
    return pl.pallas_call(
        functools.partial(_expert_kernel, e=e, tm=tm, rows=rows, final=final),
        out_shape=jax.ShapeDtypeStruct(acc.shape, F32),
        grid_spec=grid_spec,
        compiler_params=pltpu.CompilerParams(
            dimension_semantics=("arbitrary", "arbitrary"), vmem_limit_bytes=VMEM_LIMIT),
        name="expert",
    )(counts_e, acc, h, rc, rr, comb, mod, wg, wu, wd, fw)


def _moe_layer(x, mod, nw, p, fw, tm):
    bsz, t, _ = x.shape
    h, rc, rr, comb, cnt = _router(x, mod, nw, p, tm)
    counts = jnp.transpose(cnt[:, :, 0, :N_EXPERTS].reshape(bsz * (t // tm), N_EXPERTS))
    acc = x
    for e in range(N_EXPERTS):
        acc = _expert(e, counts[e], acc, h, rc, rr, comb, mod, p['w_gate'][e], p['w_up'][e], p['w_down'][e],
                      fw, tm, final=(e == N_EXPERTS - 1))
    return acc


def _pad_lanes(a):
    return jnp.pad(a, [(0, 0)] * (a.ndim - 1) + [(0, LANES - a.shape[-1])])


def _one_hot_rows(n, offset):
    r = lax.broadcasted_iota(jnp.int32, (n, LANES), 0)
    c = lax.broadcasted_iota(jnp.int32, (n, LANES), 1)
    return (c == r + offset).astype(BF16)


def _prep_ssm(w_in, conv_w, conv_b, dt_bias, a_log, d_skip, norm_w, w_out):
    d_in = SSM_GROUPS * SSM_HPG * SSM_HEAD_DIM
    cdim = conv_w.shape[1]
    heads = SSM_GROUPS * SSM_HPG
    r = lax.broadcasted_iota(jnp.int32, (LANES, d_in), 0)
    c = lax.broadcasted_iota(jnp.int32, (LANES, d_in), 1)
    return dict(
        wz=w_in[:, :d_in].astype(BF16), wx=w_in[:, d_in:d_in + cdim].astype(BF16),
        wdt=_pad_lanes(w_in[:, d_in + cdim:]).astype(BF16),
        conv_w=conv_w, conv_b=conv_b.reshape(1, cdim),
        dt_bias=_pad_lanes(dt_bias.reshape(1, heads)), a_log=_pad_lanes(a_log.reshape(1, heads)),
        d_full=jnp.repeat(d_skip, SSM_HEAD_DIM).reshape(1, d_in),
        norm=norm_w.reshape(1, d_in), w_out=w_out.astype(BF16),
        eye=_one_hot_rows(heads, 0), expand=(c // SSM_HEAD_DIM == r).astype(BF16))


def _prep_mlstm(w_in, conv_w, conv_b, w_q, w_k, w_v, w_ig, b_ig, w_fg, b_fg, norm_w, skip, w_out):
    inner = conv_w.shape[1]
    hd = inner // ML_HEADS
    wg = jnp.concatenate([w_ig, w_fg], axis=-1).reshape(ML_HEADS, 3, hd, 2 * ML_HEADS)
    part = lambda j: _pad_lanes(wg[:, j].reshape(inner, 2 * ML_HEADS)).astype(BF16)
    return dict(
        wxm=w_in[:, :inner].astype(BF16), wo=w_in[:, inner:].astype(BF16),
        conv_w=conv_w, conv_b=conv_b.reshape(1, inner),
        w_q=w_q.astype(BF16), w_k=w_k.astype(BF16), w_v=w_v.astype(BF16),
        wgq=part(0), wgk=part(1), wgv=part(2),
        bg=_pad_lanes(jnp.concatenate([b_ig, b_fg]).reshape(1, 2 * ML_HEADS)),
        norm=norm_w.reshape(1, inner), skip=skip.reshape(1, inner), w_out=w_out.astype(BF16),
        eye_i=_one_hot_rows(ML_HEADS, 0), eye_f=_one_hot_rows(ML_HEADS, ML_HEADS))


def _trunk(x, mod, ssm_conv, ssm_state, ml_conv, ml_c, ml_n, ml_m, p):
    bsz, t, d = x.shape
    L = CHUNK if t % CHUNK == 0 else t
    tt = min(t, 256)
    tm = min(t, 512)
    n_heads = SSM_GROUPS * SSM_HPG
    h0 = ssm_state.reshape(bsz, SSM_GROUPS, SSM_HPG, SSM_HEAD_DIM, SSM_STATE)
    h0 = h0.transpose(0, 1, 4, 2, 3).reshape(bsz, SSM_GROUPS, SSM_STATE, SSM_HPG * SSM_HEAD_DIM)
    x, conv_s, h_s = _ssd_layer(x, mod[0], p['norm_mix'][0], ssm_conv, h0, p['ssm'], tt, L)
    h_s = h_s.reshape(bsz, SSM_GROUPS, SSM_STATE, SSM_HPG, SSM_HEAD_DIM).transpose(0, 1, 3, 4, 2)
    h_s = h_s.reshape(bsz, n_heads, SSM_HEAD_DIM, SSM_STATE)
    x = _ffn_layer(x, mod[0], p['norm_ffn'][0], p['ffn'], tm)
    m0 = _pad_lanes(ml_m).reshape(bsz, 1, LANES)
    x, conv_m, c_m, n_m, m_m = _mlstm_layer(x, mod[1], p['norm_mix'][1], ml_conv, ml_c, ml_n, m0, p['ml'], tt, L)
    y = _moe_layer(x, mod[1], p['norm_ffn'][1], p['moe'], p['norm_final'], tm)
    return (y, conv_s[None], h_s[None], conv_m[None], c_m[None], n_m[None], m_m[:, 0, :ML_HEADS][None])


def kernel(x_prompt, x_sample, c_prompt, c_sample, state_ssm_conv, state_ssm, state_mlstm_conv, state_mlstm_C, state_mlstm_n, state_mlstm_m, w_ada, b_ada, norm_mix, norm_ffn, norm_final, ssm_w_in, ssm_conv_w, ssm_conv_b, ssm_dt_bias, ssm_a_log, ssm_d, ssm_norm, ssm_w_out, ml_w_in, ml_conv_w, ml_conv_b, ml_w_q, ml_w_k, ml_w_v, ml_w_igate, ml_b_igate, ml_w_fgate, ml_b_fgate, ml_norm, ml_skip, ml_w_out, ffn_w_gate, ffn_w_up, ffn_w_down, moe_w_router, moe_b_router, moe_w_gate, moe_w_up, moe_w_down):
    depth, d, _ = w_ada.shape
    assert depth == 2 and state_ssm.shape[0] == 1 and state_mlstm_C.shape[0] == 1
    bp, bs = x_prompt.shape[0], x_sample.shape[0]
    p = dict(
        norm_mix=norm_mix.reshape(depth, 1, d), norm_ffn=norm_ffn.reshape(depth, 1, d),
        norm_final=norm_final.reshape(1, d),
        ssm=_prep_ssm(ssm_w_in[0], ssm_conv_w[0], ssm_conv_b[0], ssm_dt_bias[0], ssm_a_log[0], ssm_d[0],
                      ssm_norm[0], ssm_w_out[0]),
        ml=_prep_mlstm(ml_w_in[0], ml_conv_w[0], ml_conv_b[0], ml_w_q[0], ml_w_k[0], ml_w_v[0], ml_w_igate[0],
                       ml_b_igate[0], ml_w_fgate[0], ml_b_fgate[0], ml_norm[0], ml_skip[0], ml_w_out[0]),
        ffn=dict(w_gate=ffn_w_gate[0].astype(BF16), w_up=ffn_w_up[0].astype(BF16),
                 w_down=ffn_w_down[0].astype(BF16)),
        moe=dict(w_router=_pad_lanes(moe_w_router[0]), b_router=_pad_lanes(moe_b_router[0].reshape(1, -1)),
                 eye16=_one_hot_rows(16, 0),
                 w_gate=moe_w_gate[0].astype(BF16), w_up=moe_w_up[0].astype(BF16),
                 w_down=moe_w_down[0].astype(BF16)))
    mod = _ada(jnp.concatenate([c_prompt, c_sample], axis=0), w_ada, b_ada)
    mod = mod.reshape(depth, bp + bs, 6, d)

    f = F32
    zeros = lambda a, b: jnp.zeros((b,) + a.shape[2:], f)
    out_p = _trunk(x_prompt, mod[:, :bp], zeros(state_ssm_conv, bp), zeros(state_ssm, bp),
                   zeros(state_mlstm_conv, bp), zeros(state_mlstm_C, bp), zeros(state_mlstm_n, bp),
                   zeros(state_mlstm_m, bp), p)
    out_s = _trunk(x_sample, mod[:, bp:], state_ssm_conv[0], state_ssm[0], state_mlstm_conv[0],
                   state_mlstm_C[0], state_mlstm_n[0], state_mlstm_m[0], p)
    return (out_p[0], out_s[0]) + tuple(out_p[1:]) + tuple(out_s[1:])
```

```python
import functools

import jax
import jax.numpy as jnp
from jax import lax
from jax.experimental import pallas as pl
from jax.experimental.pallas import tpu as pltpu

F32 = jnp.float32
BF16 = jnp.bfloat16
RMS_EPS = 1e-6
CONV_W = 4
LANES = 128
SCAN_CHUNK = LANES
TAIL0 = 8 - (CONV_W - 1)
VMEM_LIMIT = 60 * 1024 * 1024

SSM_GROUPS = 4
SSM_HPG = 8
SSM_HEAD_DIM = 64
SSM_STATE = 128
ML_HEADS = 8
N_EXPERTS = 8


def _dot(a, b):
    return jnp.dot(a, b, preferred_element_type=F32)


def _dot_nt(a, b):
    return lax.dot_general(a, b, (((1,), (1,)), ((), ())), preferred_element_type=F32)


def _dot_tn(a, b):
    return lax.dot_general(a, b, (((0,), (0,)), ((), ())), preferred_element_type=F32)


def _split3(x):
    h1 = x.astype(BF16)
    r = x - h1.astype(F32)
    h2 = r.astype(BF16)
    r = r - h2.astype(F32)
    return h1, h2, r.astype(BF16)


def _dot3_l(sel, x):
    return sum(_dot(sel, p) for p in _split3(x))


def _dot3_r(x, sel):
    return sum(_dot(p, sel) for p in _split3(x))


def _transpose_rows(x, eye):
    return sum(_dot_nt(eye, p) for p in _split3(x))


def _sigmoid(x):
    return 1.0 / (1.0 + jnp.exp(-x))


def _silu(x):
    return x * _sigmoid(x)


def _softplus(x):
    return jnp.maximum(x, 0.0) + jnp.log1p(jnp.exp(-jnp.abs(x)))


def _rms(x, g):
    return x * lax.rsqrt(jnp.mean(x * x, axis=-1, keepdims=True) + RMS_EPS) * g


def _tri_mask(n):
    r = lax.broadcasted_iota(jnp.int32, (n, n), 0)
    c = lax.broadcasted_iota(jnp.int32, (n, n), 1)
    return r >= c


def _causal_conv_silu(buf, dst, tt, cw_ref, cb_ref):
    xn = buf[8:8 + tt, :]
    y = cb_ref[...] + xn * cw_ref[CONV_W - 1:CONV_W, :]
    for j in range(1, CONV_W):
        y = y + pltpu.roll(xn, j, axis=0) * cw_ref[CONV_W - 1 - j:CONV_W - j, :]
    dst[...] = _silu(y)
    head = cb_ref[...]
    for k in range(CONV_W):
        head = head + buf[TAIL0 + k:TAIL0 + k + 8, :] * cw_ref[k:k + 1, :]
    dst[0:8, :] = _silu(head)


def _ada_kernel(c_ref, w_ref, b_ref, o_ref):
    ca = _silu(c_ref[...]).astype(BF16)
    o_ref[0] = _dot(ca, w_ref[0].astype(BF16)) + b_ref[0]


def _ada(c_all, w_ada, b_ada):
    depth, d, n = w_ada.shape
    bt = c_all.shape[0]
    tn = n // 4
    return pl.pallas_call(
        _ada_kernel,
        out_shape=jax.ShapeDtypeStruct((depth, bt, n), F32),
        grid=(depth, n // tn),
        in_specs=[pl.BlockSpec((bt, d), lambda i, j: (0, 0)),
                  pl.BlockSpec((1, d, tn), lambda i, j: (i, 0, j)),
                  pl.BlockSpec((1, 1, tn), lambda i, j: (i, 0, j))],
        out_specs=pl.BlockSpec((1, bt, tn), lambda i, j: (i, 0, j)),
        compiler_params=pltpu.CompilerParams(
            dimension_semantics=("arbitrary", "arbitrary"), vmem_limit_bytes=VMEM_LIMIT),
        name="ada",
    )(c_all, w_ada, b_ada.reshape(depth, 1, n))


def _const_spec(shape):
    nd = len(shape)
    return pl.BlockSpec(shape, lambda *_: (0,) * nd, pipeline_mode=pl.Buffered(1))


def _ssd_kernel(x_ref, mod_ref, nw_ref, conv0_ref, h0_ref, wz_ref, wx_ref, wdt_ref, cw_ref, cb_ref,
                dtb_ref, alog_ref, dfull_ref, gnw_ref, wout_ref, eye_ref, expand_ref,
                xo_ref, convo_ref, ho_ref,
                xbc_buf, act_buf, z_buf, dt_buf, y_buf, xw_buf, h_scr, *, tt, L):
    t = pl.program_id(1)
    inner = SSM_HPG * SSM_HEAD_DIM
    d_in = SSM_GROUPS * inner
    gn = SSM_GROUPS * SSM_STATE

    @pl.when(t == 0)
    def _():
        xbc_buf[TAIL0:8, :] = conv0_ref[0]
        h_scr[...] = h0_ref[0]

    x = x_ref[0]
    mod = mod_ref[0]
    hn = (_rms(x, nw_ref[...]) * (1.0 + mod[1:2]) + mod[0:1]).astype(BF16)
    z_buf[...] = _dot(hn, wz_ref[...])
    xbc_buf[8:8 + tt, :] = _dot(hn, wx_ref[...])
    dtr = _dot(hn, wdt_ref[...])
    _causal_conv_silu(xbc_buf, act_buf, tt, cw_ref, cb_ref)
    tail = xbc_buf[tt + TAIL0:tt + 8, :]
    xbc_buf[TAIL0:8, :] = tail
    convo_ref[0] = tail
    dt_buf[...] = _softplus(dtr + dtb_ref[...])
    a = -jnp.exp(alog_ref[...])
    tri = _tri_mask(L)
    tri_b = jnp.where(tri, 1.0, 0.0).astype(BF16)
    eye = eye_ref[...]
    left = lax.broadcasted_iota(jnp.int32, (1, LANES), 1) < SSM_HEAD_DIM

    def chunk(c, carry):
        r0 = pl.multiple_of(c * L, L)
        rows = pl.ds(r0, L)
        dtc = dt_buf[rows, :]
        cum = _dot3_l(tri_b, dtc * a)
        cum_t = _transpose_rows(cum, eye)
        dt_t = _transpose_rows(dtc, eye)
        cum_last = cum[L - 1:L, :]
        wend = jnp.exp(cum_last - cum) * dtc
        dec_last = jnp.exp(_dot3_r(cum[L - 8:L, :], expand_ref[...])[7:8, :])
        for g in range(SSM_GROUPS):
            bg = act_buf[rows, d_in + g * SSM_STATE:d_in + (g + 1) * SSM_STATE].astype(BF16)
            cg = act_buf[rows, d_in + gn + g * SSM_STATE:d_in + gn + (g + 1) * SSM_STATE].astype(BF16)
            cb = _dot_nt(cg, bg)
            hg = h_scr[g]
            y_int = _dot(cg, hg.astype(BF16))
            for pr in range(SSM_HPG // 2):
                hd0 = g * SSM_HPG + 2 * pr
                lo = hd0 * SSM_HEAD_DIM
                xp = act_buf[rows, lo:lo + LANES]
                xpb = xp.astype(BF16)
                ys, es, ws = [], [], []
                for hd in (hd0, hd0 + 1):
                    ccol = jnp.broadcast_to(cum[:, hd:hd + 1], (L, LANES))
                    dec = jnp.exp(jnp.where(tri, ccol[:, :L] - cum_t[hd:hd + 1, :], -jnp.inf))
                    wm = cb * dec * dt_t[hd:hd + 1, :]
                    ys.append(_dot(wm.astype(BF16), xpb))
                    es.append(jnp.exp(ccol))
                    ws.append(jnp.broadcast_to(wend[:, hd:hd + 1], (L, LANES)))
                yi = y_int[:, pr * LANES:(pr + 1) * LANES]
                y_buf[rows, lo:lo + LANES] = (jnp.where(left, ys[0], ys[1])
                                              + yi * jnp.where(left, es[0], es[1]))
                xw_buf[:, pr * LANES:(pr + 1) * LANES] = xp * jnp.where(left, ws[0], ws[1])
            h_scr[g] = hg * dec_last[:, g * inner:(g + 1) * inner] + _dot_tn(bg, xw_buf[...].astype(BF16))
        return carry

    lax.fori_loop(0, tt // L, chunk, 0)

    y = y_buf[...] + dfull_ref[...] * act_buf[:, :d_in]
    y = y * _silu(z_buf[...])
    gnw = gnw_ref[...]
    parts = []
    for g in range(SSM_GROUPS):
        parts.append(_rms(y[:, g * inner:(g + 1) * inner], gnw[:, g * inner:(g + 1) * inner]).astype(BF16))
    yn = jnp.concatenate(parts, axis=-1)
    xo_ref[0] = x + mod[2:3] * _dot(yn, wout_ref[...])

    @pl.when(t == pl.num_programs(1) - 1)
    def _():
        ho_ref[0] = h_scr[...]


def _ssd_layer(x, mod, nw, conv0, h0, p, tt, L):
    bsz, t, d = x.shape
    d_in = SSM_GROUPS * SSM_HPG * SSM_HEAD_DIM
    cdim = p['conv_w'].shape[1]
    nt = t // tt
    blk = lambda b, i: (b, 0, 0)
    in_specs = [
        pl.BlockSpec((1, tt, d), lambda b, i: (b, i, 0)),
        pl.BlockSpec((1, 6, d), blk),
        _const_spec((1, d)),
        pl.BlockSpec((1, CONV_W - 1, cdim), blk),
        pl.BlockSpec((1, SSM_GROUPS, SSM_STATE, SSM_HPG * SSM_HEAD_DIM), lambda b, i: (b, 0, 0, 0)),
        _const_spec(p['wz'].shape), _const_spec(p['wx'].shape), _const_spec(p['wdt'].shape),
        _const_spec(p['conv_w'].shape), _const_spec(p['conv_b'].shape),
        _const_spec(p['dt_bias'].shape), _const_spec(p['a_log'].shape), _const_spec(p['d_full'].shape),
        _const_spec(p['norm'].shape), _const_spec(p['w_out'].shape),
        _const_spec(p['eye'].shape), _const_spec(p['expand'].shape),
    ]
    out_shape = (jax.ShapeDtypeStruct((bsz, t, d), F32),
                 jax.ShapeDtypeStruct((bsz, CONV_W - 1, cdim), F32),
                 jax.ShapeDtypeStruct(h0.shape, F32))
    out_specs = (pl.BlockSpec((1, tt, d), lambda b, i: (b, i, 0)),
                 pl.BlockSpec((1, CONV_W - 1, cdim), blk),
                 pl.BlockSpec((1, SSM_GROUPS, SSM_STATE, SSM_HPG * SSM_HEAD_DIM), lambda b, i: (b, 0, 0, 0)))
    scratch = [pltpu.VMEM((8 + tt, cdim), F32), pltpu.VMEM((tt, cdim), F32), pltpu.VMEM((tt, d_in), F32),
               pltpu.VMEM((tt, LANES), F32), pltpu.VMEM((tt, d_in), F32),
               pltpu.VMEM((L, SSM_HPG * SSM_HEAD_DIM), F32),
               pltpu.VMEM((SSM_GROUPS, SSM_STATE, SSM_HPG * SSM_HEAD_DIM), F32)]
    return pl.pallas_call(
        functools.partial(_ssd_kernel, tt=tt, L=L),
        out_shape=out_shape, grid=(bsz, nt), in_specs=in_specs, out_specs=out_specs,
        scratch_shapes=scratch,
        compiler_params=pltpu.CompilerParams(
            dimension_semantics=("arbitrary", "arbitrary"), vmem_limit_bytes=VMEM_LIMIT),
        name="ssd_mixer",
    )(x, mod, nw, conv0, h0, p['wz'], p['wx'], p['wdt'], p['conv_w'], p['conv_b'], p['dt_bias'],
      p['a_log'], p['d_full'], p['norm'], p['w_out'], p['eye'], p['expand'])


def _ffn_kernel(x_ref, mod_ref, nw_ref, wg_ref, wu_ref, wd_ref, o_ref, *, nchunk):
    x = x_ref[0]
    mod = mod_ref[0]
    hn = (_rms(x, nw_ref[...]) * (1.0 + mod[4:5]) + mod[3:4]).astype(BF16)
    fc = wg_ref.shape[1] // nchunk
    acc = None
    for j in range(nchunk):
        g = _dot(hn, wg_ref[:, j * fc:(j + 1) * fc])
        u = _dot(hn, wu_ref[:, j * fc:(j + 1) * fc])
        part = _dot((_silu(g) * u).astype(BF16), wd_ref[j * fc:(j + 1) * fc, :])
        acc = part if acc is None else acc + part
    o_ref[0] = x + mod[5:6] * acc


def _ffn_layer(x, mod, nw, p, tm):
    bsz, t, d = x.shape
    return pl.pallas_call(
        functools.partial(_ffn_kernel, nchunk=2),
        out_shape=jax.ShapeDtypeStruct(x.shape, F32),
        grid=(bsz, t // tm),
        in_specs=[pl.BlockSpec((1, tm, d), lambda b, i: (b, i, 0)),
                  pl.BlockSpec((1, 6, d), lambda b, i: (b, 0, 0)),
                  _const_spec((1, d)),
                  _const_spec(p['w_gate'].shape), _const_spec(p['w_up'].shape), _const_spec(p['w_down'].shape)],
        out_specs=pl.BlockSpec((1, tm, d), lambda b, i: (b, i, 0)),
        compiler_params=pltpu.CompilerParams(
            dimension_semantics=("arbitrary", "arbitrary"), vmem_limit_bytes=VMEM_LIMIT),
        name="ffn",
    )(x, mod, nw, p['w_gate'], p['w_up'], p['w_down'])


def _mlstm_kernel(x_ref, mod_ref, nw_ref, conv0_ref, c0_ref, n0_ref, m0_ref,
                  wxm_ref, wo_ref, cw_ref, cb_ref, wq_ref, wk_ref, wv_ref, wgq_ref, wgk_ref, wgv_ref, bg_ref,
                  gnw_ref, skip_ref, wout_ref, eyei_ref, eyef_ref,
                  xo_ref, convo_ref, co_ref, no_ref, mo_ref,
                  xm_buf, xc_buf, q_buf, k_buf, v_buf, op_buf, gi_buf, lf_buf, hh_buf, c_scr, n_scr, m_scr,
                  *, tt, L):
    t = pl.program_id(1)
    hd_dim = wq_ref.shape[1]
    k_scale = hd_dim ** -0.5

    @pl.when(t == 0)
    def _():
        xm_buf[TAIL0:8, :] = conv0_ref[0]
        c_scr[...] = c0_ref[0]
        n_scr[...] = n0_ref[0]
        m_scr[...] = m0_ref[0]

    x = x_ref[0]
    mod = mod_ref[0]
    hn = (_rms(x, nw_ref[...]) * (1.0 + mod[1:2]) + mod[0:1]).astype(BF16)
    xm_buf[8:8 + tt, :] = _dot(hn, wxm_ref[...])
    op_buf[...] = _dot(hn, wo_ref[...])
    _causal_conv_silu(xm_buf, xc_buf, tt, cw_ref, cb_ref)
    tail = xm_buf[tt + TAIL0:tt + 8, :]
    convo_ref[0] = tail

    gates = bg_ref[...]
    for h in range(ML_HEADS):
        sl = slice(h * hd_dim, (h + 1) * hd_dim)
        xc_h = xc_buf[:, sl].astype(BF16)
        xm_h = xm_buf[8:8 + tt, sl].astype(BF16)
        q = _dot(xc_h, wq_ref[h])
        k = _dot(xc_h, wk_ref[h])
        v = _dot(xm_h, wv_ref[h])
        gates = gates + _dot(q.astype(BF16), wgq_ref[sl, :]) + _dot(k.astype(BF16), wgk_ref[sl, :]) \
            + _dot(v.astype(BF16), wgv_ref[sl, :])
        q_buf[:, sl] = q
        k_buf[:, sl] = k * k_scale
        v_buf[:, sl] = v
    xm_buf[TAIL0:8, :] = tail
    gi_buf[...] = gates
    lf_buf[...] = jnp.minimum(gates, 0.0) - jnp.log1p(jnp.exp(-jnp.abs(gates)))
    tri = _tri_mask(L)
    tri_b = jnp.where(tri, 1.0, 0.0).astype(BF16)
    lane = lax.broadcasted_iota(jnp.int32, (1, LANES), 1)

    def chunk(c, carry):
        r0 = pl.multiple_of(c * L, L)
        rows = pl.ds(r0, L)
        gi = gi_buf[rows, :]
        bcum = _dot3_l(tri_b, lf_buf[rows, :])
        li_t = _transpose_rows(gi, eyei_ref[...])
        b_t = _transpose_rows(bcum, eyef_ref[...])
        m_prev = m_scr[...]
        m_next = m_prev
        for h in range(ML_HEADS):
            sl = slice(h * hd_dim, (h + 1) * hd_dim)
            bcol = bcum[:, ML_HEADS + h:ML_HEADS + h + 1]
            licol = gi[:, h:h + 1]
            mp = m_prev[:, h:h + 1]
            dmat = jnp.where(tri, bcol - b_t[h:h + 1, :] + li_t[h:h + 1, :], -jnp.inf)
            a_inter = bcol + mp
            m_t = jnp.maximum(a_inter, jnp.max(dmat, axis=1, keepdims=True))
            qh = q_buf[rows, sl]
            kh = k_buf[rows, sl]
            vh = v_buf[rows, sl].astype(BF16)
            qb = qh.astype(BF16)
            s_mat = jnp.exp(dmat - m_t) * _dot_nt(qb, kh.astype(BF16))
            w_inter = jnp.exp(a_inter - m_t)
            ch = c_scr[h]
            nh = n_scr[h:h + 1, :]
            num = _dot(s_mat.astype(BF16), vh) + w_inter * _dot(qb, ch.astype(BF16))
            den = jnp.sum(s_mat, axis=1, keepdims=True) + w_inter * jnp.sum(qh * nh, axis=1, keepdims=True)
            hh_buf[rows, sl] = num / jnp.maximum(jnp.abs(den), jnp.exp(-m_t))
            m_new = m_t[L - 1:L, :]
            b_last = bcol[L - 1:L, :]
            w_end = jnp.exp(b_last - bcol + licol - m_new)
            w_old = jnp.exp(b_last + mp - m_new)
            kw = kh * w_end
            c_scr[h] = w_old * ch + _dot_tn(kw.astype(BF16), vh)
            n_scr[h:h + 1, :] = w_old * nh + jnp.sum(kw, axis=0, keepdims=True)
            m_next = jnp.where(lane == h, m_new, m_next)
        m_scr[...] = m_next
        return carry

    lax.fori_loop(0, tt // L, chunk, 0)

    gnw = gnw_ref[...]
    skip = skip_ref[...]
    parts = []
    for h in range(ML_HEADS):
        sl = slice(h * hd_dim, (h + 1) * hd_dim)
        hn_h = _rms(hh_buf[:, sl], gnw[:, sl])
        parts.append(((hn_h + skip[:, sl] * xc_buf[:, sl]) * _sigmoid(op_buf[:, sl])).astype(BF16))
    out = jnp.concatenate(parts, axis=-1)
    xo_ref[0] = x + mod[2:3] * _dot(out, wout_ref[...])

    @pl.when(t == pl.num_programs(1) - 1)
    def _():
        co_ref[0] = c_scr[...]
        no_ref[0] = n_scr[...]
        mo_ref[0] = m_scr[...]


def _mlstm_layer(x, mod, nw, conv0, c0, n0, m0, p, tt, L):
    bsz, t, d = x.shape
    inner = p['conv_w'].shape[1]
    hd = inner // ML_HEADS
    nt = t // tt
    blk = lambda b, i: (b, 0, 0)
    blk4 = lambda b, i: (b, 0, 0, 0)
    names = ['wxm', 'wo', 'conv_w', 'conv_b', 'w_q', 'w_k', 'w_v', 'wgq', 'wgk', 'wgv', 'bg',
             'norm', 'skip', 'w_out', 'eye_i', 'eye_f']
    in_specs = [
        pl.BlockSpec((1, tt, d), lambda b, i: (b, i, 0)),
        pl.BlockSpec((1, 6, d), blk),
        _const_spec((1, d)),
        pl.BlockSpec((1, CONV_W - 1, inner), blk),
        pl.BlockSpec((1, ML_HEADS, hd, hd), blk4),
        pl.BlockSpec((1, ML_HEADS, hd), blk),
        pl.BlockSpec((1, 1, LANES), blk),
    ] + [_const_spec(p[n].shape) for n in names]
    out_shape = (jax.ShapeDtypeStruct((bsz, t, d), F32),
                 jax.ShapeDtypeStruct((bsz, CONV_W - 1, inner), F32),
                 jax.ShapeDtypeStruct((bsz, ML_HEADS, hd, hd), F32),
                 jax.ShapeDtypeStruct((bsz, ML_HEADS, hd), F32),
                 jax.ShapeDtypeStruct((bsz, 1, LANES), F32))
    out_specs = (pl.BlockSpec((1, tt, d), lambda b, i: (b, i, 0)),
                 pl.BlockSpec((1, CONV_W - 1, inner), blk),
                 pl.BlockSpec((1, ML_HEADS, hd, hd), blk4),
                 pl.BlockSpec((1, ML_HEADS, hd), blk),
                 pl.BlockSpec((1, 1, LANES), blk))
    scratch = ([pltpu.VMEM((8 + tt, inner), F32)] + [pltpu.VMEM((tt, inner), F32)] * 5
               + [pltpu.VMEM((tt, LANES), F32)] * 2 + [pltpu.VMEM((tt, inner), F32)]
               + [pltpu.VMEM((ML_HEADS, hd, hd), F32), pltpu.VMEM((ML_HEADS, hd), F32),
                  pltpu.VMEM((1, LANES), F32)])
    return pl.pallas_call(
        functools.partial(_mlstm_kernel, tt=tt, L=L),
        out_shape=out_shape, grid=(bsz, nt), in_specs=in_specs, out_specs=out_specs,
        scratch_shapes=scratch,
        compiler_params=pltpu.CompilerParams(
            dimension_semantics=("arbitrary", "arbitrary"), vmem_limit_bytes=VMEM_LIMIT),
        name="mlstm_mixer",
    )(x, mod, nw, conv0, c0, n0, m0, *[p[n] for n in names])


def _router_kernel(x_ref, mod_ref, nw_ref, wr_ref, br_ref, eye_ref,
                   h_ref, rc_ref, rr_ref, comb_ref, cnt_ref, *, tm):
    x = x_ref[0]
    mod = mod_ref[0]
    h = _rms(x, nw_ref[...]) * (1.0 + mod[4:5]) + mod[3:4]
    h_ref[0] = h.astype(BF16)
    h1, h2, h3 = _split3(h)
    w1, w2, w3 = _split3(wr_ref[...])
    logits = (_dot(h1, w1) + (_dot(h1, w2) + _dot(h2, w1))
              + (_dot(h1, w3) + _dot(h2, w2) + _dot(h3, w1))) + br_ref[...]
    lane = lax.broadcasted_iota(jnp.int32, (tm, LANES), 1)
    lg = jnp.where(lane < N_EXPERTS, logits, -jnp.inf)
    m1 = jnp.max(lg, axis=1, keepdims=True)
    i1 = jnp.min(jnp.where(lg == m1, lane, LANES), axis=1, keepdims=True)
    lg2 = jnp.where(lane == i1, -jnp.inf, lg)
    m2 = jnp.max(lg2, axis=1, keepdims=True)
    i2 = jnp.min(jnp.where(lg2 == m2, lane, LANES), axis=1, keepdims=True)
    e2 = jnp.exp(m2 - m1)
    w_top1 = 1.0 / (1.0 + e2)
    w_top2 = e2 / (1.0 + e2)
    comb_ref[0] = jnp.where(lane == i1, w_top1, 0.0) + jnp.where(lane == i2, w_top2, 0.0)
    sel = jnp.logical_or(lane == i1, lane == i2)
    mask = jnp.where(sel, 1.0, 0.0).astype(BF16)
    r = lax.broadcasted_iota(jnp.int32, (tm, tm), 0)
    c = lax.broadcasted_iota(jnp.int32, (tm, tm), 1)
    before_col = jnp.where(c < r, 1.0, 0.0).astype(BF16)
    before_row = jnp.where(r < c, 1.0, 0.0).astype(BF16)
    rank_c = _dot(before_col, mask)
    rc_ref[0] = jnp.where(sel, rank_c, -1.0)
    mask_t = _dot_nt(eye_ref[...], mask)
    rank_r = _dot(mask_t.astype(BF16), before_row)
    rr_ref[0, 0] = jnp.where(mask_t > 0.5, rank_r, -1.0)
    cnt = jnp.sum(mask.astype(F32), axis=0, keepdims=True)
    cnt_ref[0, 0] = jnp.broadcast_to(cnt, (8, LANES)).astype(jnp.int32)


def _router(x, mod, nw, p, tm):
    bsz, t, d = x.shape
    nt = t // tm
    ti = lambda b, i: (b, i, 0)
    t4 = lambda b, i: (b, i, 0, 0)
    out_shape = (jax.ShapeDtypeStruct((bsz, t, d), BF16),
                 jax.ShapeDtypeStruct((bsz, t, LANES), F32),
                 jax.ShapeDtypeStruct((bsz, nt, 16, tm), F32),
                 jax.ShapeDtypeStruct((bsz, t, LANES), F32),
                 jax.ShapeDtypeStruct((bsz, nt, 8, LANES), jnp.int32))
    out_specs = (pl.BlockSpec((1, tm, d), ti), pl.BlockSpec((1, tm, LANES), ti),
                 pl.BlockSpec((1, 1, 16, tm), t4), pl.BlockSpec((1, tm, LANES), ti),
                 pl.BlockSpec((1, 1, 8, LANES), t4))
    return pl.pallas_call(
        functools.partial(_router_kernel, tm=tm),
        out_shape=out_shape, grid=(bsz, nt),
        in_specs=[pl.BlockSpec((1, tm, d), ti), pl.BlockSpec((1, 6, d), lambda b, i: (b, 0, 0)),
                  _const_spec((1, d)), _const_spec(p['w_router'].shape), _const_spec(p['b_router'].shape),
                  _const_spec(p['eye16'].shape)],
        out_specs=out_specs,
        compiler_params=pltpu.CompilerParams(
            dimension_semantics=("arbitrary", "arbitrary"), vmem_limit_bytes=VMEM_LIMIT),
        name="router",
    )(x, mod, nw, p['w_router'], p['b_router'], p['eye16'])


def _expert_kernel(cnt_ref, acc_ref, h_ref, rc_ref, rr_ref, comb_ref, mod_ref, wg_ref, wu_ref, wd_ref, fw_ref,
                   o_ref, *, e, tm, bounds, final):
    idx = pl.program_id(0) * pl.num_programs(1) + pl.program_id(1)
    cnt = cnt_ref[idx]
    o_ref[0] = acc_ref[0]
    scale = mod_ref[0][5:6] * comb_ref[0][:, e:e + 1]

    for start, stop in bounds:
        rows = stop - start

        @pl.when(cnt > start)
        def _():
            rr = rr_ref[0, 0][e:e + 1, :]
            slot_r = lax.broadcasted_iota(jnp.int32, (rows, tm), 0).astype(F32) + float(start)
            gather = jnp.where(slot_r == rr, 1.0, 0.0).astype(BF16)
            xc = _dot(gather, h_ref[0]).astype(BF16)
            g = _dot(xc, wg_ref[...])
            u = _dot(xc, wu_ref[...])
            yc = _dot((_silu(g) * u).astype(BF16), wd_ref[...])
            rc = rc_ref[0][:, e:e + 1]
            slot_c = lax.broadcasted_iota(jnp.int32, (tm, rows), 1).astype(F32) + float(start)
            scatter = jnp.where(slot_c == rc, 1.0, 0.0).astype(BF16)
            o_ref[0] = o_ref[0] + scale * _dot(scatter, yc.astype(BF16))

    if final:
        o_ref[0] = _rms(o_ref[0], fw_ref[...])


def _expert(e, counts_e, acc, h, rc, rr, comb, mod, wg, wu, wd, fw, tm, final):
    bsz, t, d = acc.shape
    nt = t // tm
    first = min(tm, max(16, (tm // 4 + tm // 16) // 16 * 16))
    edges = [0] + list(range(first, tm, LANES)) + [tm]
    bounds = tuple(zip(edges[:-1], edges[1:]))
    ti = lambda b, i, cnt: (b, i, 0)
    grid_spec = pltpu.PrefetchScalarGridSpec(
        num_scalar_prefetch=1, grid=(bsz, nt),
        in_specs=[pl.BlockSpec((1, tm, d), ti), pl.BlockSpec((1, tm, d), ti),
                  pl.BlockSpec((1, tm, LANES), ti), pl.BlockSpec((1, 1, 16, tm), lambda b, i, cnt: (b, i, 0, 0)),
                  pl.BlockSpec((1, tm, LANES), ti), pl.BlockSpec((1, 6, d), lambda b, i, cnt: (b, 0, 0)),
                  _const_spec(wg.shape), _const_spec(wu.shape), _const_spec(wd.shape), _const_spec(fw.shape)],
        out_specs=pl.BlockSpec((1, tm, d), ti))
    return pl.pallas_call(
        functools.partial(_expert_kernel, e=e, tm=tm, bounds=bounds, final=final),
        out_shape=jax.ShapeDtypeStruct(acc.shape, F32),
        grid_spec=grid_spec,
        compiler_params=pltpu.CompilerParams(
            dimension_semantics=("arbitrary", "arbitrary"), vmem_limit_bytes=VMEM_LIMIT),
        name="expert",
    )(counts_e, acc, h, rc, rr, comb, mod, wg, wu, wd, fw)


def _moe_layer(x, mod, nw, p, fw, tm):
    bsz, t, _ = x.shape
    h, rc, rr, comb, cnt = _router(x, mod, nw, p, tm)
    counts = jnp.transpose(cnt[:, :, 0, :N_EXPERTS].reshape(bsz * (t // tm), N_EXPERTS))
    acc = x
    for e in range(N_EXPERTS):
        acc = _expert(e, counts[e], acc, h, rc, rr, comb, mod, p['w_gate'][e], p['w_up'][e], p['w_down'][e],
                      fw, tm, final=(e == N_EXPERTS - 1))
    return acc


def _pad_lanes(a):
    return jnp.pad(a, [(0, 0)] * (a.ndim - 1) + [(0, LANES - a.shape[-1])])


def _one_hot_rows(n, offset):
    r = lax.broadcasted_iota(jnp.int32, (n, LANES), 0)
    c = lax.broadcasted_iota(jnp.int32, (n, LANES), 1)
    return (c == r + offset).astype(BF16)


def _prep_ssm(w_in, conv_w, conv_b, dt_bias, a_log, d_skip, norm_w, w_out):
    d_in = SSM_GROUPS * SSM_HPG * SSM_HEAD_DIM
    cdim = conv_w.shape[1]
    heads = SSM_GROUPS * SSM_HPG
    r = lax.broadcasted_iota(jnp.int32, (LANES, d_in), 0)
    c = lax.broadcasted_iota(jnp.int32, (LANES, d_in), 1)
    return dict(
        wz=w_in[:, :d_in].astype(BF16), wx=w_in[:, d_in:d_in + cdim].astype(BF16),
        wdt=_pad_lanes(w_in[:, d_in + cdim:]).astype(BF16),
        conv_w=conv_w, conv_b=conv_b.reshape(1, cdim),
        dt_bias=_pad_lanes(dt_bias.reshape(1, heads)), a_log=_pad_lanes(a_log.reshape(1, heads)),
        d_full=jnp.repeat(d_skip, SSM_HEAD_DIM).reshape(1, d_in),
        norm=norm_w.reshape(1, d_in), w_out=w_out.astype(BF16),
        eye=_one_hot_rows(heads, 0), expand=(c // SSM_HEAD_DIM == r).astype(BF16))


def _prep_mlstm(w_in, conv_w, conv_b, w_q, w_k, w_v, w_ig, b_ig, w_fg, b_fg, norm_w, skip, w_out):
    inner = conv_w.shape[1]
    hd = inner // ML_HEADS
    wg = jnp.concatenate([w_ig, w_fg], axis=-1).reshape(ML_HEADS, 3, hd, 2 * ML_HEADS)
    part = lambda j: _pad_lanes(wg[:, j].reshape(inner, 2 * ML_HEADS)).astype(BF16)
    return dict(
        wxm=w_in[:, :inner].astype(BF16), wo=w_in[:, inner:].astype(BF16),
        conv_w=conv_w, conv_b=conv_b.reshape(1, inner),
        w_q=w_q.astype(BF16), w_k=w_k.astype(BF16), w_v=w_v.astype(BF16),
        wgq=part(0), wgk=part(1), wgv=part(2),
        bg=_pad_lanes(jnp.concatenate([b_ig, b_fg]).reshape(1, 2 * ML_HEADS)),
        norm=norm_w.reshape(1, inner), skip=skip.reshape(1, inner), w_out=w_out.astype(BF16),
        eye_i=_one_hot_rows(ML_HEADS, 0), eye_f=_one_hot_rows(ML_HEADS, ML_HEADS))


def _trunk(x, mod, ssm_conv, ssm_state, ml_conv, ml_c, ml_n, ml_m, p):
    bsz, t, d = x.shape
    L = SCAN_CHUNK if t % SCAN_CHUNK == 0 else t
    tt = min(t, 256)
    tm = min(t, 512)
    n_heads = SSM_GROUPS * SSM_HPG
    h0 = ssm_state.reshape(bsz, SSM_GROUPS, SSM_HPG, SSM_HEAD_DIM, SSM_STATE)
    h0 = h0.transpose(0, 1, 4, 2, 3).reshape(bsz, SSM_GROUPS, SSM_STATE, SSM_HPG * SSM_HEAD_DIM)
    x, conv_s, h_s = _ssd_layer(x, mod[0], p['norm_mix'][0], ssm_conv, h0, p['ssm'], tt, L)
    h_s = h_s.reshape(bsz, SSM_GROUPS, SSM_STATE, SSM_HPG, SSM_HEAD_DIM).transpose(0, 1, 3, 4, 2)
    h_s = h_s.reshape(bsz, n_heads, SSM_HEAD_DIM, SSM_STATE)
    x = _ffn_layer(x, mod[0], p['norm_ffn'][0], p['ffn'], tm)
    m0 = _pad_lanes(ml_m).reshape(bsz, 1, LANES)
    x, conv_m, c_m, n_m, m_m = _mlstm_layer(x, mod[1], p['norm_mix'][1], ml_conv, ml_c, ml_n, m0, p['ml'], tt, L)
    y = _moe_layer(x, mod[1], p['norm_ffn'][1], p['moe'], p['norm_final'], tm)
    return (y, conv_s[None], h_s[None], conv_m[None], c_m[None], n_m[None], m_m[:, 0, :ML_HEADS][None])


def kernel(x_prompt, x_sample, c_prompt, c_sample, state_ssm_conv, state_ssm, state_mlstm_conv, state_mlstm_C, state_mlstm_n, state_mlstm_m, w_ada, b_ada, norm_mix, norm_ffn, norm_final, ssm_w_in, ssm_conv_w, ssm_conv_b, ssm_dt_bias, ssm_a_log, ssm_d, ssm_norm, ssm_w_out, ml_w_in, ml_conv_w, ml_conv_b, ml_w_q, ml_w_k, ml_w_v, ml_w_igate, ml_b_igate, ml_w_fgate, ml_b_fgate, ml_norm, ml_skip, ml_w_out, ffn_w_gate, ffn_w_up, ffn_w_down, moe_w_router, moe_b_router, moe_w_gate, moe_w_up, moe_w_down):
    depth, d, _ = w_ada.shape
    assert depth == 2 and state_ssm.shape[0] == 1 and state_mlstm_C.shape[0] == 1
    bp, bs = x_prompt.shape[0], x_sample.shape[0]
    p = dict(
        norm_mix=norm_mix.reshape(depth, 1, d), norm_ffn=norm_ffn.reshape(depth, 1, d),
        norm_final=norm_final.reshape(1, d),
        ssm=_prep_ssm(ssm_w_in[0], ssm_conv_w[0], ssm_conv_b[0], ssm_dt_bias[0], ssm_a_log[0], ssm_d[0],
                      ssm_norm[0], ssm_w_out[0]),
        ml=_prep_mlstm(ml_w_in[0], ml_conv_w[0], ml_conv_b[0], ml_w_q[0], ml_w_k[0], ml_w_v[0], ml_w_igate[0],
                       ml_b_igate[0], ml_w_fgate[0], ml_b_fgate[0], ml_norm[0], ml_skip[0], ml_w_out[0]),
        ffn=dict(w_gate=ffn_w_gate[0].astype(BF16), w_up=ffn_w_up[0].astype(BF16),
                 w_down=ffn_w_down[0].astype(BF16)),
        moe=dict(w_router=_pad_lanes(moe_w_router[0]), b_router=_pad_lanes(moe_b_router[0].reshape(1, -1)),
                 eye16=_one_hot_rows(16, 0),
                 w_gate=moe_w_gate[0].astype(BF16), w_up=moe_w_up[0].astype(BF16),
                 w_down=moe_w_down[0].astype(BF16)))
    mod = _ada(jnp.concatenate([c_prompt, c_sample], axis=0), w_ada, b_ada)
    mod = mod.reshape(depth, bp + bs, 6, d)

    f = F32
    zeros = lambda a, b: jnp.zeros((b,) + a.shape[2:], f)
    out_p = _trunk(x_prompt, mod[:, :bp], zeros(state_ssm_conv, bp), zeros(state_ssm, bp),
                   zeros(state_mlstm_conv, bp), zeros(state_mlstm_C, bp), zeros(state_mlstm_n, bp),
                   zeros(state_mlstm_m, bp), p)
    out_s = _trunk(x_sample, mod[:, bp:], state_ssm_conv[0], state_ssm[0], state_mlstm_conv[0],
                   state_mlstm_C[0], state_mlstm_n[0], state_mlstm_m[0], p)
    return (out_p[0], out_s[0]) + tuple(out_p[1:]) + tuple(out_s[1:])
```

```python
import functools

import jax
import jax.numpy as jnp
from jax import lax
from jax.experimental import pallas as pl
from jax.experimental.pallas import tpu as pltpu

F32 = jnp.float32
BF16 = jnp.bfloat16
RMS_EPS = 1e-6
CONV_W = 4
LANES = 128
SCAN_CHUNK = LANES
TAIL0 = 8 - (CONV_W - 1)
VMEM_LIMIT = 60 * 1024 * 1024

SSM_GROUPS = 4
SSM_HPG = 8
SSM_HEAD_DIM = 64
SSM_STATE = 128
ML_HEADS = 8
N_EXPERTS = 8


def _dot(a, b):
    return jnp.dot(a, b, preferred_element_type=F32)


def _dot_nt(a, b):
    return lax.dot_general(a, b, (((1,), (1,)), ((), ())), preferred_element_type=F32)


def _dot_tn(a, b):
    return lax.dot_general(a, b, (((0,), (0,)), ((), ())), preferred_element_type=F32)


def _split3(x):
    h1 = x.astype(BF16)
    r = x - h1.astype(F32)
    h2 = r.astype(BF16)
    r = r - h2.astype(F32)
    return h1, h2, r.astype(BF16)


def _dot3_l(sel, x):
    return sum(_dot(sel, p) for p in _split3(x))


def _dot3_r(x, sel):
    return sum(_dot(p, sel) for p in _split3(x))


def _transpose_rows(x, eye):
    return sum(_dot_nt(eye, p) for p in _split3(x))


def _sigmoid(x):
    return 1.0 / (1.0 + jnp.exp(-x))


def _silu(x):
    return x * _sigmoid(x)


def _softplus(x):
    return jnp.maximum(x, 0.0) + jnp.log1p(jnp.exp(-jnp.abs(x)))


def _rms(x, g):
    return x * lax.rsqrt(jnp.mean(x * x, axis=-1, keepdims=True) + RMS_EPS) * g


def _tri_mask(n):
    r = lax.broadcasted_iota(jnp.int32, (n, n), 0)
    c = lax.broadcasted_iota(jnp.int32, (n, n), 1)
    return r >= c


def _causal_conv_silu(buf, dst, tt, cw_ref, cb_ref):
    xn = buf[8:8 + tt, :]
    y = cb_ref[...] + xn * cw_ref[CONV_W - 1:CONV_W, :]
    for j in range(1, CONV_W):
        y = y + pltpu.roll(xn, j, axis=0) * cw_ref[CONV_W - 1 - j:CONV_W - j, :]
    dst[...] = _silu(y)
    head = cb_ref[...]
    for k in range(CONV_W):
        head = head + buf[TAIL0 + k:TAIL0 + k + 8, :] * cw_ref[k:k + 1, :]
    dst[0:8, :] = _silu(head)


def _ada_kernel(c_ref, w_ref, b_ref, o_ref):
    ca = _silu(c_ref[...]).astype(BF16)
    o_ref[0] = _dot(ca, w_ref[0].astype(BF16)) + b_ref[0]


def _ada(c_all, w_ada, b_ada):
    depth, d, n = w_ada.shape
    bt = c_all.shape[0]
    tn = n // 4
    return pl.pallas_call(
        _ada_kernel,
        out_shape=jax.ShapeDtypeStruct((depth, bt, n), F32),
        grid=(depth, n // tn),
        in_specs=[pl.BlockSpec((bt, d), lambda i, j: (0, 0)),
                  pl.BlockSpec((1, d, tn), lambda i, j: (i, 0, j)),
                  pl.BlockSpec((1, 1, tn), lambda i, j: (i, 0, j))],
        out_specs=pl.BlockSpec((1, bt, tn), lambda i, j: (i, 0, j)),
        compiler_params=pltpu.CompilerParams(
            dimension_semantics=("arbitrary", "arbitrary"), vmem_limit_bytes=VMEM_LIMIT),
        name="ada",
    )(c_all, w_ada, b_ada.reshape(depth, 1, n))


def _const_spec(shape):
    nd = len(shape)
    return pl.BlockSpec(shape, lambda *_: (0,) * nd, pipeline_mode=pl.Buffered(1))


def _ssd_kernel(x_ref, mod_ref, nw_ref, conv0_ref, h0_ref, wz_ref, wx_ref, wdt_ref, cw_ref, cb_ref,
                dtb_ref, alog_ref, dfull_ref, gnw_ref, wout_ref, eye_ref, expand_ref,
                xo_ref, convo_ref, ho_ref,
                xbc_buf, act_buf, z_buf, y_buf, xw_buf, *, R, L):
    t = pl.program_id(1)
    inner = SSM_HPG * SSM_HEAD_DIM
    d_in = SSM_GROUPS * inner
    gn = SSM_GROUPS * SSM_STATE

    @pl.when(t == 0)
    def _():
        for r in range(R):
            xbc_buf[r, TAIL0:8, :] = conv0_ref[r]
        ho_ref[...] = h0_ref[...]

    nw = nw_ref[...]
    hn = jnp.concatenate(
        [(_rms(x_ref[r], nw) * (1.0 + mod_ref[r][1:2]) + mod_ref[r][0:1]).astype(BF16) for r in range(R)], axis=0)
    z_buf[...] = _dot(hn, wz_ref[...])
    xbc = _dot(hn, wx_ref[...])
    dt = _softplus(_dot(hn, wdt_ref[...]) + dtb_ref[...])
    a = -jnp.exp(alog_ref[...])
    tri = _tri_mask(L)
    tri_b = jnp.where(tri, 1.0, 0.0).astype(BF16)
    eye = eye_ref[...]
    expand = expand_ref[...]
    left = lax.broadcasted_iota(jnp.int32, (1, LANES), 1) < SSM_HEAD_DIM

    for r in range(R):
        xbc_buf[r, 8:8 + L, :] = xbc[r * L:(r + 1) * L]
        _causal_conv_silu(xbc_buf.at[r], act_buf.at[r], L, cw_ref, cb_ref)
        tail = xbc_buf[r, L + TAIL0:L + 8, :]
        xbc_buf[r, TAIL0:8, :] = tail
        convo_ref[r] = tail

    for r in range(R):
        dtc = dt[r * L:(r + 1) * L]
        cum = _dot3_l(tri_b, dtc * a)
        cum_t = _transpose_rows(cum, eye)
        dt_t = _transpose_rows(dtc, eye)
        cum_last = cum[L - 1:L, :]
        w1, w2, _ = _split3(jnp.exp(cum_last - cum) * dtc)
        xw_buf[r] = (act_buf[r, :, :d_in] * (_dot(w1, expand) + _dot(w2, expand))).astype(BF16)
        dec_last = jnp.exp(_dot3_r(cum[L - 8:L, :], expand)[7:8, :])
        for g in range(SSM_GROUPS):
            bg = act_buf[r, :, d_in + g * SSM_STATE:d_in + (g + 1) * SSM_STATE].astype(BF16)
            cg = act_buf[r, :, d_in + gn + g * SSM_STATE:d_in + gn + (g + 1) * SSM_STATE].astype(BF16)
            cb = _dot_nt(cg, bg)
            hg = ho_ref[r, g]
            y_int = _dot(cg, hg.astype(BF16))
            for pr in range(SSM_HPG // 2):
                hd0 = g * SSM_HPG + 2 * pr
                lo = hd0 * SSM_HEAD_DIM
                xpb = act_buf[r, :, lo:lo + LANES].astype(BF16)
                ys, es = [], []
                for hd in (hd0, hd0 + 1):
                    ccol = jnp.broadcast_to(cum[:, hd:hd + 1], (L, LANES))
                    dec = jnp.exp(jnp.where(tri, ccol[:, :L] - cum_t[hd:hd + 1, :], -jnp.inf))
                    wm = cb * dec * dt_t[hd:hd + 1, :]
                    ys.append(_dot(wm.astype(BF16), xpb))
                    es.append(jnp.exp(ccol))
                yi = y_int[:, pr * LANES:(pr + 1) * LANES]
                y_buf[r, :, lo:lo + LANES] = (jnp.where(left, ys[0], ys[1])
                                              + yi * jnp.where(left, es[0], es[1]))
            ho_ref[r, g] = (hg * dec_last[:, g * inner:(g + 1) * inner]
                           + _dot_tn(bg, xw_buf[r, :, g * inner:(g + 1) * inner]))

    gnw = gnw_ref[...]
    rows = []
    for r in range(R):
        y = y_buf[r] + dfull_ref[...] * act_buf[r, :, :d_in]
        y = y * _silu(z_buf[r * L:(r + 1) * L, :])
        rows.append(jnp.concatenate(
            [_rms(y[:, g * inner:(g + 1) * inner], gnw[:, g * inner:(g + 1) * inner]).astype(BF16)
             for g in range(SSM_GROUPS)], axis=-1))
    out = _dot(jnp.concatenate(rows, axis=0), wout_ref[...])
    for r in range(R):
        xo_ref[r] = x_ref[r] + mod_ref[r][2:3] * out[r * L:(r + 1) * L]


def _ssd_layer(x, mod, nw, conv0, h0, p, R, L):
    bsz, t, d = x.shape
    d_in = SSM_GROUPS * SSM_HPG * SSM_HEAD_DIM
    cdim = p['conv_w'].shape[1]
    blk = lambda b, i: (b, 0, 0)
    st_blk = (R, SSM_GROUPS, SSM_STATE, SSM_HPG * SSM_HEAD_DIM)
    names = ['wz', 'wx', 'wdt', 'conv_w', 'conv_b', 'dt_bias', 'a_log', 'd_full', 'norm', 'w_out', 'eye', 'expand']
    in_specs = [
        pl.BlockSpec((R, L, d), lambda b, i: (b, i, 0)),
        pl.BlockSpec((R, 6, d), blk),
        _const_spec((1, d)),
        pl.BlockSpec((R, CONV_W - 1, cdim), blk),
        pl.BlockSpec(st_blk, lambda b, i: (b, 0, 0, 0)),
    ] + [_const_spec(p[n].shape) for n in names]
    out_shape = (jax.ShapeDtypeStruct((bsz, t, d), F32),
                 jax.ShapeDtypeStruct((bsz, CONV_W - 1, cdim), F32),
                 jax.ShapeDtypeStruct(h0.shape, F32))
    out_specs = (pl.BlockSpec((R, L, d), lambda b, i: (b, i, 0)),
                 pl.BlockSpec((R, CONV_W - 1, cdim), blk),
                 pl.BlockSpec(st_blk, lambda b, i: (b, 0, 0, 0)))
    scratch = [pltpu.VMEM((R, 8 + L, cdim), F32), pltpu.VMEM((R, L, cdim), F32), pltpu.VMEM((R * L, d_in), F32),
               pltpu.VMEM((R, L, d_in), F32), pltpu.VMEM((R, L, d_in), BF16)]
    return pl.pallas_call(
        functools.partial(_ssd_kernel, R=R, L=L),
        out_shape=out_shape, grid=(bsz // R, t // L), in_specs=in_specs, out_specs=out_specs,
        scratch_shapes=scratch,
        compiler_params=pltpu.CompilerParams(
            dimension_semantics=("arbitrary", "arbitrary"), vmem_limit_bytes=VMEM_LIMIT),
        name="ssd_mixer",
    )(x, mod, nw, conv0, h0, *[p[n] for n in names])


def _ffn_kernel(x_ref, mod_ref, nw_ref, wg_ref, wu_ref, wd_ref, o_ref, *, nchunk):
    x = x_ref[0]
    mod = mod_ref[0]
    hn = (_rms(x, nw_ref[...]) * (1.0 + mod[4:5]) + mod[3:4]).astype(BF16)
    fc = wg_ref.shape[1] // nchunk
    acc = None
    for j in range(nchunk):
        g = _dot(hn, wg_ref[:, j * fc:(j + 1) * fc])
        u = _dot(hn, wu_ref[:, j * fc:(j + 1) * fc])
        part = _dot((_silu(g) * u).astype(BF16), wd_ref[j * fc:(j + 1) * fc, :])
        acc = part if acc is None else acc + part
    o_ref[0] = x + mod[5:6] * acc


def _ffn_layer(x, mod, nw, p, tm):
    bsz, t, d = x.shape
    return pl.pallas_call(
        functools.partial(_ffn_kernel, nchunk=2),
        out_shape=jax.ShapeDtypeStruct(x.shape, F32),
        grid=(bsz, t // tm),
        in_specs=[pl.BlockSpec((1, tm, d), lambda b, i: (b, i, 0)),
                  pl.BlockSpec((1, 6, d), lambda b, i: (b, 0, 0)),
                  _const_spec((1, d)),
                  _const_spec(p['w_gate'].shape), _const_spec(p['w_up'].shape), _const_spec(p['w_down'].shape)],
        out_specs=pl.BlockSpec((1, tm, d), lambda b, i: (b, i, 0)),
        compiler_params=pltpu.CompilerParams(
            dimension_semantics=("arbitrary", "arbitrary"), vmem_limit_bytes=VMEM_LIMIT),
        name="ffn",
    )(x, mod, nw, p['w_gate'], p['w_up'], p['w_down'])


def _mlstm_kernel(x_ref, mod_ref, nw_ref, conv0_ref, c0_ref, n0_ref, m0_ref,
                  wxm_ref, wo_ref, cw_ref, cb_ref, wq_ref, wk_ref, wkt_ref, wv_ref, wgq_ref, wgk_ref, wgv_ref, bg_ref,
                  gnw_ref, skip_ref, wout_ref, eyei_ref, eyef_ref,
                  xo_ref, convo_ref, co_ref, no_ref, mo_ref,
                  xm_buf, xc_buf, q_buf, k_buf, kt_buf, v_buf, op_buf, hh_buf,
                  *, R, L):
    t = pl.program_id(1)
    hd_dim = wq_ref.shape[1]
    rep = hd_dim // LANES
    k_scale = hd_dim ** -0.5

    @pl.when(t == 0)
    def _():
        for r in range(R):
            xm_buf[r, TAIL0:8, :] = conv0_ref[r]
        co_ref[...] = c0_ref[...]
        no_ref[...] = n0_ref[...]
        mo_ref[...] = m0_ref[...]

    nw = nw_ref[...]
    hn = jnp.concatenate(
        [(_rms(x_ref[r], nw) * (1.0 + mod_ref[r][1:2]) + mod_ref[r][0:1]).astype(BF16) for r in range(R)], axis=0)
    xm = _dot(hn, wxm_ref[...])
    for r in range(R):
        xm_buf[r, 8:8 + L, :] = xm[r * L:(r + 1) * L]
    op_buf[...] = _dot(hn, wo_ref[...])
    for r in range(R):
        _causal_conv_silu(xm_buf.at[r], xc_buf.at[r], L, cw_ref, cb_ref)
        tail = xm_buf[r, L + TAIL0:L + 8, :]
        xm_buf[r, TAIL0:8, :] = tail
        convo_ref[r] = tail

    gates = bg_ref[...]
    for h in range(ML_HEADS):
        sl = slice(h * hd_dim, (h + 1) * hd_dim)
        xc_h = jnp.concatenate([xc_buf[r, :, sl].astype(BF16) for r in range(R)], axis=0)
        xm_h = jnp.concatenate([xm_buf[r, 8:8 + L, sl].astype(BF16) for r in range(R)], axis=0)
        q = _dot(xc_h, wq_ref[h])
        k = _dot(xc_h, wk_ref[h])
        v = _dot(xm_h, wv_ref[h])
        gates = gates + _dot(q.astype(BF16), wgq_ref[sl, :]) + _dot(k.astype(BF16), wgk_ref[sl, :]) \
            + _dot(v.astype(BF16), wgv_ref[sl, :])
        q_buf[:, sl] = q.astype(BF16)
        k_buf[:, sl] = (k * k_scale).astype(BF16)
        v_buf[:, sl] = v.astype(BF16)
        for r in range(R):
            kt_buf[r, h] = _dot_nt(wkt_ref[h], xc_h[r * L:(r + 1) * L]) * k_scale
    lf = jnp.minimum(gates, 0.0) - jnp.log1p(jnp.exp(-jnp.abs(gates)))
    tri = _tri_mask(L)
    tri_b = jnp.where(tri, 1.0, 0.0).astype(BF16)
    lane = lax.broadcasted_iota(jnp.int32, (1, LANES), 1)
    wide = lambda a: jnp.concatenate([a] * rep, axis=1)

    for r in range(R):
        rs = slice(r * L, (r + 1) * L)
        gi = gates[rs]
        bcum = _dot3_l(tri_b, lf[rs])
        li_t = _transpose_rows(gi, eyei_ref[...])
        b_t = _transpose_rows(bcum, eyef_ref[...])
        m_prev = mo_ref[r]
        m_next = m_prev
        for h in range(ML_HEADS):
            sl = slice(h * hd_dim, (h + 1) * hd_dim)
            bcol = jnp.broadcast_to(bcum[:, ML_HEADS + h:ML_HEADS + h + 1], (L, LANES))
            mp = jnp.broadcast_to(m_prev[:, h:h + 1], (1, LANES))
            logw_t = li_t[h:h + 1, :] - b_t[h:h + 1, :]
            dmat = jnp.where(tri, bcol[:, :L] + logw_t, -jnp.inf)
            a_inter = bcol + mp
            m_t = jnp.maximum(a_inter, jnp.max(dmat, axis=1, keepdims=True))
            qb = q_buf[rs, sl]
            vb = v_buf[rs, sl]
            s_mat = jnp.exp(dmat - m_t[:, :L]) * _dot_nt(qb, k_buf[rs, sl])
            w_inter = jnp.exp(a_inter - m_t)
            ch = co_ref[r, h]
            nh = no_ref[r, h:h + 1, :]
            num = _dot(s_mat.astype(BF16), vb) + wide(w_inter) * _dot(qb, ch.astype(BF16))
            qn = jnp.sum(qb.astype(F32) * nh, axis=1, keepdims=True)
            den = jnp.sum(s_mat, axis=1, keepdims=True) + w_inter * qn
            inv = 1.0 / jnp.maximum(jnp.abs(den), jnp.exp(-m_t))
            hh_buf[r, :, sl] = num * wide(inv)
            m_new = m_t[L - 1:L, :]
            b_last = bcol[L - 1:L, :]
            w_end = jnp.exp(b_last[:, :L] + logw_t - m_new[:, :L])
            w_old = wide(jnp.exp(b_last + mp - m_new))
            kw_t = (kt_buf[r, h] * w_end).astype(BF16)
            co_ref[r, h] = w_old * ch + _dot(kw_t, vb)
            w8 = jnp.broadcast_to(w_end, (8, L)).astype(BF16)
            no_ref[r, h:h + 1, :] = w_old * nh + _dot(w8, k_buf[rs, sl])[0:1, :]
            m_next = jnp.where(lane == h, m_new, m_next)
        mo_ref[r] = m_next

    gnw = gnw_ref[...]
    skip = skip_ref[...]
    rows = []
    for r in range(R):
        parts = []
        for h in range(ML_HEADS):
            sl = slice(h * hd_dim, (h + 1) * hd_dim)
            hn_h = _rms(hh_buf[r, :, sl], gnw[:, sl])
            parts.append(((hn_h + skip[:, sl] * xc_buf[r, :, sl])
                          * _sigmoid(op_buf[r * L:(r + 1) * L, sl])).astype(BF16))
        rows.append(jnp.concatenate(parts, axis=-1))
    out = _dot(jnp.concatenate(rows, axis=0), wout_ref[...])
    for r in range(R):
        xo_ref[r] = x_ref[r] + mod_ref[r][2:3] * out[r * L:(r + 1) * L]


def _mlstm_layer(x, mod, nw, conv0, c0, n0, m0, p, R, L):
    bsz, t, d = x.shape
    inner = p['conv_w'].shape[1]
    hd = inner // ML_HEADS
    blk = lambda b, i: (b, 0, 0)
    blk4 = lambda b, i: (b, 0, 0, 0)
    names = ['wxm', 'wo', 'conv_w', 'conv_b', 'w_q', 'w_k', 'w_kt', 'w_v', 'wgq', 'wgk', 'wgv', 'bg',
             'norm', 'skip', 'w_out', 'eye_i', 'eye_f']
    in_specs = [
        pl.BlockSpec((R, L, d), lambda b, i: (b, i, 0)),
        pl.BlockSpec((R, 6, d), blk),
        _const_spec((1, d)),
        pl.BlockSpec((R, CONV_W - 1, inner), blk),
        pl.BlockSpec((R, ML_HEADS, hd, hd), blk4, pipeline_mode=pl.Buffered(1)),
        pl.BlockSpec((R, ML_HEADS, hd), blk),
        pl.BlockSpec((R, 1, LANES), blk),
    ] + [_const_spec(p[n].shape) for n in names]
    out_shape = (jax.ShapeDtypeStruct((bsz, t, d), F32),
                 jax.ShapeDtypeStruct((bsz, CONV_W - 1, inner), F32),
                 jax.ShapeDtypeStruct((bsz, ML_HEADS, hd, hd), F32),
                 jax.ShapeDtypeStruct((bsz, ML_HEADS, hd), F32),
                 jax.ShapeDtypeStruct((bsz, 1, LANES), F32))
    out_specs = (pl.BlockSpec((R, L, d), lambda b, i: (b, i, 0)),
                 pl.BlockSpec((R, CONV_W - 1, inner), blk),
                 pl.BlockSpec((R, ML_HEADS, hd, hd), blk4),
                 pl.BlockSpec((R, ML_HEADS, hd), blk),
                 pl.BlockSpec((R, 1, LANES), blk))
    scratch = [pltpu.VMEM((R, 8 + L, inner), F32), pltpu.VMEM((R, L, inner), F32),
               pltpu.VMEM((R * L, inner), BF16), pltpu.VMEM((R * L, inner), BF16),
               pltpu.VMEM((R, ML_HEADS, hd, L), F32), pltpu.VMEM((R * L, inner), BF16),
               pltpu.VMEM((R * L, inner), F32), pltpu.VMEM((R, L, inner), F32)]
    return pl.pallas_call(
        functools.partial(_mlstm_kernel, R=R, L=L),
        out_shape=out_shape, grid=(bsz // R, t // L), in_specs=in_specs, out_specs=out_specs,
        scratch_shapes=scratch,
        compiler_params=pltpu.CompilerParams(
            dimension_semantics=("arbitrary", "arbitrary"), vmem_limit_bytes=VMEM_LIMIT),
        name="mlstm_mixer",
    )(x, mod, nw, conv0, c0, n0, m0, *[p[n] for n in names])


def _router_kernel(x_ref, mod_ref, nw_ref, wr_ref, br_ref, eye_ref,
                   h_ref, rc_ref, rr_ref, comb_ref, cnt_ref, *, tm):
    x = x_ref[0]
    mod = mod_ref[0]
    h = _rms(x, nw_ref[...]) * (1.0 + mod[4:5]) + mod[3:4]
    h_ref[0] = h.astype(BF16)
    h1, h2, h3 = _split3(h)
    w1, w2, w3 = _split3(wr_ref[...])
    logits = (_dot(h1, w1) + (_dot(h1, w2) + _dot(h2, w1))
              + (_dot(h1, w3) + _dot(h2, w2) + _dot(h3, w1))) + br_ref[...]
    lane = lax.broadcasted_iota(jnp.int32, (tm, LANES), 1)
    lg = jnp.where(lane < N_EXPERTS, logits, -jnp.inf)
    m1 = jnp.max(lg, axis=1, keepdims=True)
    i1 = jnp.min(jnp.where(lg == m1, lane, LANES), axis=1, keepdims=True)
    lg2 = jnp.where(lane == i1, -jnp.inf, lg)
    m2 = jnp.max(lg2, axis=1, keepdims=True)
    i2 = jnp.min(jnp.where(lg2 == m2, lane, LANES), axis=1, keepdims=True)
    e2 = jnp.exp(m2 - m1)
    w_top1 = 1.0 / (1.0 + e2)
    w_top2 = e2 / (1.0 + e2)
    comb_ref[0] = jnp.where(lane == i1, w_top1, 0.0) + jnp.where(lane == i2, w_top2, 0.0)
    sel = jnp.logical_or(lane == i1, lane == i2)
    mask = jnp.where(sel, 1.0, 0.0).astype(BF16)
    r = lax.broadcasted_iota(jnp.int32, (tm, tm), 0)
    c = lax.broadcasted_iota(jnp.int32, (tm, tm), 1)
    before_col = jnp.where(c < r, 1.0, 0.0).astype(BF16)
    before_row = jnp.where(r < c, 1.0, 0.0).astype(BF16)
    rank_c = _dot(before_col, mask)
    rc_ref[0] = jnp.where(sel, rank_c, -1.0)
    mask_t = _dot_nt(eye_ref[...], mask)
    rank_r = _dot(mask_t.astype(BF16), before_row)
    rr_ref[0, 0] = jnp.where(mask_t > 0.5, rank_r, -1.0)
    cnt = jnp.sum(mask.astype(F32), axis=0, keepdims=True)
    cnt_ref[0, 0] = jnp.broadcast_to(cnt, (8, LANES)).astype(jnp.int32)


def _router(x, mod, nw, p, tm):
    bsz, t, d = x.shape
    nt = t // tm
    ti = lambda b, i: (b, i, 0)
    t4 = lambda b, i: (b, i, 0, 0)
    out_shape = (jax.ShapeDtypeStruct((bsz, t, d), BF16),
                 jax.ShapeDtypeStruct((bsz, t, LANES), F32),
                 jax.ShapeDtypeStruct((bsz, nt, 16, tm), F32),
                 jax.ShapeDtypeStruct((bsz, t, LANES), F32),
                 jax.ShapeDtypeStruct((bsz, nt, 8, LANES), jnp.int32))
    out_specs = (pl.BlockSpec((1, tm, d), ti), pl.BlockSpec((1, tm, LANES), ti),
                 pl.BlockSpec((1, 1, 16, tm), t4), pl.BlockSpec((1, tm, LANES), ti),
                 pl.BlockSpec((1, 1, 8, LANES), t4))
    return pl.pallas_call(
        functools.partial(_router_kernel, tm=tm),
        out_shape=out_shape, grid=(bsz, nt),
        in_specs=[pl.BlockSpec((1, tm, d), ti), pl.BlockSpec((1, 6, d), lambda b, i: (b, 0, 0)),
                  _const_spec((1, d)), _const_spec(p['w_router'].shape), _const_spec(p['b_router'].shape),
                  _const_spec(p['eye16'].shape)],
        out_specs=out_specs,
        compiler_params=pltpu.CompilerParams(
            dimension_semantics=("arbitrary", "arbitrary"), vmem_limit_bytes=VMEM_LIMIT),
        name="router",
    )(x, mod, nw, p['w_router'], p['b_router'], p['eye16'])


def _expert_kernel(cnt_ref, acc_ref, h_ref, rc_ref, rr_ref, comb_ref, mod_ref, wg_ref, wu_ref, wd_ref, fw_ref,
                   o_ref, *, e, tm, bounds, final):
    idx = pl.program_id(0) * pl.num_programs(1) + pl.program_id(1)
    cnt = cnt_ref[idx]
    o_ref[0] = acc_ref[0]
    scale = mod_ref[0][5:6] * comb_ref[0][:, e:e + 1]

    for start, stop in bounds:
        rows = stop - start

        @pl.when(cnt > start)
        def _():
            rr = rr_ref[0, 0][e:e + 1, :]
            slot_r = lax.broadcasted_iota(jnp.int32, (rows, tm), 0).astype(F32) + float(start)
            gather = jnp.where(slot_r == rr, 1.0, 0.0).astype(BF16)
            xc = _dot(gather, h_ref[0]).astype(BF16)
            g = _dot(xc, wg_ref[...])
            u = _dot(xc, wu_ref[...])
            yc = _dot((_silu(g) * u).astype(BF16), wd_ref[...])
            rc = rc_ref[0][:, e:e + 1]
            slot_c = lax.broadcasted_iota(jnp.int32, (tm, rows), 1).astype(F32) + float(start)
            scatter = jnp.where(slot_c == rc, 1.0, 0.0).astype(BF16)
            o_ref[0] = o_ref[0] + scale * _dot(scatter, yc.astype(BF16))

    if final:
        o_ref[0] = _rms(o_ref[0], fw_ref[...])


def _expert(e, counts_e, acc, h, rc, rr, comb, mod, wg, wu, wd, fw, tm, final):
    bsz, t, d = acc.shape
    nt = t // tm
    first = min(tm, max(16, (tm // 4 + tm // 16) // 16 * 16))
    edges = [0] + list(range(first, tm, LANES)) + [tm]
    bounds = tuple(zip(edges[:-1], edges[1:]))
    ti = lambda b, i, cnt: (b, i, 0)
    grid_spec = pltpu.PrefetchScalarGridSpec(
        num_scalar_prefetch=1, grid=(bsz, nt),
        in_specs=[pl.BlockSpec((1, tm, d), ti), pl.BlockSpec((1, tm, d), ti),
                  pl.BlockSpec((1, tm, LANES), ti), pl.BlockSpec((1, 1, 16, tm), lambda b, i, cnt: (b, i, 0, 0)),
                  pl.BlockSpec((1, tm, LANES), ti), pl.BlockSpec((1, 6, d), lambda b, i, cnt: (b, 0, 0)),
                  _const_spec(wg.shape), _const_spec(wu.shape), _const_spec(wd.shape), _const_spec(fw.shape)],
        out_specs=pl.BlockSpec((1, tm, d), ti))
    return pl.pallas_call(
        functools.partial(_expert_kernel, e=e, tm=tm, bounds=bounds, final=final),
        out_shape=jax.ShapeDtypeStruct(acc.shape, F32),
        grid_spec=grid_spec,
        compiler_params=pltpu.CompilerParams(
            dimension_semantics=("arbitrary", "arbitrary"), vmem_limit_bytes=VMEM_LIMIT),
        name="expert",
    )(counts_e, acc, h, rc, rr, comb, mod, wg, wu, wd, fw)


def _moe_layer(x, mod, nw, p, fw, tm):
    bsz, t, _ = x.shape
    h, rc, rr, comb, cnt = _router(x, mod, nw, p, tm)
    counts = jnp.transpose(cnt[:, :, 0, :N_EXPERTS].reshape(bsz * (t // tm), N_EXPERTS))
    acc = x
    for e in range(N_EXPERTS):
        acc = _expert(e, counts[e], acc, h, rc, rr, comb, mod, p['w_gate'][e], p['w_up'][e], p['w_down'][e],
                      fw, tm, final=(e == N_EXPERTS - 1))
    return acc


def _pad_lanes(a):
    return jnp.pad(a, [(0, 0)] * (a.ndim - 1) + [(0, LANES - a.shape[-1])])


def _one_hot_rows(n, offset):
    r = lax.broadcasted_iota(jnp.int32, (n, LANES), 0)
    c = lax.broadcasted_iota(jnp.int32, (n, LANES), 1)
    return (c == r + offset).astype(BF16)


def _prep_ssm(w_in, conv_w, conv_b, dt_bias, a_log, d_skip, norm_w, w_out):
    d_in = SSM_GROUPS * SSM_HPG * SSM_HEAD_DIM
    cdim = conv_w.shape[1]
    heads = SSM_GROUPS * SSM_HPG
    r = lax.broadcasted_iota(jnp.int32, (LANES, d_in), 0)
    c = lax.broadcasted_iota(jnp.int32, (LANES, d_in), 1)
    return dict(
        wz=w_in[:, :d_in].astype(BF16), wx=w_in[:, d_in:d_in + cdim].astype(BF16),
        wdt=_pad_lanes(w_in[:, d_in + cdim:]).astype(BF16),
        conv_w=conv_w, conv_b=conv_b.reshape(1, cdim),
        dt_bias=_pad_lanes(dt_bias.reshape(1, heads)), a_log=_pad_lanes(a_log.reshape(1, heads)),
        d_full=jnp.repeat(d_skip, SSM_HEAD_DIM).reshape(1, d_in),
        norm=norm_w.reshape(1, d_in), w_out=w_out.astype(BF16),
        eye=_one_hot_rows(heads, 0), expand=(c // SSM_HEAD_DIM == r).astype(BF16))


def _prep_mlstm(w_in, conv_w, conv_b, w_q, w_k, w_v, w_ig, b_ig, w_fg, b_fg, norm_w, skip, w_out):
    inner = conv_w.shape[1]
    hd = inner // ML_HEADS
    wg = jnp.concatenate([w_ig, w_fg], axis=-1).reshape(ML_HEADS, 3, hd, 2 * ML_HEADS)
    part = lambda j: _pad_lanes(wg[:, j].reshape(inner, 2 * ML_HEADS)).astype(BF16)
    return dict(
        wxm=w_in[:, :inner].astype(BF16), wo=w_in[:, inner:].astype(BF16),
        conv_w=conv_w, conv_b=conv_b.reshape(1, inner),
        w_q=w_q.astype(BF16), w_k=w_k.astype(BF16), w_kt=jnp.swapaxes(w_k, 1, 2).astype(BF16),
        w_v=w_v.astype(BF16),
        wgq=part(0), wgk=part(1), wgv=part(2),
        bg=_pad_lanes(jnp.concatenate([b_ig, b_fg]).reshape(1, 2 * ML_HEADS)),
        norm=norm_w.reshape(1, inner), skip=skip.reshape(1, inner), w_out=w_out.astype(BF16),
        eye_i=_one_hot_rows(ML_HEADS, 0), eye_f=_one_hot_rows(ML_HEADS, ML_HEADS))


def _trunk(x, mod, ssm_conv, ssm_state, ml_conv, ml_c, ml_n, ml_m, p):
    bsz, t, d = x.shape
    L = SCAN_CHUNK if t % SCAN_CHUNK == 0 else t
    R = 2 if bsz % 2 == 0 else 1
    tm = min(t, 512)
    n_heads = SSM_GROUPS * SSM_HPG
    h0 = ssm_state.reshape(bsz, SSM_GROUPS, SSM_HPG, SSM_HEAD_DIM, SSM_STATE)
    h0 = h0.transpose(0, 1, 4, 2, 3).reshape(bsz, SSM_GROUPS, SSM_STATE, SSM_HPG * SSM_HEAD_DIM)
    x, conv_s, h_s = _ssd_layer(x, mod[0], p['norm_mix'][0], ssm_conv, h0, p['ssm'], R, L)
    h_s = h_s.reshape(bsz, SSM_GROUPS, SSM_STATE, SSM_HPG, SSM_HEAD_DIM).transpose(0, 1, 3, 4, 2)
    h_s = h_s.reshape(bsz, n_heads, SSM_HEAD_DIM, SSM_STATE)
    x = _ffn_layer(x, mod[0], p['norm_ffn'][0], p['ffn'], tm)
    m0 = _pad_lanes(ml_m).reshape(bsz, 1, LANES)
    x, conv_m, c_m, n_m, m_m = _mlstm_layer(x, mod[1], p['norm_mix'][1], ml_conv, ml_c, ml_n, m0, p['ml'], R, L)
    y = _moe_layer(x, mod[1], p['norm_ffn'][1], p['moe'], p['norm_final'], tm)
    return (y, conv_s[None], h_s[None], conv_m[None], c_m[None], n_m[None], m_m[:, 0, :ML_HEADS][None])


def kernel(x_prompt, x_sample, c_prompt, c_sample, state_ssm_conv, state_ssm, state_mlstm_conv, state_mlstm_C, state_mlstm_n, state_mlstm_m, w_ada, b_ada, norm_mix, norm_ffn, norm_final, ssm_w_in, ssm_conv_w, ssm_conv_b, ssm_dt_bias, ssm_a_log, ssm_d, ssm_norm, ssm_w_out, ml_w_in, ml_conv_w, ml_conv_b, ml_w_q, ml_w_k, ml_w_v, ml_w_igate, ml_b_igate, ml_w_fgate, ml_b_fgate, ml_norm, ml_skip, ml_w_out, ffn_w_gate, ffn_w_up, ffn_w_down, moe_w_router, moe_b_router, moe_w_gate, moe_w_up, moe_w_down):
    depth, d, _ = w_ada.shape
    assert depth == 2 and state_ssm.shape[0] == 1 and state_mlstm_C.shape[0] == 1
    bp, bs = x_prompt.shape[0], x_sample.shape[0]
    p = dict(
        norm_mix=norm_mix.reshape(depth, 1, d), norm_ffn=norm_ffn.reshape(depth, 1, d),
        norm_final=norm_final.reshape(1, d),
        ssm=_prep_ssm(ssm_w_in[0], ssm_conv_w[0], ssm_conv_b[0], ssm_dt_bias[0], ssm_a_log[0], ssm_d[0],
                      ssm_norm[0], ssm_w_out[0]),
        ml=_prep_mlstm(ml_w_in[0], ml_conv_w[0], ml_conv_b[0], ml_w_q[0], ml_w_k[0], ml_w_v[0], ml_w_igate[0],
                       ml_b_igate[0], ml_w_fgate[0], ml_b_fgate[0], ml_norm[0], ml_skip[0], ml_w_out[0]),
        ffn=dict(w_gate=ffn_w_gate[0].astype(BF16), w_up=ffn_w_up[0].astype(BF16),
                 w_down=ffn_w_down[0].astype(BF16)),
        moe=dict(w_router=_pad_lanes(moe_w_router[0]), b_router=_pad_lanes(moe_b_router[0].reshape(1, -1)),
                 eye16=_one_hot_rows(16, 0),
                 w_gate=moe_w_gate[0].astype(BF16), w_up=moe_w_up[0].astype(BF16),
                 w_down=moe_w_down[0].astype(BF16)))
    mod = _ada(jnp.concatenate([c_prompt, c_sample], axis=0), w_ada, b_ada)
    mod = mod.reshape(depth, bp + bs, 6, d)

    f = F32
    zeros = lambda a, b: jnp.zeros((b,) + a.shape[2:], f)
    out_p = _trunk(x_prompt, mod[:, :bp], zeros(state_ssm_conv, bp), zeros(state_ssm, bp),
                   zeros(state_mlstm_conv, bp), zeros(state_mlstm_C, bp), zeros(state_mlstm_n, bp),
                   zeros(state_mlstm_m, bp), p)
    out_s = _trunk(x_sample, mod[:, bp:], state_ssm_conv[0], state_ssm[0], state_mlstm_conv[0],
                   state_mlstm_C[0], state_mlstm_n[0], state_mlstm_m[0], p)
    return (out_p[0], out_s[0]) + tuple(out_p[1:]) + tuple(out_s[1:])
```

```python
import functools

import jax
import jax.numpy as jnp
from jax import lax
from jax.experimental import pallas as pl
from jax.experimental.pallas import tpu as pltpu

F32 = jnp.float32
BF16 = jnp.bfloat16
RMS_EPS = 1e-6
CONV_W = 4
LANES = 128
SCAN_CHUNK = LANES
TAIL0 = 8 - (CONV_W - 1)
VMEM_LIMIT = 60 * 1024 * 1024

SSM_GROUPS = 4
SSM_HPG = 8
SSM_HEAD_DIM = 64
SSM_STATE = 128
ML_HEADS = 8
N_EXPERTS = 8


def _dot(a, b):
    return jnp.dot(a, b, preferred_element_type=F32)


def _dot_nt(a, b):
    return lax.dot_general(a, b, (((1,), (1,)), ((), ())), preferred_element_type=F32)


def _dot_tn(a, b):
    return lax.dot_general(a, b, (((0,), (0,)), ((), ())), preferred_element_type=F32)


def _split3(x):
    h1 = x.astype(BF16)
    r = x - h1.astype(F32)
    h2 = r.astype(BF16)
    r = r - h2.astype(F32)
    return h1, h2, r.astype(BF16)


def _dot3_l(sel, x):
    return sum(_dot(sel, p) for p in _split3(x))


def _dot3_r(x, sel):
    return sum(_dot(p, sel) for p in _split3(x))


def _transpose_rows(x, eye):
    return sum(_dot_nt(eye, p) for p in _split3(x))


def _sigmoid(x):
    return 1.0 / (1.0 + jnp.exp(-x))


def _silu(x):
    return x * _sigmoid(x)


def _softplus(x):
    return jnp.maximum(x, 0.0) + jnp.log1p(jnp.exp(-jnp.abs(x)))


def _rms(x, g):
    return x * lax.rsqrt(jnp.mean(x * x, axis=-1, keepdims=True) + RMS_EPS) * g


def _tri_mask(n):
    r = lax.broadcasted_iota(jnp.int32, (n, n), 0)
    c = lax.broadcasted_iota(jnp.int32, (n, n), 1)
    return r >= c


def _causal_conv_silu(buf, dst, tt, cw_ref, cb_ref):
    xn = buf[8:8 + tt, :]
    x1 = pltpu.roll(xn, 1, axis=0)
    w = [cw_ref[k:k + 1, :] for k in range(CONV_W)]
    y = cb_ref[...] + xn * w[3] + x1 * w[2] + pltpu.roll(xn * w[1] + x1 * w[0], 2, axis=0)
    dst[...] = _silu(y)
    head = cb_ref[...]
    for k in range(CONV_W):
        head = head + buf[TAIL0 + k:TAIL0 + k + 8, :] * cw_ref[k:k + 1, :]
    dst[0:8, :] = _silu(head)


def _ada_kernel(c_ref, w_ref, b_ref, o_ref):
    ca = _silu(c_ref[...]).astype(BF16)
    o_ref[0] = _dot(ca, w_ref[0].astype(BF16)) + b_ref[0]


def _ada(c_all, w_ada, b_ada):
    depth, d, n = w_ada.shape
    bt = c_all.shape[0]
    tn = n // 4
    return pl.pallas_call(
        _ada_kernel,
        out_shape=jax.ShapeDtypeStruct((depth, bt, n), F32),
        grid=(depth, n // tn),
        in_specs=[pl.BlockSpec((bt, d), lambda i, j: (0, 0)),
                  pl.BlockSpec((1, d, tn), lambda i, j: (i, 0, j)),
                  pl.BlockSpec((1, 1, tn), lambda i, j: (i, 0, j))],
        out_specs=pl.BlockSpec((1, bt, tn), lambda i, j: (i, 0, j)),
        compiler_params=pltpu.CompilerParams(
            dimension_semantics=("arbitrary", "arbitrary"), vmem_limit_bytes=VMEM_LIMIT),
        name="ada",
    )(c_all, w_ada, b_ada.reshape(depth, 1, n))


def _const_spec(shape):
    nd = len(shape)
    return pl.BlockSpec(shape, lambda *_: (0,) * nd, pipeline_mode=pl.Buffered(1))


def _ssd_kernel(x_ref, mod_ref, nw_ref, conv0_ref, h0_ref, wz_ref, wx_ref, wdt_ref, cw_ref, cb_ref,
                dtb_ref, alog_ref, dfull_ref, gnw_ref, wout_ref, eye_ref, expand_ref,
                xo_ref, convo_ref, ho_ref,
                xbc_buf, act_buf, z_buf, y_buf, xw_buf, *, R, L):
    t = pl.program_id(1)
    inner = SSM_HPG * SSM_HEAD_DIM
    d_in = SSM_GROUPS * inner
    gn = SSM_GROUPS * SSM_STATE

    @pl.when(t == 0)
    def _():
        for r in range(R):
            xbc_buf[r, TAIL0:8, :] = conv0_ref[r]
        ho_ref[...] = h0_ref[...]

    nw = nw_ref[...]
    hn = jnp.concatenate(
        [(_rms(x_ref[r], nw) * (1.0 + mod_ref[r][1:2]) + mod_ref[r][0:1]).astype(BF16) for r in range(R)], axis=0)
    z_buf[...] = _dot(hn, wz_ref[...])
    xbc = _dot(hn, wx_ref[...])
    dt = _softplus(_dot(hn, wdt_ref[...]) + dtb_ref[...])
    a = -jnp.exp(alog_ref[...])
    tri = _tri_mask(L)
    tri_b = jnp.where(tri, 1.0, 0.0).astype(BF16)
    eye = eye_ref[...]
    expand = expand_ref[...]
    left = lax.broadcasted_iota(jnp.int32, (1, LANES), 1) < SSM_HEAD_DIM

    for r in range(R):
        xbc_buf[r, 8:8 + L, :] = xbc[r * L:(r + 1) * L]
        _causal_conv_silu(xbc_buf.at[r], act_buf.at[r], L, cw_ref, cb_ref)
        tail = xbc_buf[r, L + TAIL0:L + 8, :]
        xbc_buf[r, TAIL0:8, :] = tail
        convo_ref[r] = tail

    for r in range(R):
        dtc = dt[r * L:(r + 1) * L]
        cum = _dot3_l(tri_b, dtc * a)
        cum_t = _transpose_rows(cum, eye)
        dt_t = _transpose_rows(dtc, eye)
        cum_last = cum[L - 1:L, :]
        wend = (jnp.exp(cum_last - cum) * dtc).astype(BF16)
        xw_buf[r] = (act_buf[r, :, :d_in] * _dot(wend, expand)).astype(BF16)
        dec_last = jnp.exp(_dot3_r(cum[L - 8:L, :], expand)[7:8, :])
        for g in range(SSM_GROUPS):
            bg = act_buf[r, :, d_in + g * SSM_STATE:d_in + (g + 1) * SSM_STATE].astype(BF16)
            cg = act_buf[r, :, d_in + gn + g * SSM_STATE:d_in + gn + (g + 1) * SSM_STATE].astype(BF16)
            cb = _dot_nt(cg, bg)
            hg = ho_ref[r, g]
            y_int = _dot(cg, hg.astype(BF16))
            for pr in range(SSM_HPG // 2):
                hd0 = g * SSM_HPG + 2 * pr
                lo = hd0 * SSM_HEAD_DIM
                xp = act_buf[r, :, lo:lo + LANES]
                x2 = jnp.concatenate([jnp.where(left, xp, 0.0), jnp.where(left, 0.0, xp)], axis=0).astype(BF16)
                wms, es = [], []
                for hd in (hd0, hd0 + 1):
                    ccol = jnp.broadcast_to(cum[:, hd:hd + 1], (L, LANES))
                    dec = jnp.exp(jnp.where(tri, ccol[:, :L] - cum_t[hd:hd + 1, :], -jnp.inf))
                    wms.append((cb * dec * dt_t[hd:hd + 1, :]).astype(BF16))
                    es.append(jnp.exp(ccol))
                yi = y_int[:, pr * LANES:(pr + 1) * LANES]
                y_buf[r, :, lo:lo + LANES] = (_dot(jnp.concatenate(wms, axis=1), x2)
                                              + yi * jnp.where(left, es[0], es[1]))
            ho_ref[r, g] = (hg * dec_last[:, g * inner:(g + 1) * inner]
                           + _dot_tn(bg, xw_buf[r, :, g * inner:(g + 1) * inner]))

    gnw = gnw_ref[...]
    rows = []
    for r in range(R):
        y = y_buf[r] + dfull_ref[...] * act_buf[r, :, :d_in]
        y = y * _silu(z_buf[r * L:(r + 1) * L, :])
        rows.append(jnp.concatenate(
            [_rms(y[:, g * inner:(g + 1) * inner], gnw[:, g * inner:(g + 1) * inner]).astype(BF16)
             for g in range(SSM_GROUPS)], axis=-1))
    out = _dot(jnp.concatenate(rows, axis=0), wout_ref[...])
    for r in range(R):
        xo_ref[r] = x_ref[r] + mod_ref[r][2:3] * out[r * L:(r + 1) * L]


def _ssd_layer(x, mod, nw, conv0, h0, p, R, L):
    bsz, t, d = x.shape
    d_in = SSM_GROUPS * SSM_HPG * SSM_HEAD_DIM
    cdim = p['conv_w'].shape[1]
    blk = lambda b, i: (b, 0, 0)
    st_blk = (R, SSM_GROUPS, SSM_STATE, SSM_HPG * SSM_HEAD_DIM)
    names = ['wz', 'wx', 'wdt', 'conv_w', 'conv_b', 'dt_bias', 'a_log', 'd_full', 'norm', 'w_out', 'eye', 'expand']
    in_specs = [
        pl.BlockSpec((R, L, d), lambda b, i: (b, i, 0)),
        pl.BlockSpec((R, 6, d), blk),
        _const_spec((1, d)),
        pl.BlockSpec((R, CONV_W - 1, cdim), blk),
        pl.BlockSpec(st_blk, lambda b, i: (b, 0, 0, 0)),
    ] + [_const_spec(p[n].shape) for n in names]
    out_shape = (jax.ShapeDtypeStruct((bsz, t, d), F32),
                 jax.ShapeDtypeStruct((bsz, CONV_W - 1, cdim), F32),
                 jax.ShapeDtypeStruct(h0.shape, F32))
    out_specs = (pl.BlockSpec((R, L, d), lambda b, i: (b, i, 0)),
                 pl.BlockSpec((R, CONV_W - 1, cdim), blk),
                 pl.BlockSpec(st_blk, lambda b, i: (b, 0, 0, 0)))
    scratch = [pltpu.VMEM((R, 8 + L, cdim), F32), pltpu.VMEM((R, L, cdim), F32), pltpu.VMEM((R * L, d_in), F32),
               pltpu.VMEM((R, L, d_in), F32), pltpu.VMEM((R, L, d_in), BF16)]
    return pl.pallas_call(
        functools.partial(_ssd_kernel, R=R, L=L),
        out_shape=out_shape, grid=(bsz // R, t // L), in_specs=in_specs, out_specs=out_specs,
        scratch_shapes=scratch,
        compiler_params=pltpu.CompilerParams(
            dimension_semantics=("arbitrary", "arbitrary"), vmem_limit_bytes=VMEM_LIMIT),
        name="ssd_mixer",
    )(x, mod, nw, conv0, h0, *[p[n] for n in names])


def _ffn_kernel(x_ref, mod_ref, nw_ref, wg_ref, wu_ref, wd_ref, o_ref, *, nchunk):
    x = x_ref[0]
    mod = mod_ref[0]
    hn = (_rms(x, nw_ref[...]) * (1.0 + mod[4:5]) + mod[3:4]).astype(BF16)
    fc = wg_ref.shape[1] // nchunk
    acc = None
    for j in range(nchunk):
        g = _dot(hn, wg_ref[:, j * fc:(j + 1) * fc])
        u = _dot(hn, wu_ref[:, j * fc:(j + 1) * fc])
        part = _dot((_silu(g) * u).astype(BF16), wd_ref[j * fc:(j + 1) * fc, :])
        acc = part if acc is None else acc + part
    o_ref[0] = x + mod[5:6] * acc


def _ffn_layer(x, mod, nw, p, tm):
    bsz, t, d = x.shape
    tm = min(t, 2 * tm)
    return pl.pallas_call(
        functools.partial(_ffn_kernel, nchunk=p['w_gate'].shape[1] // (2 * LANES)),
        out_shape=jax.ShapeDtypeStruct(x.shape, F32),
        grid=(bsz, t // tm),
        in_specs=[pl.BlockSpec((1, tm, d), lambda b, i: (b, i, 0)),
                  pl.BlockSpec((1, 6, d), lambda b, i: (b, 0, 0)),
                  _const_spec((1, d)),
                  _const_spec(p['w_gate'].shape), _const_spec(p['w_up'].shape), _const_spec(p['w_down'].shape)],
        out_specs=pl.BlockSpec((1, tm, d), lambda b, i: (b, i, 0)),
        compiler_params=pltpu.CompilerParams(
            dimension_semantics=("arbitrary", "arbitrary"), vmem_limit_bytes=VMEM_LIMIT),
        name="ffn",
    )(x, mod, nw, p['w_gate'], p['w_up'], p['w_down'])


def _mlstm_kernel(x_ref, mod_ref, nw_ref, conv0_ref, c0_ref, n0_ref, m0_ref,
                  wxm_ref, wo_ref, cw_ref, cb_ref, wq_ref, wk_ref, wkt_ref, wv_ref, wgq_ref, wgk_ref, wgv_ref, bg_ref,
                  gnw_ref, skip_ref, wout_ref, eyei_ref, eyef_ref,
                  xo_ref, convo_ref, co_ref, no_ref, mo_ref,
                  xm_buf, xc_buf, q_buf, k_buf, kt_buf, v_buf, op_buf, hh_buf,
                  *, R, L):
    t = pl.program_id(1)
    hd_dim = wq_ref.shape[1]
    rep = hd_dim // LANES
    k_scale = hd_dim ** -0.5

    @pl.when(t == 0)
    def _():
        for r in range(R):
            xm_buf[r, TAIL0:8, :] = conv0_ref[r]
        co_ref[...] = c0_ref[...]
        no_ref[...] = n0_ref[...]
        mo_ref[...] = m0_ref[...]

    nw = nw_ref[...]
    hn = jnp.concatenate(
        [(_rms(x_ref[r], nw) * (1.0 + mod_ref[r][1:2]) + mod_ref[r][0:1]).astype(BF16) for r in range(R)], axis=0)
    xm = _dot(hn, wxm_ref[...])
    for r in range(R):
        xm_buf[r, 8:8 + L, :] = xm[r * L:(r + 1) * L]
    op_buf[...] = _dot(hn, wo_ref[...])
    for r in range(R):
        _causal_conv_silu(xm_buf.at[r], xc_buf.at[r], L, cw_ref, cb_ref)
        tail = xm_buf[r, L + TAIL0:L + 8, :]
        xm_buf[r, TAIL0:8, :] = tail
        convo_ref[r] = tail

    gates = bg_ref[...]
    for h in range(ML_HEADS):
        sl = slice(h * hd_dim, (h + 1) * hd_dim)
        xc_h = jnp.concatenate([xc_buf[r, :, sl].astype(BF16) for r in range(R)], axis=0)
        xm_h = jnp.concatenate([xm_buf[r, 8:8 + L, sl].astype(BF16) for r in range(R)], axis=0)
        q = _dot(xc_h, wq_ref[h])
        k = _dot(xc_h, wk_ref[h])
        v = _dot(xm_h, wv_ref[h])
        gates = gates + _dot(q.astype(BF16), wgq_ref[sl, :]) + _dot(k.astype(BF16), wgk_ref[sl, :]) \
            + _dot(v.astype(BF16), wgv_ref[sl, :])
        q_buf[:, sl] = q.astype(BF16)
        k_buf[:, sl] = (k * k_scale).astype(BF16)
        v_buf[:, sl] = v.astype(BF16)
        for r in range(R):
            kt_buf[r, h] = _dot_nt(wkt_ref[h], xc_h[r * L:(r + 1) * L]) * k_scale
    lf = jnp.minimum(gates, 0.0) - jnp.log1p(jnp.exp(-jnp.abs(gates)))
    tri = _tri_mask(L)
    tri_b = jnp.where(tri, 1.0, 0.0).astype(BF16)
    lane = lax.broadcasted_iota(jnp.int32, (1, LANES), 1)
    wide = lambda a: jnp.concatenate([a] * rep, axis=1)

    for r in range(R):
        rs = slice(r * L, (r + 1) * L)
        gi = gates[rs]
        bcum = _dot3_l(tri_b, lf[rs])
        li_t = _transpose_rows(gi, eyei_ref[...])
        b_t = _transpose_rows(bcum, eyef_ref[...])
        m_prev = mo_ref[r]
        m_next = m_prev
        for h in range(ML_HEADS):
            sl = slice(h * hd_dim, (h + 1) * hd_dim)
            bcol = jnp.broadcast_to(bcum[:, ML_HEADS + h:ML_HEADS + h + 1], (L, LANES))
            mp = jnp.broadcast_to(m_prev[:, h:h + 1], (1, LANES))
            logw_t = li_t[h:h + 1, :] - b_t[h:h + 1, :]
            dmat = jnp.where(tri, bcol[:, :L] + logw_t, -jnp.inf)
            a_inter = bcol + mp
            m_t = jnp.maximum(a_inter, jnp.max(dmat, axis=1, keepdims=True))
            qb = q_buf[rs, sl]
            vb = v_buf[rs, sl]
            s_mat = jnp.exp(dmat - m_t[:, :L]) * _dot_nt(qb, k_buf[rs, sl])
            w_inter = jnp.exp(a_inter - m_t)
            ch = co_ref[r, h]
            nh = no_ref[r, h:h + 1, :]
            num = _dot(s_mat.astype(BF16), vb) + wide(w_inter) * _dot(qb, ch.astype(BF16))
            qn = jnp.sum(qb.astype(F32) * nh, axis=1, keepdims=True)
            den = jnp.sum(s_mat, axis=1, keepdims=True) + w_inter * qn
            inv = 1.0 / jnp.maximum(jnp.abs(den), jnp.exp(-m_t))
            hh_buf[r, :, sl] = num * wide(inv)
            m_new = m_t[L - 1:L, :]
            b_last = bcol[L - 1:L, :]
            w_end = jnp.exp(b_last[:, :L] + logw_t - m_new[:, :L])
            w_old = wide(jnp.exp(b_last + mp - m_new))
            kw_t = (kt_buf[r, h] * w_end).astype(BF16)
            co_ref[r, h] = w_old * ch + _dot(kw_t, vb)
            w8 = jnp.broadcast_to(w_end, (8, L)).astype(BF16)
            no_ref[r, h:h + 1, :] = w_old * nh + _dot(w8, k_buf[rs, sl])[0:1, :]
            m_next = jnp.where(lane == h, m_new, m_next)
        mo_ref[r] = m_next

    gnw = gnw_ref[...]
    skip = skip_ref[...]
    rows = []
    for r in range(R):
        parts = []
        for h in range(ML_HEADS):
            sl = slice(h * hd_dim, (h + 1) * hd_dim)
            hn_h = _rms(hh_buf[r, :, sl], gnw[:, sl])
            parts.append(((hn_h + skip[:, sl] * xc_buf[r, :, sl])
                          * _sigmoid(op_buf[r * L:(r + 1) * L, sl])).astype(BF16))
        rows.append(jnp.concatenate(parts, axis=-1))
    out = _dot(jnp.concatenate(rows, axis=0), wout_ref[...])
    for r in range(R):
        xo_ref[r] = x_ref[r] + mod_ref[r][2:3] * out[r * L:(r + 1) * L]


def _mlstm_layer(x, mod, nw, conv0, c0, n0, m0, p, R, L):
    bsz, t, d = x.shape
    inner = p['conv_w'].shape[1]
    hd = inner // ML_HEADS
    blk = lambda b, i: (b, 0, 0)
    blk4 = lambda b, i: (b, 0, 0, 0)
    names = ['wxm', 'wo', 'conv_w', 'conv_b', 'w_q', 'w_k', 'w_kt', 'w_v', 'wgq', 'wgk', 'wgv', 'bg',
             'norm', 'skip', 'w_out', 'eye_i', 'eye_f']
    in_specs = [
        pl.BlockSpec((R, L, d), lambda b, i: (b, i, 0)),
        pl.BlockSpec((R, 6, d), blk),
        _const_spec((1, d)),
        pl.BlockSpec((R, CONV_W - 1, inner), blk),
        pl.BlockSpec((R, ML_HEADS, hd, hd), blk4, pipeline_mode=pl.Buffered(1)),
        pl.BlockSpec((R, ML_HEADS, hd), blk),
        pl.BlockSpec((R, 1, LANES), blk),
    ] + [_const_spec(p[n].shape) for n in names]
    out_shape = (jax.ShapeDtypeStruct((bsz, t, d), F32),
                 jax.ShapeDtypeStruct((bsz, CONV_W - 1, inner), F32),
                 jax.ShapeDtypeStruct((bsz, ML_HEADS, hd, hd), F32),
                 jax.ShapeDtypeStruct((bsz, ML_HEADS, hd), F32),
                 jax.ShapeDtypeStruct((bsz, 1, LANES), F32))
    out_specs = (pl.BlockSpec((R, L, d), lambda b, i: (b, i, 0)),
                 pl.BlockSpec((R, CONV_W - 1, inner), blk),
                 pl.BlockSpec((R, ML_HEADS, hd, hd), blk4),
                 pl.BlockSpec((R, ML_HEADS, hd), blk),
                 pl.BlockSpec((R, 1, LANES), blk))
    scratch = [pltpu.VMEM((R, 8 + L, inner), F32), pltpu.VMEM((R, L, inner), F32),
               pltpu.VMEM((R * L, inner), BF16), pltpu.VMEM((R * L, inner), BF16),
               pltpu.VMEM((R, ML_HEADS, hd, L), F32), pltpu.VMEM((R * L, inner), BF16),
               pltpu.VMEM((R * L, inner), F32), pltpu.VMEM((R, L, inner), F32)]
    return pl.pallas_call(
        functools.partial(_mlstm_kernel, R=R, L=L),
        out_shape=out_shape, grid=(bsz // R, t // L), in_specs=in_specs, out_specs=out_specs,
        scratch_shapes=scratch,
        compiler_params=pltpu.CompilerParams(
            dimension_semantics=("arbitrary", "arbitrary"), vmem_limit_bytes=VMEM_LIMIT),
        name="mlstm_mixer",
    )(x, mod, nw, conv0, c0, n0, m0, *[p[n] for n in names])


def _router_kernel(x_ref, mod_ref, nw_ref, wr_ref, br_ref, eye_ref,
                   h_ref, rc_ref, rr_ref, comb_ref, cnt_ref, *, tm):
    x = x_ref[0]
    mod = mod_ref[0]
    h = _rms(x, nw_ref[...]) * (1.0 + mod[4:5]) + mod[3:4]
    h_ref[0] = h.astype(BF16)
    h1, h2, h3 = _split3(h)
    w1, w2, w3 = _split3(wr_ref[...])
    logits = (_dot(h1, w1) + (_dot(h1, w2) + _dot(h2, w1))
              + (_dot(h1, w3) + _dot(h2, w2) + _dot(h3, w1))) + br_ref[...]
    lane = lax.broadcasted_iota(jnp.int32, (tm, LANES), 1)
    lg = jnp.where(lane < N_EXPERTS, logits, -jnp.inf)
    m1 = jnp.max(lg, axis=1, keepdims=True)
    i1 = jnp.min(jnp.where(lg == m1, lane, LANES), axis=1, keepdims=True)
    lg2 = jnp.where(lane == i1, -jnp.inf, lg)
    m2 = jnp.max(lg2, axis=1, keepdims=True)
    i2 = jnp.min(jnp.where(lg2 == m2, lane, LANES), axis=1, keepdims=True)
    e2 = jnp.exp(m2 - m1)
    w_top1 = 1.0 / (1.0 + e2)
    w_top2 = e2 / (1.0 + e2)
    comb_ref[0] = jnp.where(lane == i1, w_top1, 0.0) + jnp.where(lane == i2, w_top2, 0.0)
    sel = jnp.logical_or(lane == i1, lane == i2)
    mask = jnp.where(sel, 1.0, 0.0).astype(BF16)
    r = lax.broadcasted_iota(jnp.int32, (tm, tm), 0)
    c = lax.broadcasted_iota(jnp.int32, (tm, tm), 1)
    before_col = jnp.where(c < r, 1.0, 0.0).astype(BF16)
    before_row = jnp.where(r < c, 1.0, 0.0).astype(BF16)
    rank_c = _dot(before_col, mask)
    rc_ref[0] = jnp.where(sel, rank_c, -1.0)
    mask_t = _dot_nt(eye_ref[...], mask)
    rank_r = _dot(mask_t.astype(BF16), before_row)
    rr_ref[0, 0] = jnp.where(mask_t > 0.5, rank_r, -1.0)
    cnt = jnp.sum(mask.astype(F32), axis=0, keepdims=True)
    cnt_ref[0, 0] = jnp.broadcast_to(cnt, (8, LANES)).astype(jnp.int32)


def _router(x, mod, nw, p, tm):
    bsz, t, d = x.shape
    nt = t // tm
    ti = lambda b, i: (b, i, 0)
    t4 = lambda b, i: (b, i, 0, 0)
    out_shape = (jax.ShapeDtypeStruct((bsz, t, d), BF16),
                 jax.ShapeDtypeStruct((bsz, t, LANES), F32),
                 jax.ShapeDtypeStruct((bsz, nt, 16, tm), F32),
                 jax.ShapeDtypeStruct((bsz, t, LANES), F32),
                 jax.ShapeDtypeStruct((bsz, nt, 8, LANES), jnp.int32))
    out_specs = (pl.BlockSpec((1, tm, d), ti), pl.BlockSpec((1, tm, LANES), ti),
                 pl.BlockSpec((1, 1, 16, tm), t4), pl.BlockSpec((1, tm, LANES), ti),
                 pl.BlockSpec((1, 1, 8, LANES), t4))
    return pl.pallas_call(
        functools.partial(_router_kernel, tm=tm),
        out_shape=out_shape, grid=(bsz, nt),
        in_specs=[pl.BlockSpec((1, tm, d), ti), pl.BlockSpec((1, 6, d), lambda b, i: (b, 0, 0)),
                  _const_spec((1, d)), _const_spec(p['w_router'].shape), _const_spec(p['b_router'].shape),
                  _const_spec(p['eye16'].shape)],
        out_specs=out_specs,
        compiler_params=pltpu.CompilerParams(
            dimension_semantics=("arbitrary", "arbitrary"), vmem_limit_bytes=VMEM_LIMIT),
        name="router",
    )(x, mod, nw, p['w_router'], p['b_router'], p['eye16'])


def _expert_kernel(cnt_ref, acc_ref, h_ref, rc_ref, rr_ref, comb_ref, mod_ref, wg_ref, wu_ref, wd_ref, fw_ref,
                   o_ref, *, e, tm, bounds, final):
    idx = pl.program_id(0) * pl.num_programs(1) + pl.program_id(1)
    cnt = cnt_ref[idx]
    o_ref[0] = acc_ref[0]
    scale = mod_ref[0][5:6] * comb_ref[0][:, e:e + 1]

    for start, stop in bounds:
        rows = stop - start

        @pl.when(cnt > start)
        def _():
            rr = rr_ref[0, 0][e:e + 1, :]
            slot_r = lax.broadcasted_iota(jnp.int32, (rows, tm), 0).astype(F32) + float(start)
            gather = jnp.where(slot_r == rr, 1.0, 0.0).astype(BF16)
            xc = _dot(gather, h_ref[0]).astype(BF16)
            g = _dot(xc, wg_ref[...])
            u = _dot(xc, wu_ref[...])
            yc = _dot((_silu(g) * u).astype(BF16), wd_ref[...])
            rc = rc_ref[0][:, e:e + 1]
            slot_c = lax.broadcasted_iota(jnp.int32, (tm, rows), 1).astype(F32) + float(start)
            scatter = jnp.where(slot_c == rc, 1.0, 0.0).astype(BF16)
            o_ref[0] = o_ref[0] + scale * _dot(scatter, yc.astype(BF16))

    if final:
        o_ref[0] = _rms(o_ref[0], fw_ref[...])


def _expert(e, counts_e, acc, h, rc, rr, comb, mod, wg, wu, wd, fw, tm, final):
    bsz, t, d = acc.shape
    nt = t // tm
    first = min(tm, max(16, (tm // 4 + tm // 16) // 16 * 16))
    edges = [0] + list(range(first, tm, LANES)) + [tm]
    bounds = tuple(zip(edges[:-1], edges[1:]))
    ti = lambda b, i, cnt: (b, i, 0)
    grid_spec = pltpu.PrefetchScalarGridSpec(
        num_scalar_prefetch=1, grid=(bsz, nt),
        in_specs=[pl.BlockSpec((1, tm, d), ti), pl.BlockSpec((1, tm, d), ti),
                  pl.BlockSpec((1, tm, LANES), ti), pl.BlockSpec((1, 1, 16, tm), lambda b, i, cnt: (b, i, 0, 0)),
                  pl.BlockSpec((1, tm, LANES), ti), pl.BlockSpec((1, 6, d), lambda b, i, cnt: (b, 0, 0)),
                  _const_spec(wg.shape), _const_spec(wu.shape), _const_spec(wd.shape), _const_spec(fw.shape)],
        out_specs=pl.BlockSpec((1, tm, d), ti))
    return pl.pallas_call(
        functools.partial(_expert_kernel, e=e, tm=tm, bounds=bounds, final=final),
        out_shape=jax.ShapeDtypeStruct(acc.shape, F32),
        grid_spec=grid_spec,
        compiler_params=pltpu.CompilerParams(
            dimension_semantics=("arbitrary", "arbitrary"), vmem_limit_bytes=VMEM_LIMIT),
        name="expert",
    )(counts_e, acc, h, rc, rr, comb, mod, wg, wu, wd, fw)


def _moe_layer(x, mod, nw, p, fw, tm):
    bsz, t, _ = x.shape
    h, rc, rr, comb, cnt = _router(x, mod, nw, p, tm)
    counts = jnp.transpose(cnt[:, :, 0, :N_EXPERTS].reshape(bsz * (t // tm), N_EXPERTS))
    acc = x
    for e in range(N_EXPERTS):
        acc = _expert(e, counts[e], acc, h, rc, rr, comb, mod, p['w_gate'][e], p['w_up'][e], p['w_down'][e],
                      fw, tm, final=(e == N_EXPERTS - 1))
    return acc


def _cast_kernel(x_ref, o_ref):
    o_ref[...] = x_ref[...].astype(BF16)


def _to_bf16(a, tr=512):
    a2 = a.reshape(-1, a.shape[-1])
    rows, cols = a2.shape
    assert rows % tr == 0
    out = pl.pallas_call(
        _cast_kernel,
        out_shape=jax.ShapeDtypeStruct(a2.shape, BF16),
        grid=(rows // tr,),
        in_specs=[pl.BlockSpec((tr, cols), lambda i: (i, 0))],
        out_specs=pl.BlockSpec((tr, cols), lambda i: (i, 0)),
        compiler_params=pltpu.CompilerParams(
            dimension_semantics=("arbitrary",), vmem_limit_bytes=VMEM_LIMIT),
        name="to_bf16",
    )(a2)
    return out.reshape(a.shape)


def _pad_lanes(a):
    return jnp.pad(a, [(0, 0)] * (a.ndim - 1) + [(0, LANES - a.shape[-1])])


def _one_hot_rows(n, offset):
    r = lax.broadcasted_iota(jnp.int32, (n, LANES), 0)
    c = lax.broadcasted_iota(jnp.int32, (n, LANES), 1)
    return (c == r + offset).astype(BF16)


def _prep_ssm(w_in, conv_w, conv_b, dt_bias, a_log, d_skip, norm_w, w_out):
    d_in = SSM_GROUPS * SSM_HPG * SSM_HEAD_DIM
    cdim = conv_w.shape[1]
    heads = SSM_GROUPS * SSM_HPG
    r = lax.broadcasted_iota(jnp.int32, (LANES, d_in), 0)
    c = lax.broadcasted_iota(jnp.int32, (LANES, d_in), 1)
    return dict(
        wz=w_in[:, :d_in].astype(BF16), wx=w_in[:, d_in:d_in + cdim].astype(BF16),
        wdt=_pad_lanes(w_in[:, d_in + cdim:]).astype(BF16),
        conv_w=conv_w, conv_b=conv_b.reshape(1, cdim),
        dt_bias=_pad_lanes(dt_bias.reshape(1, heads)), a_log=_pad_lanes(a_log.reshape(1, heads)),
        d_full=jnp.repeat(d_skip, SSM_HEAD_DIM).reshape(1, d_in),
        norm=norm_w.reshape(1, d_in), w_out=w_out.astype(BF16),
        eye=_one_hot_rows(heads, 0), expand=(c // SSM_HEAD_DIM == r).astype(BF16))


def _prep_mlstm(w_in, conv_w, conv_b, w_q, w_k, w_v, w_ig, b_ig, w_fg, b_fg, norm_w, skip, w_out):
    inner = conv_w.shape[1]
    hd = inner // ML_HEADS
    wg = jnp.concatenate([w_ig, w_fg], axis=-1).reshape(ML_HEADS, 3, hd, 2 * ML_HEADS)
    part = lambda j: _pad_lanes(wg[:, j].reshape(inner, 2 * ML_HEADS)).astype(BF16)
    return dict(
        wxm=w_in[:, :inner].astype(BF16), wo=w_in[:, inner:].astype(BF16),
        conv_w=conv_w, conv_b=conv_b.reshape(1, inner),
        w_q=w_q.astype(BF16), w_k=w_k.astype(BF16), w_kt=jnp.swapaxes(w_k, 1, 2).astype(BF16),
        w_v=w_v.astype(BF16),
        wgq=part(0), wgk=part(1), wgv=part(2),
        bg=_pad_lanes(jnp.concatenate([b_ig, b_fg]).reshape(1, 2 * ML_HEADS)),
        norm=norm_w.reshape(1, inner), skip=skip.reshape(1, inner), w_out=w_out.astype(BF16),
        eye_i=_one_hot_rows(ML_HEADS, 0), eye_f=_one_hot_rows(ML_HEADS, ML_HEADS))


def _trunk(x, mod, ssm_conv, ssm_state, ml_conv, ml_c, ml_n, ml_m, p):
    bsz, t, d = x.shape
    L = SCAN_CHUNK if t % SCAN_CHUNK == 0 else t
    R = 2 if bsz % 2 == 0 else 1
    tm = min(t, 512)
    n_heads = SSM_GROUPS * SSM_HPG
    h0 = ssm_state.reshape(bsz, SSM_GROUPS, SSM_HPG, SSM_HEAD_DIM, SSM_STATE)
    h0 = h0.transpose(0, 1, 4, 2, 3).reshape(bsz, SSM_GROUPS, SSM_STATE, SSM_HPG * SSM_HEAD_DIM)
    x, conv_s, h_s = _ssd_layer(x, mod[0], p['norm_mix'][0], ssm_conv, h0, p['ssm'], R, L)
    h_s = h_s.reshape(bsz, SSM_GROUPS, SSM_STATE, SSM_HPG, SSM_HEAD_DIM).transpose(0, 1, 3, 4, 2)
    h_s = h_s.reshape(bsz, n_heads, SSM_HEAD_DIM, SSM_STATE)
    x = _ffn_layer(x, mod[0], p['norm_ffn'][0], p['ffn'], tm)
    m0 = _pad_lanes(ml_m).reshape(bsz, 1, LANES)
    x, conv_m, c_m, n_m, m_m = _mlstm_layer(x, mod[1], p['norm_mix'][1], ml_conv, ml_c, ml_n, m0, p['ml'], R, L)
    y = _moe_layer(x, mod[1], p['norm_ffn'][1], p['moe'], p['norm_final'], tm)
    return (y, conv_s[None], h_s[None], conv_m[None], c_m[None], n_m[None], m_m[:, 0, :ML_HEADS][None])


def kernel(x_prompt, x_sample, c_prompt, c_sample, state_ssm_conv, state_ssm, state_mlstm_conv, state_mlstm_C, state_mlstm_n, state_mlstm_m, w_ada, b_ada, norm_mix, norm_ffn, norm_final, ssm_w_in, ssm_conv_w, ssm_conv_b, ssm_dt_bias, ssm_a_log, ssm_d, ssm_norm, ssm_w_out, ml_w_in, ml_conv_w, ml_conv_b, ml_w_q, ml_w_k, ml_w_v, ml_w_igate, ml_b_igate, ml_w_fgate, ml_b_fgate, ml_norm, ml_skip, ml_w_out, ffn_w_gate, ffn_w_up, ffn_w_down, moe_w_router, moe_b_router, moe_w_gate, moe_w_up, moe_w_down):
    depth, d, _ = w_ada.shape
    assert depth == 2 and state_ssm.shape[0] == 1 and state_mlstm_C.shape[0] == 1
    bp, bs = x_prompt.shape[0], x_sample.shape[0]
    p = dict(
        norm_mix=norm_mix.reshape(depth, 1, d), norm_ffn=norm_ffn.reshape(depth, 1, d),
        norm_final=norm_final.reshape(1, d),
        ssm=_prep_ssm(ssm_w_in[0], ssm_conv_w[0], ssm_conv_b[0], ssm_dt_bias[0], ssm_a_log[0], ssm_d[0],
                      ssm_norm[0], ssm_w_out[0]),
        ml=_prep_mlstm(ml_w_in[0], ml_conv_w[0], ml_conv_b[0], ml_w_q[0], ml_w_k[0], ml_w_v[0], ml_w_igate[0],
                       ml_b_igate[0], ml_w_fgate[0], ml_b_fgate[0], ml_norm[0], ml_skip[0], ml_w_out[0]),
        ffn=dict(w_gate=ffn_w_gate[0].astype(BF16), w_up=ffn_w_up[0].astype(BF16),
                 w_down=ffn_w_down[0].astype(BF16)),
        moe=dict(w_router=_pad_lanes(moe_w_router[0]), b_router=_pad_lanes(moe_b_router[0].reshape(1, -1)),
                 eye16=_one_hot_rows(16, 0),
                 w_gate=_to_bf16(moe_w_gate[0]), w_up=_to_bf16(moe_w_up[0]), w_down=_to_bf16(moe_w_down[0])))
    mod = _ada(jnp.concatenate([c_prompt, c_sample], axis=0), w_ada, b_ada)
    mod = mod.reshape(depth, bp + bs, 6, d)

    f = F32
    zeros = lambda a, b: jnp.zeros((b,) + a.shape[2:], f)
    out_p = _trunk(x_prompt, mod[:, :bp], zeros(state_ssm_conv, bp), zeros(state_ssm, bp),
                   zeros(state_mlstm_conv, bp), zeros(state_mlstm_C, bp), zeros(state_mlstm_n, bp),
                   zeros(state_mlstm_m, bp), p)
    out_s = _trunk(x_sample, mod[:, bp:], state_ssm_conv[0], state_ssm[0], state_mlstm_conv[0],
                   state_mlstm_C[0], state_mlstm_n[0], state_mlstm_m[0], p)
    return (out_p[0], out_s[0]) + tuple(out_p[1:]) + tuple(out_s[1:])
```

```python
import functools

import jax
import jax.numpy as jnp
from jax import lax
from jax.experimental import pallas as pl
from jax.experimental.pallas import tpu as pltpu

F32 = jnp.float32
BF16 = jnp.bfloat16
RMS_EPS = 1e-6
CONV_W = 4
LANES = 128
SCAN_CHUNK = LANES
TAIL0 = 8 - (CONV_W - 1)
VMEM_LIMIT = 60 * 1024 * 1024

SSM_GROUPS = 4
SSM_HPG = 8
SSM_HEAD_DIM = 64
SSM_STATE = 128
ML_HEADS = 8
N_EXPERTS = 8


def _dot(a, b):
    return jnp.dot(a, b, preferred_element_type=F32)


def _dot_nt(a, b):
    return lax.dot_general(a, b, (((1,), (1,)), ((), ())), preferred_element_type=F32)


def _dot_tn(a, b):
    return lax.dot_general(a, b, (((0,), (0,)), ((), ())), preferred_element_type=F32)


def _split3(x):
    h1 = x.astype(BF16)
    r = x - h1.astype(F32)
    h2 = r.astype(BF16)
    r = r - h2.astype(F32)
    return h1, h2, r.astype(BF16)


def _dot3_l(sel, x):
    return sum(_dot(sel, p) for p in _split3(x))


def _dot3_r(x, sel):
    return sum(_dot(p, sel) for p in _split3(x))


def _transpose_rows(x, eye):
    return sum(_dot_nt(eye, p) for p in _split3(x))


def _sigmoid(x):
    return 1.0 / (1.0 + jnp.exp(-x))


def _silu(x):
    return x * _sigmoid(x)


def _softplus(x):
    return jnp.maximum(x, 0.0) + jnp.log1p(jnp.exp(-jnp.abs(x)))


def _rms(x, g):
    return x * lax.rsqrt(jnp.mean(x * x, axis=-1, keepdims=True) + RMS_EPS) * g


def _tri_mask(n):
    r = lax.broadcasted_iota(jnp.int32, (n, n), 0)
    c = lax.broadcasted_iota(jnp.int32, (n, n), 1)
    return r >= c


def _causal_conv_silu(buf, dst, tt, cw_ref, cb_ref):
    xn = buf[8:8 + tt, :]
    x1 = pltpu.roll(xn, 1, axis=0)
    w = [cw_ref[k:k + 1, :] for k in range(CONV_W)]
    y = cb_ref[...] + xn * w[3] + x1 * w[2] + pltpu.roll(xn * w[1] + x1 * w[0], 2, axis=0)
    dst[...] = _silu(y)
    head = cb_ref[...]
    for k in range(CONV_W):
        head = head + buf[TAIL0 + k:TAIL0 + k + 8, :] * cw_ref[k:k + 1, :]
    dst[0:8, :] = _silu(head)


def _ada_kernel(c_ref, w_ref, b_ref, o_ref):
    ca = _silu(c_ref[...]).astype(BF16)
    o_ref[0] = _dot(ca, w_ref[0].astype(BF16)) + b_ref[0]


def _ada(c_all, w_ada, b_ada):
    depth, d, n = w_ada.shape
    bt = c_all.shape[0]
    tn = n // 4
    return pl.pallas_call(
        _ada_kernel,
        out_shape=jax.ShapeDtypeStruct((depth, bt, n), F32),
        grid=(depth, n // tn),
        in_specs=[pl.BlockSpec((bt, d), lambda i, j: (0, 0)),
                  pl.BlockSpec((1, d, tn), lambda i, j: (i, 0, j)),
                  pl.BlockSpec((1, 1, tn), lambda i, j: (i, 0, j))],
        out_specs=pl.BlockSpec((1, bt, tn), lambda i, j: (i, 0, j)),
        compiler_params=pltpu.CompilerParams(
            dimension_semantics=("arbitrary", "arbitrary"), vmem_limit_bytes=VMEM_LIMIT),
        name="ada",
    )(c_all, w_ada, b_ada.reshape(depth, 1, n))


def _const_spec(shape):
    nd = len(shape)
    return pl.BlockSpec(shape, lambda *_: (0,) * nd, pipeline_mode=pl.Buffered(1))


def _ssd_kernel(x_ref, mod_ref, nw_ref, conv0_ref, h0_ref, wz_ref, wx_ref, wdt_ref, cw_ref, cb_ref,
                dtb_ref, alog_ref, dfull_ref, gnw_ref, wout_ref, eye_ref, expand_ref,
                xo_ref, convo_ref, ho_ref,
                xbc_buf, act_buf, z_buf, y_buf, xw_buf, *, R, L):
    t = pl.program_id(1)
    inner = SSM_HPG * SSM_HEAD_DIM
    d_in = SSM_GROUPS * inner
    gn = SSM_GROUPS * SSM_STATE

    @pl.when(t == 0)
    def _():
        for r in range(R):
            xbc_buf[r, TAIL0:8, :] = conv0_ref[r]
        ho_ref[...] = h0_ref[...]

    nw = nw_ref[...]
    hn = jnp.concatenate(
        [(_rms(x_ref[r], nw) * (1.0 + mod_ref[r][1:2]) + mod_ref[r][0:1]).astype(BF16) for r in range(R)], axis=0)
    z_buf[...] = _dot(hn, wz_ref[...])
    xbc = _dot(hn, wx_ref[...])
    dt = _softplus(_dot(hn, wdt_ref[...]) + dtb_ref[...])
    a = -jnp.exp(alog_ref[...])
    tri = _tri_mask(L)
    tri_b = jnp.where(tri, 1.0, 0.0).astype(BF16)
    eye = eye_ref[...]
    expand = expand_ref[...]
    left = lax.broadcasted_iota(jnp.int32, (1, LANES), 1) < SSM_HEAD_DIM

    for r in range(R):
        xbc_buf[r, 8:8 + L, :] = xbc[r * L:(r + 1) * L]
        _causal_conv_silu(xbc_buf.at[r], act_buf.at[r], L, cw_ref, cb_ref)
        tail = xbc_buf[r, L + TAIL0:L + 8, :]
        xbc_buf[r, TAIL0:8, :] = tail
        convo_ref[r] = tail

    for r in range(R):
        dtc = dt[r * L:(r + 1) * L]
        cum = _dot3_l(tri_b, dtc * a)
        cum_t = _transpose_rows(cum, eye)
        dt_t = _transpose_rows(dtc, eye)
        cum_last = cum[L - 1:L, :]
        wend = (jnp.exp(cum_last - cum) * dtc).astype(BF16)
        xw_buf[r] = (act_buf[r, :, :d_in] * _dot(wend, expand)).astype(BF16)
        dec_last = jnp.exp(_dot3_r(cum[L - 8:L, :], expand)[7:8, :])
        for g in range(SSM_GROUPS):
            bg = act_buf[r, :, d_in + g * SSM_STATE:d_in + (g + 1) * SSM_STATE].astype(BF16)
            cg = act_buf[r, :, d_in + gn + g * SSM_STATE:d_in + gn + (g + 1) * SSM_STATE].astype(BF16)
            cb = _dot_nt(cg, bg)
            hg = ho_ref[r, g]
            y_int = _dot(cg, hg.astype(BF16))
            for pr in range(SSM_HPG // 2):
                hd0 = g * SSM_HPG + 2 * pr
                lo = hd0 * SSM_HEAD_DIM
                xp = act_buf[r, :, lo:lo + LANES]
                x2 = jnp.concatenate([jnp.where(left, xp, 0.0), jnp.where(left, 0.0, xp)], axis=0).astype(BF16)
                wms, es = [], []
                for hd in (hd0, hd0 + 1):
                    ccol = jnp.broadcast_to(cum[:, hd:hd + 1], (L, LANES))
                    dec = jnp.exp(jnp.where(tri, ccol[:, :L] - cum_t[hd:hd + 1, :], -jnp.inf))
                    wms.append((cb * dec * dt_t[hd:hd + 1, :]).astype(BF16))
                    es.append(jnp.exp(ccol))
                yi = y_int[:, pr * LANES:(pr + 1) * LANES]
                y_buf[r, :, lo:lo + LANES] = (_dot(jnp.concatenate(wms, axis=1), x2)
                                              + yi * jnp.where(left, es[0], es[1]))
            ho_ref[r, g] = (hg * dec_last[:, g * inner:(g + 1) * inner]
                           + _dot_tn(bg, xw_buf[r, :, g * inner:(g + 1) * inner]))

    gnw = gnw_ref[...]
    rows = []
    for r in range(R):
        y = y_buf[r] + dfull_ref[...] * act_buf[r, :, :d_in]
        y = y * _silu(z_buf[r * L:(r + 1) * L, :])
        rows.append(jnp.concatenate(
            [_rms(y[:, g * inner:(g + 1) * inner], gnw[:, g * inner:(g + 1) * inner]).astype(BF16)
             for g in range(SSM_GROUPS)], axis=-1))
    out = _dot(jnp.concatenate(rows, axis=0), wout_ref[...])
    for r in range(R):
        xo_ref[r] = x_ref[r] + mod_ref[r][2:3] * out[r * L:(r + 1) * L]


def _ssd_layer(x, mod, nw, conv0, h0, p, R, L):
    bsz, t, d = x.shape
    d_in = SSM_GROUPS * SSM_HPG * SSM_HEAD_DIM
    cdim = p['conv_w'].shape[1]
    blk = lambda b, i: (b, 0, 0)
    st_blk = (R, SSM_GROUPS, SSM_STATE, SSM_HPG * SSM_HEAD_DIM)
    names = ['wz', 'wx', 'wdt', 'conv_w', 'conv_b', 'dt_bias', 'a_log', 'd_full', 'norm', 'w_out', 'eye', 'expand']
    in_specs = [
        pl.BlockSpec((R, L, d), lambda b, i: (b, i, 0)),
        pl.BlockSpec((R, 6, d), blk),
        _const_spec((1, d)),
        pl.BlockSpec((R, CONV_W - 1, cdim), blk),
        pl.BlockSpec(st_blk, lambda b, i: (b, 0, 0, 0)),
    ] + [_const_spec(p[n].shape) for n in names]
    out_shape = (jax.ShapeDtypeStruct((bsz, t, d), F32),
                 jax.ShapeDtypeStruct((bsz, CONV_W - 1, cdim), F32),
                 jax.ShapeDtypeStruct(h0.shape, F32))
    out_specs = (pl.BlockSpec((R, L, d), lambda b, i: (b, i, 0)),
                 pl.BlockSpec((R, CONV_W - 1, cdim), blk),
                 pl.BlockSpec(st_blk, lambda b, i: (b, 0, 0, 0)))
    scratch = [pltpu.VMEM((R, 8 + L, cdim), F32), pltpu.VMEM((R, L, cdim), F32), pltpu.VMEM((R * L, d_in), F32),
               pltpu.VMEM((R, L, d_in), F32), pltpu.VMEM((R, L, d_in), BF16)]
    return pl.pallas_call(
        functools.partial(_ssd_kernel, R=R, L=L),
        out_shape=out_shape, grid=(bsz // R, t // L), in_specs=in_specs, out_specs=out_specs,
        scratch_shapes=scratch,
        compiler_params=pltpu.CompilerParams(
            dimension_semantics=("arbitrary", "arbitrary"), vmem_limit_bytes=VMEM_LIMIT),
        name="ssd_mixer",
    )(x, mod, nw, conv0, h0, *[p[n] for n in names])


def _ffn_kernel(x_ref, mod_ref, nw_ref, wg_ref, wu_ref, wd_ref, o_ref, *, nchunk):
    x = x_ref[0]
    mod = mod_ref[0]
    hn = (_rms(x, nw_ref[...]) * (1.0 + mod[4:5]) + mod[3:4]).astype(BF16)
    fc = wg_ref.shape[1] // nchunk
    acc = None
    for j in range(nchunk):
        g = _dot(hn, wg_ref[:, j * fc:(j + 1) * fc])
        u = _dot(hn, wu_ref[:, j * fc:(j + 1) * fc])
        part = _dot((_silu(g) * u).astype(BF16), wd_ref[j * fc:(j + 1) * fc, :])
        acc = part if acc is None else acc + part
    o_ref[0] = x + mod[5:6] * acc


def _ffn_layer(x, mod, nw, p, tm):
    bsz, t, d = x.shape
    tm = min(t, 2 * tm)
    return pl.pallas_call(
        functools.partial(_ffn_kernel, nchunk=p['w_gate'].shape[1] // (2 * LANES)),
        out_shape=jax.ShapeDtypeStruct(x.shape, F32),
        grid=(bsz, t // tm),
        in_specs=[pl.BlockSpec((1, tm, d), lambda b, i: (b, i, 0)),
                  pl.BlockSpec((1, 6, d), lambda b, i: (b, 0, 0)),
                  _const_spec((1, d)),
                  _const_spec(p['w_gate'].shape), _const_spec(p['w_up'].shape), _const_spec(p['w_down'].shape)],
        out_specs=pl.BlockSpec((1, tm, d), lambda b, i: (b, i, 0)),
        compiler_params=pltpu.CompilerParams(
            dimension_semantics=("arbitrary", "arbitrary"), vmem_limit_bytes=VMEM_LIMIT),
        name="ffn",
    )(x, mod, nw, p['w_gate'], p['w_up'], p['w_down'])


def _mlstm_kernel(x_ref, mod_ref, nw_ref, conv0_ref, c0_ref, n0_ref, m0_ref,
                  wxm_ref, wo_ref, cw_ref, cb_ref, wq_ref, wk_ref, wkt_ref, wv_ref, wgq_ref, wgk_ref, wgv_ref, bg_ref,
                  gnw_ref, skip_ref, wout_ref, eyei_ref, eyef_ref,
                  xo_ref, convo_ref, co_ref, no_ref, mo_ref,
                  xm_buf, xc_buf, q_buf, k_buf, kt_buf, v_buf, op_buf, hh_buf,
                  *, R, L):
    t = pl.program_id(1)
    hd_dim = wq_ref.shape[1]
    rep = hd_dim // LANES
    k_scale = hd_dim ** -0.5

    @pl.when(t == 0)
    def _():
        for r in range(R):
            xm_buf[r, TAIL0:8, :] = conv0_ref[r]
        co_ref[...] = c0_ref[...]
        no_ref[...] = n0_ref[...]
        mo_ref[...] = m0_ref[...]

    nw = nw_ref[...]
    hn = jnp.concatenate(
        [(_rms(x_ref[r], nw) * (1.0 + mod_ref[r][1:2]) + mod_ref[r][0:1]).astype(BF16) for r in range(R)], axis=0)
    xm = _dot(hn, wxm_ref[...])
    for r in range(R):
        xm_buf[r, 8:8 + L, :] = xm[r * L:(r + 1) * L]
    op_buf[...] = _dot(hn, wo_ref[...])
    for r in range(R):
        _causal_conv_silu(xm_buf.at[r], xc_buf.at[r], L, cw_ref, cb_ref)
        tail = xm_buf[r, L + TAIL0:L + 8, :]
        xm_buf[r, TAIL0:8, :] = tail
        convo_ref[r] = tail

    gates = bg_ref[...]
    for h in range(ML_HEADS):
        sl = slice(h * hd_dim, (h + 1) * hd_dim)
        xc_h = jnp.concatenate([xc_buf[r, :, sl].astype(BF16) for r in range(R)], axis=0)
        xm_h = jnp.concatenate([xm_buf[r, 8:8 + L, sl].astype(BF16) for r in range(R)], axis=0)
        q = _dot(xc_h, wq_ref[h])
        k = _dot(xc_h, wk_ref[h])
        v = _dot(xm_h, wv_ref[h])
        gates = gates + _dot(q.astype(BF16), wgq_ref[sl, :]) + _dot(k.astype(BF16), wgk_ref[sl, :]) \
            + _dot(v.astype(BF16), wgv_ref[sl, :])
        q_buf[:, sl] = q.astype(BF16)
        k_buf[:, sl] = (k * k_scale).astype(BF16)
        v_buf[:, sl] = v.astype(BF16)
        for r in range(R):
            kt_buf[r, h] = _dot_nt(wkt_ref[h], xc_h[r * L:(r + 1) * L]) * k_scale
    lf = jnp.minimum(gates, 0.0) - jnp.log1p(jnp.exp(-jnp.abs(gates)))
    tri = _tri_mask(L)
    tri_b = jnp.where(tri, 1.0, 0.0).astype(BF16)
    lane = lax.broadcasted_iota(jnp.int32, (1, LANES), 1)
    wide = lambda a: jnp.concatenate([a] * rep, axis=1)

    for r in range(R):
        rs = slice(r * L, (r + 1) * L)
        gi = gates[rs]
        bcum = _dot3_l(tri_b, lf[rs])
        li_t = _transpose_rows(gi, eyei_ref[...])
        b_t = _transpose_rows(bcum, eyef_ref[...])
        m_prev = mo_ref[r]
        m_next = m_prev
        for h in range(ML_HEADS):
            sl = slice(h * hd_dim, (h + 1) * hd_dim)
            bcol = jnp.broadcast_to(bcum[:, ML_HEADS + h:ML_HEADS + h + 1], (L, LANES))
            mp = jnp.broadcast_to(m_prev[:, h:h + 1], (1, LANES))
            logw_t = li_t[h:h + 1, :] - b_t[h:h + 1, :]
            dmat = jnp.where(tri, bcol[:, :L] + logw_t, -jnp.inf)
            a_inter = bcol + mp
            m_t = jnp.maximum(a_inter, jnp.max(dmat, axis=1, keepdims=True))
            qb = q_buf[rs, sl]
            vb = v_buf[rs, sl]
            s_mat = jnp.exp(dmat - m_t[:, :L]) * _dot_nt(qb, k_buf[rs, sl])
            w_inter = jnp.exp(a_inter - m_t)
            ch = co_ref[r, h]
            nh = no_ref[r, h:h + 1, :]
            num = _dot(s_mat.astype(BF16), vb) + wide(w_inter) * _dot(qb, ch.astype(BF16))
            qn = jnp.sum(qb.astype(F32) * nh, axis=1, keepdims=True)
            den = jnp.sum(s_mat, axis=1, keepdims=True) + w_inter * qn
            inv = 1.0 / jnp.maximum(jnp.abs(den), jnp.exp(-m_t))
            hh_buf[r, :, sl] = num * wide(inv)
            m_new = m_t[L - 1:L, :]
            b_last = bcol[L - 1:L, :]
            w_end = jnp.exp(b_last[:, :L] + logw_t - m_new[:, :L])
            w_old = wide(jnp.exp(b_last + mp - m_new))
            kw_t = (kt_buf[r, h] * w_end).astype(BF16)
            co_ref[r, h] = w_old * ch + _dot(kw_t, vb)
            w8 = jnp.broadcast_to(w_end, (8, L)).astype(BF16)
            no_ref[r, h:h + 1, :] = w_old * nh + _dot(w8, k_buf[rs, sl])[0:1, :]
            m_next = jnp.where(lane == h, m_new, m_next)
        mo_ref[r] = m_next

    gnw = gnw_ref[...]
    skip = skip_ref[...]
    rows = []
    for r in range(R):
        parts = []
        for h in range(ML_HEADS):
            sl = slice(h * hd_dim, (h + 1) * hd_dim)
            hn_h = _rms(hh_buf[r, :, sl], gnw[:, sl])
            parts.append(((hn_h + skip[:, sl] * xc_buf[r, :, sl])
                          * _sigmoid(op_buf[r * L:(r + 1) * L, sl])).astype(BF16))
        rows.append(jnp.concatenate(parts, axis=-1))
    out = _dot(jnp.concatenate(rows, axis=0), wout_ref[...])
    for r in range(R):
        xo_ref[r] = x_ref[r] + mod_ref[r][2:3] * out[r * L:(r + 1) * L]


def _mlstm_layer(x, mod, nw, conv0, c0, n0, m0, p, R, L):
    bsz, t, d = x.shape
    inner = p['conv_w'].shape[1]
    hd = inner // ML_HEADS
    blk = lambda b, i: (b, 0, 0)
    blk4 = lambda b, i: (b, 0, 0, 0)
    names = ['wxm', 'wo', 'conv_w', 'conv_b', 'w_q', 'w_k', 'w_kt', 'w_v', 'wgq', 'wgk', 'wgv', 'bg',
             'norm', 'skip', 'w_out', 'eye_i', 'eye_f']
    in_specs = [
        pl.BlockSpec((R, L, d), lambda b, i: (b, i, 0)),
        pl.BlockSpec((R, 6, d), blk),
        _const_spec((1, d)),
        pl.BlockSpec((R, CONV_W - 1, inner), blk),
        pl.BlockSpec((R, ML_HEADS, hd, hd), blk4, pipeline_mode=pl.Buffered(1)),
        pl.BlockSpec((R, ML_HEADS, hd), blk),
        pl.BlockSpec((R, 1, LANES), blk),
    ] + [_const_spec(p[n].shape) for n in names]
    out_shape = (jax.ShapeDtypeStruct((bsz, t, d), F32),
                 jax.ShapeDtypeStruct((bsz, CONV_W - 1, inner), F32),
                 jax.ShapeDtypeStruct((bsz, ML_HEADS, hd, hd), F32),
                 jax.ShapeDtypeStruct((bsz, ML_HEADS, hd), F32),
                 jax.ShapeDtypeStruct((bsz, 1, LANES), F32))
    out_specs = (pl.BlockSpec((R, L, d), lambda b, i: (b, i, 0)),
                 pl.BlockSpec((R, CONV_W - 1, inner), blk),
                 pl.BlockSpec((R, ML_HEADS, hd, hd), blk4),
                 pl.BlockSpec((R, ML_HEADS, hd), blk),
                 pl.BlockSpec((R, 1, LANES), blk))
    scratch = [pltpu.VMEM((R, 8 + L, inner), F32), pltpu.VMEM((R, L, inner), F32),
               pltpu.VMEM((R * L, inner), BF16), pltpu.VMEM((R * L, inner), BF16),
               pltpu.VMEM((R, ML_HEADS, hd, L), F32), pltpu.VMEM((R * L, inner), BF16),
               pltpu.VMEM((R * L, inner), F32), pltpu.VMEM((R, L, inner), F32)]
    return pl.pallas_call(
        functools.partial(_mlstm_kernel, R=R, L=L),
        out_shape=out_shape, grid=(bsz // R, t // L), in_specs=in_specs, out_specs=out_specs,
        scratch_shapes=scratch,
        compiler_params=pltpu.CompilerParams(
            dimension_semantics=("arbitrary", "arbitrary"), vmem_limit_bytes=VMEM_LIMIT),
        name="mlstm_mixer",
    )(x, mod, nw, conv0, c0, n0, m0, *[p[n] for n in names])


def _router_kernel(x_ref, mod_ref, nw_ref, wr_ref, br_ref, h_ref, info_ref, cnt_ref, *, tm):
    x = x_ref[0]
    mod = mod_ref[0]
    h = _rms(x, nw_ref[...]) * (1.0 + mod[4:5]) + mod[3:4]
    h_ref[0] = h
    h1, h2, h3 = _split3(h)
    w1, w2, w3 = _split3(wr_ref[...])
    logits = (_dot(h1, w1) + (_dot(h1, w2) + _dot(h2, w1))
              + (_dot(h1, w3) + _dot(h2, w2) + _dot(h3, w1))) + br_ref[...]
    lane = lax.broadcasted_iota(jnp.int32, (tm, LANES), 1)
    lg = jnp.where(lane < N_EXPERTS, logits, -jnp.inf)
    m1 = jnp.max(lg, axis=1, keepdims=True)
    i1 = jnp.min(jnp.where(lg == m1, lane, LANES), axis=1, keepdims=True)
    lg2 = jnp.where(lane == i1, -jnp.inf, lg)
    m2 = jnp.max(lg2, axis=1, keepdims=True)
    i2 = jnp.min(jnp.where(lg2 == m2, lane, LANES), axis=1, keepdims=True)
    e2 = jnp.exp(m2 - m1)
    w_top1 = 1.0 / (1.0 + e2)
    w_top2 = e2 / (1.0 + e2)
    sel = jnp.logical_or(lane == i1, lane == i2)
    mask = jnp.where(sel, 1.0, 0.0).astype(BF16)
    r = lax.broadcasted_iota(jnp.int32, (tm, tm), 0)
    c = lax.broadcasted_iota(jnp.int32, (tm, tm), 1)
    before = jnp.where(c < r, 1.0, 0.0).astype(BF16)
    rank = _dot(before, mask)
    r1 = jnp.sum(jnp.where(lane == i1, rank, 0.0), axis=1, keepdims=True)
    r2 = jnp.sum(jnp.where(lane == i2, rank, 0.0), axis=1, keepdims=True)
    cols = (i1.astype(F32), i2.astype(F32), r1, r2, w_top1, w_top2)
    info = jnp.zeros((tm, LANES), F32)
    for k, col in enumerate(cols):
        info = jnp.where(lane == k, col, info)
    info_ref[0] = info
    cnt = jnp.sum(mask.astype(F32), axis=0, keepdims=True)
    cnt_ref[0, 0] = jnp.broadcast_to(cnt, (8, LANES)).astype(jnp.int32)


def _router(x, mod, nw, p, tm):
    bsz, t, d = x.shape
    nt = t // tm
    ti = lambda b, i: (b, i, 0)
    t4 = lambda b, i: (b, i, 0, 0)
    out_shape = (jax.ShapeDtypeStruct((bsz, t, d), F32),
                 jax.ShapeDtypeStruct((bsz, t, LANES), F32),
                 jax.ShapeDtypeStruct((bsz, nt, 8, LANES), jnp.int32))
    out_specs = (pl.BlockSpec((1, tm, d), ti), pl.BlockSpec((1, tm, LANES), ti),
                 pl.BlockSpec((1, 1, 8, LANES), t4))
    return pl.pallas_call(
        functools.partial(_router_kernel, tm=tm),
        out_shape=out_shape, grid=(bsz, nt),
        in_specs=[pl.BlockSpec((1, tm, d), ti), pl.BlockSpec((1, 6, d), lambda b, i: (b, 0, 0)),
                  _const_spec((1, d)), _const_spec(p['w_router'].shape), _const_spec(p['b_router'].shape)],
        out_specs=out_specs,
        compiler_params=pltpu.CompilerParams(
            dimension_semantics=("arbitrary", "arbitrary"), vmem_limit_bytes=VMEM_LIMIT),
        name="router",
    )(x, mod, nw, p['w_router'], p['b_router'])


SLAB = 512


def _dispatch_kernel(pos_ref, h_ref, xs_in_ref, xs_ref, sem, *, tm):
    del xs_in_ref

    def send(t, carry):
        for k in range(2):
            pltpu.make_async_copy(h_ref.at[0, pl.ds(t, 1), :],
                                  xs_ref.at[pl.ds(pos_ref[0, 0, k * tm + t], 1), :], sem).start()
        return carry

    lax.fori_loop(0, tm, send, 0)
    for k in range(2):
        pltpu.make_async_copy(h_ref.at[0], xs_ref.at[pl.ds(0, tm), :], sem).wait()


def _dispatch(h, pos, n_rows, tm):
    bsz, t, d = h.shape
    nt = t // tm
    return pl.pallas_call(
        functools.partial(_dispatch_kernel, tm=tm),
        out_shape=jax.ShapeDtypeStruct((n_rows, d), F32),
        grid=(bsz, nt),
        in_specs=[pl.BlockSpec((1, 1, 2 * tm), lambda b, i: (b * nt + i, 0, 0), memory_space=pltpu.SMEM),
                  pl.BlockSpec((1, tm, d), lambda b, i: (b, i, 0)),
                  pl.BlockSpec(memory_space=pl.ANY)],
        out_specs=pl.BlockSpec(memory_space=pl.ANY),
        scratch_shapes=[pltpu.SemaphoreType.DMA(())],
        input_output_aliases={2: 0},
        compiler_params=pltpu.CompilerParams(
            dimension_semantics=("arbitrary", "arbitrary"), vmem_limit_bytes=VMEM_LIMIT),
        name="dispatch",
    )(pos, h, jnp.zeros((n_rows, d), F32))


def _slab_ffn_kernel(se_ref, nu_ref, x_ref, wg_ref, wu_ref, wd_ref, y_ref, *, nchunk):
    s = pl.program_id(0)

    @pl.when(s < nu_ref[0])
    def _():
        xb = x_ref[...].astype(BF16)
        fc = wg_ref.shape[2] // nchunk
        acc = None
        for j in range(nchunk):
            g = _dot(xb, wg_ref[0, :, j * fc:(j + 1) * fc])
            u = _dot(xb, wu_ref[0, :, j * fc:(j + 1) * fc])
            part = _dot((_silu(g) * u).astype(BF16), wd_ref[0, j * fc:(j + 1) * fc, :])
            acc = part if acc is None else acc + part
        y_ref[...] = acc

    @pl.when(s >= nu_ref[0])
    def _():
        y_ref[...] = jnp.zeros(y_ref.shape, F32)


def _slab_ffn(xs, slab_expert, n_used, wg, wu, wd):
    n_rows, d = xs.shape
    f = wg.shape[2]
    w_idx = lambda s, se, nu: (se[s], 0, 0)
    grid_spec = pltpu.PrefetchScalarGridSpec(
        num_scalar_prefetch=2, grid=(n_rows // SLAB,),
        in_specs=[pl.BlockSpec((SLAB, d), lambda s, se, nu: (s, 0)),
                  pl.BlockSpec((1, d, f), w_idx), pl.BlockSpec((1, d, f), w_idx), pl.BlockSpec((1, f, d), w_idx)],
        out_specs=pl.BlockSpec((SLAB, d), lambda s, se, nu: (s, 0)))
    return pl.pallas_call(
        functools.partial(_slab_ffn_kernel, nchunk=f // (2 * LANES)),
        out_shape=jax.ShapeDtypeStruct((n_rows, d), F32),
        grid_spec=grid_spec,
        compiler_params=pltpu.CompilerParams(dimension_semantics=("arbitrary",), vmem_limit_bytes=VMEM_LIMIT),
        name="slab_ffn",
    )(slab_expert, n_used, xs, wg, wu, wd)


def _combine_kernel(pos_ref, x_ref, info_ref, mod_ref, fw_ref, ys_ref, o_ref, buf, sem, *, tm):
    def fetch(t, carry):
        for k in range(2):
            pltpu.make_async_copy(ys_ref.at[pl.ds(pos_ref[0, 0, k * tm + t], 1), :],
                                  buf.at[k, pl.ds(t, 1), :], sem).start()
        return carry

    lax.fori_loop(0, tm, fetch, 0)
    for k in range(2):
        pltpu.make_async_copy(ys_ref.at[pl.ds(0, tm), :], buf.at[k], sem).wait()
    info = info_ref[0]
    y = x_ref[0] + mod_ref[0][5:6] * (info[:, 4:5] * buf[0] + info[:, 5:6] * buf[1])
    o_ref[0] = _rms(y, fw_ref[...])


def _combine(x, info, mod, fw, ys, pos, tm):
    bsz, t, d = x.shape
    nt = t // tm
    ti = lambda b, i: (b, i, 0)
    return pl.pallas_call(
        functools.partial(_combine_kernel, tm=tm),
        out_shape=jax.ShapeDtypeStruct(x.shape, F32),
        grid=(bsz, nt),
        in_specs=[pl.BlockSpec((1, 1, 2 * tm), lambda b, i: (b * nt + i, 0, 0), memory_space=pltpu.SMEM),
                  pl.BlockSpec((1, tm, d), ti), pl.BlockSpec((1, tm, LANES), ti),
                  pl.BlockSpec((1, 6, d), lambda b, i: (b, 0, 0)), _const_spec(fw.shape),
                  pl.BlockSpec(memory_space=pl.ANY)],
        out_specs=pl.BlockSpec((1, tm, d), ti),
        scratch_shapes=[pltpu.VMEM((2, tm, d), F32), pltpu.SemaphoreType.DMA(())],
        compiler_params=pltpu.CompilerParams(
            dimension_semantics=("arbitrary", "arbitrary"), vmem_limit_bytes=VMEM_LIMIT),
        name="combine",
    )(pos, x, info, mod, fw, ys)


def _moe_layer(x, mod, nw, p, fw, tm):
    bsz, t, _ = x.shape
    n_tok = bsz * t
    tiles = n_tok // tm
    h, info, cnt = _router(x, mod, nw, p, tm)
    counts = cnt[:, :, 0, :N_EXPERTS].reshape(tiles, N_EXPERTS)
    base = jnp.cumsum(counts, axis=0) - counts
    slabs = (jnp.sum(counts, axis=0) + (SLAB - 1)) // SLAB
    slab_end = jnp.cumsum(slabs)
    start = (slab_end - slabs) * SLAB
    n_slabs = (2 * n_tok) // SLAB + N_EXPERTS
    experts = jnp.arange(N_EXPERTS, dtype=jnp.int32)
    slab_expert = jnp.minimum(jnp.sum(jnp.arange(n_slabs, dtype=jnp.int32)[:, None] >= slab_end[None, :], axis=1),
                              N_EXPERTS - 1).astype(jnp.int32)
    inf2 = info.reshape(tiles, tm, LANES)
    offs = (start[None, :] + base)[:, None, :]
    pos = []
    for k in range(2):
        e_k = inf2[:, :, k].astype(jnp.int32)
        off_k = jnp.sum(jnp.where(e_k[:, :, None] == experts[None, None, :], offs, 0), axis=-1)
        pos.append(off_k + inf2[:, :, 2 + k].astype(jnp.int32))
    pos = jnp.concatenate(pos, axis=1).reshape(tiles, 1, 2 * tm).astype(jnp.int32)
    xs = _dispatch(h, pos, n_slabs * SLAB, tm)
    ys = _slab_ffn(xs, slab_expert, slab_end[-1:].astype(jnp.int32), p['w_gate'], p['w_up'], p['w_down'])
    return _combine(x, info, mod, fw, ys, pos, tm)


def _cast_kernel(x_ref, o_ref):
    o_ref[...] = x_ref[...].astype(BF16)


def _to_bf16(a, tr=512):
    a2 = a.reshape(-1, a.shape[-1])
    rows, cols = a2.shape
    assert rows % tr == 0
    out = pl.pallas_call(
        _cast_kernel,
        out_shape=jax.ShapeDtypeStruct(a2.shape, BF16),
        grid=(rows // tr,),
        in_specs=[pl.BlockSpec((tr, cols), lambda i: (i, 0))],
        out_specs=pl.BlockSpec((tr, cols), lambda i: (i, 0)),
        compiler_params=pltpu.CompilerParams(
            dimension_semantics=("arbitrary",), vmem_limit_bytes=VMEM_LIMIT),
        name="to_bf16",
    )(a2)
    return out.reshape(a.shape)


def _pad_lanes(a):
    return jnp.pad(a, [(0, 0)] * (a.ndim - 1) + [(0, LANES - a.shape[-1])])


def _one_hot_rows(n, offset):
    r = lax.broadcasted_iota(jnp.int32, (n, LANES), 0)
    c = lax.broadcasted_iota(jnp.int32, (n, LANES), 1)
    return (c == r + offset).astype(BF16)


def _prep_ssm(w_in, conv_w, conv_b, dt_bias, a_log, d_skip, norm_w, w_out):
    d_in = SSM_GROUPS * SSM_HPG * SSM_HEAD_DIM
    cdim = conv_w.shape[1]
    heads = SSM_GROUPS * SSM_HPG
    r = lax.broadcasted_iota(jnp.int32, (LANES, d_in), 0)
    c = lax.broadcasted_iota(jnp.int32, (LANES, d_in), 1)
    return dict(
        wz=w_in[:, :d_in].astype(BF16), wx=w_in[:, d_in:d_in + cdim].astype(BF16),
        wdt=_pad_lanes(w_in[:, d_in + cdim:]).astype(BF16),
        conv_w=conv_w, conv_b=conv_b.reshape(1, cdim),
        dt_bias=_pad_lanes(dt_bias.reshape(1, heads)), a_log=_pad_lanes(a_log.reshape(1, heads)),
        d_full=jnp.repeat(d_skip, SSM_HEAD_DIM).reshape(1, d_in),
        norm=norm_w.reshape(1, d_in), w_out=w_out.astype(BF16),
        eye=_one_hot_rows(heads, 0), expand=(c // SSM_HEAD_DIM == r).astype(BF16))


def _prep_mlstm(w_in, conv_w, conv_b, w_q, w_k, w_v, w_ig, b_ig, w_fg, b_fg, norm_w, skip, w_out):
    inner = conv_w.shape[1]
    hd = inner // ML_HEADS
    wg = jnp.concatenate([w_ig, w_fg], axis=-1).reshape(ML_HEADS, 3, hd, 2 * ML_HEADS)
    part = lambda j: _pad_lanes(wg[:, j].reshape(inner, 2 * ML_HEADS)).astype(BF16)
    return dict(
        wxm=w_in[:, :inner].astype(BF16), wo=w_in[:, inner:].astype(BF16),
        conv_w=conv_w, conv_b=conv_b.reshape(1, inner),
        w_q=w_q.astype(BF16), w_k=w_k.astype(BF16), w_kt=jnp.swapaxes(w_k, 1, 2).astype(BF16),
        w_v=w_v.astype(BF16),
        wgq=part(0), wgk=part(1), wgv=part(2),
        bg=_pad_lanes(jnp.concatenate([b_ig, b_fg]).reshape(1, 2 * ML_HEADS)),
        norm=norm_w.reshape(1, inner), skip=skip.reshape(1, inner), w_out=w_out.astype(BF16),
        eye_i=_one_hot_rows(ML_HEADS, 0), eye_f=_one_hot_rows(ML_HEADS, ML_HEADS))


def _trunk(x, mod, ssm_conv, ssm_state, ml_conv, ml_c, ml_n, ml_m, p):
    bsz, t, d = x.shape
    L = SCAN_CHUNK if t % SCAN_CHUNK == 0 else t
    R = 2 if bsz % 2 == 0 else 1
    tm = min(t, 512)
    n_heads = SSM_GROUPS * SSM_HPG
    h0 = ssm_state.reshape(bsz, SSM_GROUPS, SSM_HPG, SSM_HEAD_DIM, SSM_STATE)
    h0 = h0.transpose(0, 1, 4, 2, 3).reshape(bsz, SSM_GROUPS, SSM_STATE, SSM_HPG * SSM_HEAD_DIM)
    x, conv_s, h_s = _ssd_layer(x, mod[0], p['norm_mix'][0], ssm_conv, h0, p['ssm'], R, L)
    h_s = h_s.reshape(bsz, SSM_GROUPS, SSM_STATE, SSM_HPG, SSM_HEAD_DIM).transpose(0, 1, 3, 4, 2)
    h_s = h_s.reshape(bsz, n_heads, SSM_HEAD_DIM, SSM_STATE)
    x = _ffn_layer(x, mod[0], p['norm_ffn'][0], p['ffn'], tm)
    m0 = _pad_lanes(ml_m).reshape(bsz, 1, LANES)
    x, conv_m, c_m, n_m, m_m = _mlstm_layer(x, mod[1], p['norm_mix'][1], ml_conv, ml_c, ml_n, m0, p['ml'], R, L)
    y = _moe_layer(x, mod[1], p['norm_ffn'][1], p['moe'], p['norm_final'], tm)
    return (y, conv_s[None], h_s[None], conv_m[None], c_m[None], n_m[None], m_m[:, 0, :ML_HEADS][None])


def kernel(x_prompt, x_sample, c_prompt, c_sample, state_ssm_conv, state_ssm, state_mlstm_conv, state_mlstm_C, state_mlstm_n, state_mlstm_m, w_ada, b_ada, norm_mix, norm_ffn, norm_final, ssm_w_in, ssm_conv_w, ssm_conv_b, ssm_dt_bias, ssm_a_log, ssm_d, ssm_norm, ssm_w_out, ml_w_in, ml_conv_w, ml_conv_b, ml_w_q, ml_w_k, ml_w_v, ml_w_igate, ml_b_igate, ml_w_fgate, ml_b_fgate, ml_norm, ml_skip, ml_w_out, ffn_w_gate, ffn_w_up, ffn_w_down, moe_w_router, moe_b_router, moe_w_gate, moe_w_up, moe_w_down):
    depth, d, _ = w_ada.shape
    assert depth == 2 and state_ssm.shape[0] == 1 and state_mlstm_C.shape[0] == 1
    bp, bs = x_prompt.shape[0], x_sample.shape[0]
    p = dict(
        norm_mix=norm_mix.reshape(depth, 1, d), norm_ffn=norm_ffn.reshape(depth, 1, d),
        norm_final=norm_final.reshape(1, d),
        ssm=_prep_ssm(ssm_w_in[0], ssm_conv_w[0], ssm_conv_b[0], ssm_dt_bias[0], ssm_a_log[0], ssm_d[0],
                      ssm_norm[0], ssm_w_out[0]),
        ml=_prep_mlstm(ml_w_in[0], ml_conv_w[0], ml_conv_b[0], ml_w_q[0], ml_w_k[0], ml_w_v[0], ml_w_igate[0],
                       ml_b_igate[0], ml_w_fgate[0], ml_b_fgate[0], ml_norm[0], ml_skip[0], ml_w_out[0]),
        ffn=dict(w_gate=ffn_w_gate[0].astype(BF16), w_up=ffn_w_up[0].astype(BF16),
                 w_down=ffn_w_down[0].astype(BF16)),
        moe=dict(w_router=_pad_lanes(moe_w_router[0]), b_router=_pad_lanes(moe_b_router[0].reshape(1, -1)),
                 w_gate=_to_bf16(moe_w_gate[0]), w_up=_to_bf16(moe_w_up[0]), w_down=_to_bf16(moe_w_down[0])))
    mod = _ada(jnp.concatenate([c_prompt, c_sample], axis=0), w_ada, b_ada)
    mod = mod.reshape(depth, bp + bs, 6, d)

    f = F32
    zeros = lambda a, b: jnp.zeros((b,) + a.shape[2:], f)
    out_p = _trunk(x_prompt, mod[:, :bp], zeros(state_ssm_conv, bp), zeros(state_ssm, bp),
                   zeros(state_mlstm_conv, bp), zeros(state_mlstm_C, bp), zeros(state_mlstm_n, bp),
                   zeros(state_mlstm_m, bp), p)
    out_s = _trunk(x_sample, mod[:, bp:], state_ssm_conv[0], state_ssm[0], state_mlstm_conv[0],
                   state_mlstm_C[0], state_mlstm_n[0], state_mlstm_m[0], p)
    return (out_p[0], out_s[0]) + tuple(out_p[1:]) + tuple(out_s[1:])
```

```python
import functools

import jax
import jax.numpy as jnp
from jax import lax
from jax.experimental import pallas as pl
from jax.experimental.pallas import tpu as pltpu

F32 = jnp.float32
BF16 = jnp.bfloat16
RMS_EPS = 1e-6
CONV_W = 4
LANES = 128
SCAN_CHUNK = LANES
TAIL0 = 8 - (CONV_W - 1)
VMEM_LIMIT = 60 * 1024 * 1024

SSM_GROUPS = 4
SSM_HPG = 8
SSM_HEAD_DIM = 64
SSM_STATE = 128
ML_HEADS = 8
N_EXPERTS = 8


def _dot(a, b):
    return jnp.dot(a, b, preferred_element_type=F32)


def _dot_nt(a, b):
    return lax.dot_general(a, b, (((1,), (1,)), ((), ())), preferred_element_type=F32)


def _dot_tn(a, b):
    return lax.dot_general(a, b, (((0,), (0,)), ((), ())), preferred_element_type=F32)


def _split3(x):
    h1 = x.astype(BF16)
    r = x - h1.astype(F32)
    h2 = r.astype(BF16)
    r = r - h2.astype(F32)
    return h1, h2, r.astype(BF16)


def _dot3_l(sel, x):
    return sum(_dot(sel, p) for p in _split3(x))


def _dot3_r(x, sel):
    return sum(_dot(p, sel) for p in _split3(x))


def _transpose_rows(x, eye):
    return sum(_dot_nt(eye, p) for p in _split3(x))


def _sigmoid(x):
    return 1.0 / (1.0 + jnp.exp(-x))


def _silu(x):
    return x * _sigmoid(x)


def _softplus(x):
    return jnp.maximum(x, 0.0) + jnp.log1p(jnp.exp(-jnp.abs(x)))


def _rms(x, g):
    return x * lax.rsqrt(jnp.mean(x * x, axis=-1, keepdims=True) + RMS_EPS) * g


def _tri_mask(n):
    r = lax.broadcasted_iota(jnp.int32, (n, n), 0)
    c = lax.broadcasted_iota(jnp.int32, (n, n), 1)
    return r >= c


def _causal_conv_silu(buf, dst, tt, cw_ref, cb_ref):
    xn = buf[8:8 + tt, :]
    x1 = pltpu.roll(xn, 1, axis=0)
    w = [cw_ref[k:k + 1, :] for k in range(CONV_W)]
    y = cb_ref[...] + xn * w[3] + x1 * w[2] + pltpu.roll(xn * w[1] + x1 * w[0], 2, axis=0)
    dst[...] = _silu(y)
    head = cb_ref[...]
    for k in range(CONV_W):
        head = head + buf[TAIL0 + k:TAIL0 + k + 8, :] * cw_ref[k:k + 1, :]
    dst[0:8, :] = _silu(head)


def _ada_kernel(c_ref, w_ref, b_ref, o_ref):
    ca = _silu(c_ref[...]).astype(BF16)
    o_ref[0] = _dot(ca, w_ref[0].astype(BF16)) + b_ref[0]


def _ada(c_all, w_ada, b_ada):
    depth, d, n = w_ada.shape
    bt = c_all.shape[0]
    tn = n // 4
    return pl.pallas_call(
        _ada_kernel,
        out_shape=jax.ShapeDtypeStruct((depth, bt, n), F32),
        grid=(depth, n // tn),
        in_specs=[pl.BlockSpec((bt, d), lambda i, j: (0, 0)),
                  pl.BlockSpec((1, d, tn), lambda i, j: (i, 0, j)),
                  pl.BlockSpec((1, 1, tn), lambda i, j: (i, 0, j))],
        out_specs=pl.BlockSpec((1, bt, tn), lambda i, j: (i, 0, j)),
        compiler_params=pltpu.CompilerParams(
            dimension_semantics=("arbitrary", "arbitrary"), vmem_limit_bytes=VMEM_LIMIT),
        name="ada",
    )(c_all, w_ada, b_ada.reshape(depth, 1, n))


def _const_spec(shape):
    nd = len(shape)
    return pl.BlockSpec(shape, lambda *_: (0,) * nd, pipeline_mode=pl.Buffered(1))


def _ssd_kernel(x_ref, mod_ref, nw_ref, conv0_ref, h0_ref, wz_ref, wx_ref, wdt_ref, cw_ref, cb_ref,
                dtb_ref, alog_ref, dfull_ref, gnw_ref, wout_ref, eye_ref, expand_ref,
                xo_ref, convo_ref, ho_ref,
                xbc_buf, act_buf, z_buf, y_buf, xw_buf, *, R, L):
    t = pl.program_id(1)
    inner = SSM_HPG * SSM_HEAD_DIM
    d_in = SSM_GROUPS * inner
    gn = SSM_GROUPS * SSM_STATE

    @pl.when(t == 0)
    def _():
        for r in range(R):
            xbc_buf[r, TAIL0:8, :] = conv0_ref[r]
        ho_ref[...] = h0_ref[...]

    nw = nw_ref[...]
    hn = jnp.concatenate(
        [(_rms(x_ref[r], nw) * (1.0 + mod_ref[r][1:2]) + mod_ref[r][0:1]).astype(BF16) for r in range(R)], axis=0)
    z_buf[...] = _dot(hn, wz_ref[...])
    xbc = _dot(hn, wx_ref[...])
    dt = _softplus(_dot(hn, wdt_ref[...]) + dtb_ref[...])
    a = -jnp.exp(alog_ref[...])
    tri = _tri_mask(L)
    tri_b = jnp.where(tri, 1.0, 0.0).astype(BF16)
    eye = eye_ref[...]
    expand = expand_ref[...]
    left = lax.broadcasted_iota(jnp.int32, (1, LANES), 1) < SSM_HEAD_DIM

    for r in range(R):
        xbc_buf[r, 8:8 + L, :] = xbc[r * L:(r + 1) * L]
        _causal_conv_silu(xbc_buf.at[r], act_buf.at[r], L, cw_ref, cb_ref)
        tail = xbc_buf[r, L + TAIL0:L + 8, :]
        xbc_buf[r, TAIL0:8, :] = tail
        convo_ref[r] = tail

    for r in range(R):
        dtc = dt[r * L:(r + 1) * L]
        cum = _dot3_l(tri_b, dtc * a)
        cum_t = _transpose_rows(cum, eye)
        dt_t = _transpose_rows(dtc, eye)
        cum_last = cum[L - 1:L, :]
        wend = (jnp.exp(cum_last - cum) * dtc).astype(BF16)
        xw_buf[r] = (act_buf[r, :, :d_in] * _dot(wend, expand)).astype(BF16)
        dec_last = jnp.exp(_dot3_r(cum[L - 8:L, :], expand)[7:8, :])
        for g in range(SSM_GROUPS):
            bg = act_buf[r, :, d_in + g * SSM_STATE:d_in + (g + 1) * SSM_STATE].astype(BF16)
            cg = act_buf[r, :, d_in + gn + g * SSM_STATE:d_in + gn + (g + 1) * SSM_STATE].astype(BF16)
            cb = _dot_nt(cg, bg)
            hg = ho_ref[r, g]
            y_int = _dot(cg, hg.astype(BF16))
            for pr in range(SSM_HPG // 2):
                hd0 = g * SSM_HPG + 2 * pr
                lo = hd0 * SSM_HEAD_DIM
                xp = act_buf[r, :, lo:lo + LANES]
                x2 = jnp.concatenate([jnp.where(left, xp, 0.0), jnp.where(left, 0.0, xp)], axis=0).astype(BF16)
                wms, es = [], []
                for hd in (hd0, hd0 + 1):
                    ccol = jnp.broadcast_to(cum[:, hd:hd + 1], (L, LANES))
                    dec = jnp.exp(jnp.where(tri, ccol[:, :L] - cum_t[hd:hd + 1, :], -jnp.inf))
                    wms.append((cb * dec * dt_t[hd:hd + 1, :]).astype(BF16))
                    es.append(jnp.exp(ccol))
                yi = y_int[:, pr * LANES:(pr + 1) * LANES]
                y_buf[r, :, lo:lo + LANES] = (_dot(jnp.concatenate(wms, axis=1), x2)
                                              + yi * jnp.where(left, es[0], es[1]))
            ho_ref[r, g] = (hg * dec_last[:, g * inner:(g + 1) * inner]
                           + _dot_tn(bg, xw_buf[r, :, g * inner:(g + 1) * inner]))

    gnw = gnw_ref[...]
    rows = []
    for r in range(R):
        y = y_buf[r] + dfull_ref[...] * act_buf[r, :, :d_in]
        y = y * _silu(z_buf[r * L:(r + 1) * L, :])
        rows.append(jnp.concatenate(
            [_rms(y[:, g * inner:(g + 1) * inner], gnw[:, g * inner:(g + 1) * inner]).astype(BF16)
             for g in range(SSM_GROUPS)], axis=-1))
    out = _dot(jnp.concatenate(rows, axis=0), wout_ref[...])
    for r in range(R):
        xo_ref[r] = x_ref[r] + mod_ref[r][2:3] * out[r * L:(r + 1) * L]


def _ssd_layer(x, mod, nw, conv0, h0, p, R, L):
    bsz, t, d = x.shape
    d_in = SSM_GROUPS * SSM_HPG * SSM_HEAD_DIM
    cdim = p['conv_w'].shape[1]
    blk = lambda b, i: (b, 0, 0)
    st_blk = (R, SSM_GROUPS, SSM_STATE, SSM_HPG * SSM_HEAD_DIM)
    names = ['wz', 'wx', 'wdt', 'conv_w', 'conv_b', 'dt_bias', 'a_log', 'd_full', 'norm', 'w_out', 'eye', 'expand']
    in_specs = [
        pl.BlockSpec((R, L, d), lambda b, i: (b, i, 0)),
        pl.BlockSpec((R, 6, d), blk),
        _const_spec((1, d)),
        pl.BlockSpec((R, CONV_W - 1, cdim), blk),
        pl.BlockSpec(st_blk, lambda b, i: (b, 0, 0, 0)),
    ] + [_const_spec(p[n].shape) for n in names]
    out_shape = (jax.ShapeDtypeStruct((bsz, t, d), F32),
                 jax.ShapeDtypeStruct((bsz, CONV_W - 1, cdim), F32),
                 jax.ShapeDtypeStruct(h0.shape, F32))
    out_specs = (pl.BlockSpec((R, L, d), lambda b, i: (b, i, 0)),
                 pl.BlockSpec((R, CONV_W - 1, cdim), blk),
                 pl.BlockSpec(st_blk, lambda b, i: (b, 0, 0, 0)))
    scratch = [pltpu.VMEM((R, 8 + L, cdim), F32), pltpu.VMEM((R, L, cdim), F32), pltpu.VMEM((R * L, d_in), F32),
               pltpu.VMEM((R, L, d_in), F32), pltpu.VMEM((R, L, d_in), BF16)]
    return pl.pallas_call(
        functools.partial(_ssd_kernel, R=R, L=L),
        out_shape=out_shape, grid=(bsz // R, t // L), in_specs=in_specs, out_specs=out_specs,
        scratch_shapes=scratch,
        compiler_params=pltpu.CompilerParams(
            dimension_semantics=("arbitrary", "arbitrary"), vmem_limit_bytes=VMEM_LIMIT),
        name="ssd_mixer",
    )(x, mod, nw, conv0, h0, *[p[n] for n in names])


def _ffn_kernel(x_ref, mod_ref, nw_ref, wg_ref, wu_ref, wd_ref, o_ref, *, nchunk):
    x = x_ref[0]
    mod = mod_ref[0]
    hn = (_rms(x, nw_ref[...]) * (1.0 + mod[4:5]) + mod[3:4]).astype(BF16)
    fc = wg_ref.shape[1] // nchunk
    acc = None
    for j in range(nchunk):
        g = _dot(hn, wg_ref[:, j * fc:(j + 1) * fc])
        u = _dot(hn, wu_ref[:, j * fc:(j + 1) * fc])
        part = _dot((_silu(g) * u).astype(BF16), wd_ref[j * fc:(j + 1) * fc, :])
        acc = part if acc is None else acc + part
    o_ref[0] = x + mod[5:6] * acc


def _ffn_layer(x, mod, nw, p, tm):
    bsz, t, d = x.shape
    tm = min(t, 2 * tm)
    return pl.pallas_call(
        functools.partial(_ffn_kernel, nchunk=p['w_gate'].shape[1] // (2 * LANES)),
        out_shape=jax.ShapeDtypeStruct(x.shape, F32),
        grid=(bsz, t // tm),
        in_specs=[pl.BlockSpec((1, tm, d), lambda b, i: (b, i, 0)),
                  pl.BlockSpec((1, 6, d), lambda b, i: (b, 0, 0)),
                  _const_spec((1, d)),
                  _const_spec(p['w_gate'].shape), _const_spec(p['w_up'].shape), _const_spec(p['w_down'].shape)],
        out_specs=pl.BlockSpec((1, tm, d), lambda b, i: (b, i, 0)),
        compiler_params=pltpu.CompilerParams(
            dimension_semantics=("arbitrary", "arbitrary"), vmem_limit_bytes=VMEM_LIMIT),
        name="ffn",
    )(x, mod, nw, p['w_gate'], p['w_up'], p['w_down'])


def _mlstm_kernel(x_ref, mod_ref, nw_ref, conv0_ref, c0_ref, n0_ref, m0_ref,
                  wxm_ref, wo_ref, cw_ref, cb_ref, wq_ref, wk_ref, wkt_ref, wv_ref, wgq_ref, wgk_ref, wgv_ref, bg_ref,
                  gnw_ref, skip_ref, wout_ref, eyei_ref, eyef_ref,
                  xo_ref, convo_ref, co_ref, no_ref, mo_ref,
                  xm_buf, xc_buf, q_buf, k_buf, kt_buf, v_buf, op_buf, hh_buf,
                  *, R, L):
    t = pl.program_id(1)
    hd_dim = wq_ref.shape[1]
    rep = hd_dim // LANES
    k_scale = hd_dim ** -0.5

    @pl.when(t == 0)
    def _():
        for r in range(R):
            xm_buf[r, TAIL0:8, :] = conv0_ref[r]
        co_ref[...] = c0_ref[...]
        no_ref[...] = n0_ref[...]
        mo_ref[...] = m0_ref[...]

    nw = nw_ref[...]
    hn = jnp.concatenate(
        [(_rms(x_ref[r], nw) * (1.0 + mod_ref[r][1:2]) + mod_ref[r][0:1]).astype(BF16) for r in range(R)], axis=0)
    xm = _dot(hn, wxm_ref[...])
    for r in range(R):
        xm_buf[r, 8:8 + L, :] = xm[r * L:(r + 1) * L]
    op_buf[...] = _dot(hn, wo_ref[...])
    for r in range(R):
        _causal_conv_silu(xm_buf.at[r], xc_buf.at[r], L, cw_ref, cb_ref)
        tail = xm_buf[r, L + TAIL0:L + 8, :]
        xm_buf[r, TAIL0:8, :] = tail
        convo_ref[r] = tail

    gates = bg_ref[...]
    for h in range(ML_HEADS):
        sl = slice(h * hd_dim, (h + 1) * hd_dim)
        xc_h = jnp.concatenate([xc_buf[r, :, sl].astype(BF16) for r in range(R)], axis=0)
        xm_h = jnp.concatenate([xm_buf[r, 8:8 + L, sl].astype(BF16) for r in range(R)], axis=0)
        q = _dot(xc_h, wq_ref[h])
        k = _dot(xc_h, wk_ref[h])
        v = _dot(xm_h, wv_ref[h])
        gates = gates + _dot(q.astype(BF16), wgq_ref[sl, :]) + _dot(k.astype(BF16), wgk_ref[sl, :]) \
            + _dot(v.astype(BF16), wgv_ref[sl, :])
        q_buf[:, sl] = q.astype(BF16)
        k_buf[:, sl] = (k * k_scale).astype(BF16)
        v_buf[:, sl] = v.astype(BF16)
        for r in range(R):
            kt_buf[r, h] = _dot_nt(wkt_ref[h], xc_h[r * L:(r + 1) * L]) * k_scale
    lf = jnp.minimum(gates, 0.0) - jnp.log1p(jnp.exp(-jnp.abs(gates)))
    tri = _tri_mask(L)
    tri_b = jnp.where(tri, 1.0, 0.0).astype(BF16)
    lane = lax.broadcasted_iota(jnp.int32, (1, LANES), 1)
    wide = lambda a: jnp.concatenate([a] * rep, axis=1)

    for r in range(R):
        rs = slice(r * L, (r + 1) * L)
        gi = gates[rs]
        bcum = _dot3_l(tri_b, lf[rs])
        li_t = _transpose_rows(gi, eyei_ref[...])
        b_t = _transpose_rows(bcum, eyef_ref[...])
        m_prev = mo_ref[r]
        m_next = m_prev
        for h in range(ML_HEADS):
            sl = slice(h * hd_dim, (h + 1) * hd_dim)
            bcol = jnp.broadcast_to(bcum[:, ML_HEADS + h:ML_HEADS + h + 1], (L, LANES))
            mp = jnp.broadcast_to(m_prev[:, h:h + 1], (1, LANES))
            logw_t = li_t[h:h + 1, :] - b_t[h:h + 1, :]
            dmat = jnp.where(tri, bcol[:, :L] + logw_t, -jnp.inf)
            a_inter = bcol + mp
            m_t = jnp.maximum(a_inter, jnp.max(dmat, axis=1, keepdims=True))
            qb = q_buf[rs, sl]
            vb = v_buf[rs, sl]
            s_mat = jnp.exp(dmat - m_t[:, :L]) * _dot_nt(qb, k_buf[rs, sl])
            w_inter = jnp.exp(a_inter - m_t)
            ch = co_ref[r, h]
            nh = no_ref[r, h:h + 1, :]
            num = _dot(s_mat.astype(BF16), vb) + wide(w_inter) * _dot(qb, ch.astype(BF16))
            qn = jnp.sum(qb.astype(F32) * nh, axis=1, keepdims=True)
            den = jnp.sum(s_mat, axis=1, keepdims=True) + w_inter * qn
            inv = 1.0 / jnp.maximum(jnp.abs(den), jnp.exp(-m_t))
            hh_buf[r, :, sl] = num * wide(inv)
            m_new = m_t[L - 1:L, :]
            b_last = bcol[L - 1:L, :]
            w_end = jnp.exp(b_last[:, :L] + logw_t - m_new[:, :L])
            w_old = wide(jnp.exp(b_last + mp - m_new))
            kw_t = (kt_buf[r, h] * w_end).astype(BF16)
            co_ref[r, h] = w_old * ch + _dot(kw_t, vb)
            w8 = jnp.broadcast_to(w_end, (8, L)).astype(BF16)
            no_ref[r, h:h + 1, :] = w_old * nh + _dot(w8, k_buf[rs, sl])[0:1, :]
            m_next = jnp.where(lane == h, m_new, m_next)
        mo_ref[r] = m_next

    gnw = gnw_ref[...]
    skip = skip_ref[...]
    rows = []
    for r in range(R):
        parts = []
        for h in range(ML_HEADS):
            sl = slice(h * hd_dim, (h + 1) * hd_dim)
            hn_h = _rms(hh_buf[r, :, sl], gnw[:, sl])
            parts.append(((hn_h + skip[:, sl] * xc_buf[r, :, sl])
                          * _sigmoid(op_buf[r * L:(r + 1) * L, sl])).astype(BF16))
        rows.append(jnp.concatenate(parts, axis=-1))
    out = _dot(jnp.concatenate(rows, axis=0), wout_ref[...])
    for r in range(R):
        xo_ref[r] = x_ref[r] + mod_ref[r][2:3] * out[r * L:(r + 1) * L]


def _mlstm_layer(x, mod, nw, conv0, c0, n0, m0, p, R, L):
    bsz, t, d = x.shape
    inner = p['conv_w'].shape[1]
    hd = inner // ML_HEADS
    blk = lambda b, i: (b, 0, 0)
    blk4 = lambda b, i: (b, 0, 0, 0)
    names = ['wxm', 'wo', 'conv_w', 'conv_b', 'w_q', 'w_k', 'w_kt', 'w_v', 'wgq', 'wgk', 'wgv', 'bg',
             'norm', 'skip', 'w_out', 'eye_i', 'eye_f']
    in_specs = [
        pl.BlockSpec((R, L, d), lambda b, i: (b, i, 0)),
        pl.BlockSpec((R, 6, d), blk),
        _const_spec((1, d)),
        pl.BlockSpec((R, CONV_W - 1, inner), blk),
        pl.BlockSpec((R, ML_HEADS, hd, hd), blk4, pipeline_mode=pl.Buffered(1)),
        pl.BlockSpec((R, ML_HEADS, hd), blk),
        pl.BlockSpec((R, 1, LANES), blk),
    ] + [_const_spec(p[n].shape) for n in names]
    out_shape = (jax.ShapeDtypeStruct((bsz, t, d), F32),
                 jax.ShapeDtypeStruct((bsz, CONV_W - 1, inner), F32),
                 jax.ShapeDtypeStruct((bsz, ML_HEADS, hd, hd), F32),
                 jax.ShapeDtypeStruct((bsz, ML_HEADS, hd), F32),
                 jax.ShapeDtypeStruct((bsz, 1, LANES), F32))
    out_specs = (pl.BlockSpec((R, L, d), lambda b, i: (b, i, 0)),
                 pl.BlockSpec((R, CONV_W - 1, inner), blk),
                 pl.BlockSpec((R, ML_HEADS, hd, hd), blk4),
                 pl.BlockSpec((R, ML_HEADS, hd), blk),
                 pl.BlockSpec((R, 1, LANES), blk))
    scratch = [pltpu.VMEM((R, 8 + L, inner), F32), pltpu.VMEM((R, L, inner), F32),
               pltpu.VMEM((R * L, inner), BF16), pltpu.VMEM((R * L, inner), BF16),
               pltpu.VMEM((R, ML_HEADS, hd, L), F32), pltpu.VMEM((R * L, inner), BF16),
               pltpu.VMEM((R * L, inner), F32), pltpu.VMEM((R, L, inner), F32)]
    return pl.pallas_call(
        functools.partial(_mlstm_kernel, R=R, L=L),
        out_shape=out_shape, grid=(bsz // R, t // L), in_specs=in_specs, out_specs=out_specs,
        scratch_shapes=scratch,
        compiler_params=pltpu.CompilerParams(
            dimension_semantics=("arbitrary", "arbitrary"), vmem_limit_bytes=VMEM_LIMIT),
        name="mlstm_mixer",
    )(x, mod, nw, conv0, c0, n0, m0, *[p[n] for n in names])


def _router_kernel(x_ref, mod_ref, nw_ref, wr_ref, br_ref, h_ref, info_ref, cnt_ref, *, tm):
    x = x_ref[0]
    mod = mod_ref[0]
    h = _rms(x, nw_ref[...]) * (1.0 + mod[4:5]) + mod[3:4]
    h_ref[0] = h
    h1, h2, h3 = _split3(h)
    w1, w2, w3 = _split3(wr_ref[...])
    logits = (_dot(h1, w1) + (_dot(h1, w2) + _dot(h2, w1))
              + (_dot(h1, w3) + _dot(h2, w2) + _dot(h3, w1))) + br_ref[...]
    lane = lax.broadcasted_iota(jnp.int32, (tm, LANES), 1)
    lg = jnp.where(lane < N_EXPERTS, logits, -jnp.inf)
    m1 = jnp.max(lg, axis=1, keepdims=True)
    i1 = jnp.min(jnp.where(lg == m1, lane, LANES), axis=1, keepdims=True)
    lg2 = jnp.where(lane == i1, -jnp.inf, lg)
    m2 = jnp.max(lg2, axis=1, keepdims=True)
    i2 = jnp.min(jnp.where(lg2 == m2, lane, LANES), axis=1, keepdims=True)
    e2 = jnp.exp(m2 - m1)
    w_top1 = 1.0 / (1.0 + e2)
    w_top2 = e2 / (1.0 + e2)
    sel = jnp.logical_or(lane == i1, lane == i2)
    mask = jnp.where(sel, 1.0, 0.0).astype(BF16)
    r = lax.broadcasted_iota(jnp.int32, (tm, tm), 0)
    c = lax.broadcasted_iota(jnp.int32, (tm, tm), 1)
    before = jnp.where(c < r, 1.0, 0.0).astype(BF16)
    rank = _dot(before, mask)
    r1 = jnp.sum(jnp.where(lane == i1, rank, 0.0), axis=1, keepdims=True)
    r2 = jnp.sum(jnp.where(lane == i2, rank, 0.0), axis=1, keepdims=True)
    cols = (i1.astype(F32), i2.astype(F32), r1, r2, w_top1, w_top2)
    info = jnp.zeros((tm, LANES), F32)
    for k, col in enumerate(cols):
        info = jnp.where(lane == k, col, info)
    info_ref[0] = info
    cnt = jnp.sum(mask.astype(F32), axis=0, keepdims=True)
    cnt_ref[0, 0] = jnp.broadcast_to(cnt, (8, LANES)).astype(jnp.int32)


def _router(x, mod, nw, p, tm):
    bsz, t, d = x.shape
    nt = t // tm
    ti = lambda b, i: (b, i, 0)
    t4 = lambda b, i: (b, i, 0, 0)
    out_shape = (jax.ShapeDtypeStruct((bsz, t, d), F32),
                 jax.ShapeDtypeStruct((bsz, t, LANES), F32),
                 jax.ShapeDtypeStruct((bsz, nt, 8, LANES), jnp.int32))
    out_specs = (pl.BlockSpec((1, tm, d), ti), pl.BlockSpec((1, tm, LANES), ti),
                 pl.BlockSpec((1, 1, 8, LANES), t4))
    return pl.pallas_call(
        functools.partial(_router_kernel, tm=tm),
        out_shape=out_shape, grid=(bsz, nt),
        in_specs=[pl.BlockSpec((1, tm, d), ti), pl.BlockSpec((1, 6, d), lambda b, i: (b, 0, 0)),
                  _const_spec((1, d)), _const_spec(p['w_router'].shape), _const_spec(p['b_router'].shape)],
        out_specs=out_specs,
        compiler_params=pltpu.CompilerParams(
            dimension_semantics=("arbitrary", "arbitrary"), vmem_limit_bytes=VMEM_LIMIT),
        name="router",
    )(x, mod, nw, p['w_router'], p['b_router'])


SLAB = 512
DMA_UNROLL = 8


def _dispatch_kernel(ids_ref, pos_ref, h_ref, xs_ref, zbuf, sem, *, tm):
    @pl.when(jnp.logical_and(pl.program_id(0) == 0, pl.program_id(1) == 0))
    def _():
        zbuf[...] = jnp.zeros(zbuf.shape, F32)
        for i in range(2 * N_EXPERTS):
            c = pltpu.make_async_copy(zbuf, xs_ref.at[pl.ds(ids_ref[i] * SLAB, SLAB), :], sem)
            c.start()
            c.wait()

    def send(t, carry):
        for k in range(2):
            pltpu.make_async_copy(h_ref.at[0, pl.ds(t, 1), :],
                                  xs_ref.at[pl.ds(pos_ref[0, 0, k * tm + t], 1), :], sem).start(priority=k)
        return carry

    lax.fori_loop(0, tm, send, 0, unroll=DMA_UNROLL)
    for k in range(2):
        pltpu.make_async_copy(h_ref.at[0], xs_ref.at[pl.ds(0, tm), :], sem).wait()


def _dispatch(h, pos, slab_ids, n_rows, tm):
    bsz, t, d = h.shape
    nt = t // tm
    return pl.pallas_call(
        functools.partial(_dispatch_kernel, tm=tm),
        out_shape=jax.ShapeDtypeStruct((n_rows, d), F32),
        grid_spec=pltpu.PrefetchScalarGridSpec(
            num_scalar_prefetch=1, grid=(bsz, nt),
            in_specs=[pl.BlockSpec((1, 1, 2 * tm), lambda b, i, ids: (b * nt + i, 0, 0), memory_space=pltpu.SMEM),
                      pl.BlockSpec((1, tm, d), lambda b, i, ids: (b, i, 0))],
            out_specs=pl.BlockSpec(memory_space=pl.ANY),
            scratch_shapes=[pltpu.VMEM((SLAB, d), F32), pltpu.SemaphoreType.DMA(())]),
        compiler_params=pltpu.CompilerParams(
            dimension_semantics=("arbitrary", "arbitrary"), vmem_limit_bytes=VMEM_LIMIT),
        name="dispatch",
    )(slab_ids, pos, h)


def _slab_ffn_kernel(se_ref, nu_ref, x_ref, wg_ref, wu_ref, wd_ref, y_ref, *, nchunk):
    s = pl.program_id(0)

    @pl.when(s < nu_ref[0])
    def _():
        xb = x_ref[...].astype(BF16)
        fc = wg_ref.shape[2] // nchunk
        acc = None
        for j in range(nchunk):
            g = _dot(xb, wg_ref[0, :, j * fc:(j + 1) * fc])
            u = _dot(xb, wu_ref[0, :, j * fc:(j + 1) * fc])
            part = _dot((_silu(g) * u).astype(BF16), wd_ref[0, j * fc:(j + 1) * fc, :])
            acc = part if acc is None else acc + part
        y_ref[...] = acc

    @pl.when(s >= nu_ref[0])
    def _():
        y_ref[...] = jnp.zeros(y_ref.shape, F32)


def _slab_ffn(xs, slab_expert, n_used, wg, wu, wd):
    n_rows, d = xs.shape
    f = wg.shape[2]
    w_idx = lambda s, se, nu: (se[s], 0, 0)
    grid_spec = pltpu.PrefetchScalarGridSpec(
        num_scalar_prefetch=2, grid=(n_rows // SLAB,),
        in_specs=[pl.BlockSpec((SLAB, d), lambda s, se, nu: (s, 0)),
                  pl.BlockSpec((1, d, f), w_idx), pl.BlockSpec((1, d, f), w_idx), pl.BlockSpec((1, f, d), w_idx)],
        out_specs=pl.BlockSpec((SLAB, d), lambda s, se, nu: (s, 0)))
    return pl.pallas_call(
        functools.partial(_slab_ffn_kernel, nchunk=f // (2 * LANES)),
        out_shape=jax.ShapeDtypeStruct((n_rows, d), F32),
        grid_spec=grid_spec,
        compiler_params=pltpu.CompilerParams(dimension_semantics=("arbitrary",), vmem_limit_bytes=VMEM_LIMIT),
        name="slab_ffn",
    )(slab_expert, n_used, xs, wg, wu, wd)


def _combine_kernel(pos_ref, x_ref, info_ref, mod_ref, fw_ref, ys_ref, o_ref, buf, sem, *, tm):
    def fetch(t, carry):
        for k in range(2):
            pltpu.make_async_copy(ys_ref.at[pl.ds(pos_ref[0, 0, k * tm + t], 1), :],
                                  buf.at[k, pl.ds(t, 1), :], sem).start(priority=k)
        return carry

    lax.fori_loop(0, tm, fetch, 0, unroll=DMA_UNROLL)
    for k in range(2):
        pltpu.make_async_copy(ys_ref.at[pl.ds(0, tm), :], buf.at[k], sem).wait()
    info = info_ref[0]
    y = x_ref[0] + mod_ref[0][5:6] * (info[:, 4:5] * buf[0] + info[:, 5:6] * buf[1])
    o_ref[0] = _rms(y, fw_ref[...])


def _combine(x, info, mod, fw, ys, pos, tm):
    bsz, t, d = x.shape
    nt = t // tm
    ti = lambda b, i: (b, i, 0)
    return pl.pallas_call(
        functools.partial(_combine_kernel, tm=tm),
        out_shape=jax.ShapeDtypeStruct(x.shape, F32),
        grid=(bsz, nt),
        in_specs=[pl.BlockSpec((1, 1, 2 * tm), lambda b, i: (b * nt + i, 0, 0), memory_space=pltpu.SMEM),
                  pl.BlockSpec((1, tm, d), ti), pl.BlockSpec((1, tm, LANES), ti),
                  pl.BlockSpec((1, 6, d), lambda b, i: (b, 0, 0)), _const_spec(fw.shape),
                  pl.BlockSpec(memory_space=pl.ANY)],
        out_specs=pl.BlockSpec((1, tm, d), ti),
        scratch_shapes=[pltpu.VMEM((2, tm, d), F32), pltpu.SemaphoreType.DMA(())],
        compiler_params=pltpu.CompilerParams(
            dimension_semantics=("arbitrary", "arbitrary"), vmem_limit_bytes=VMEM_LIMIT),
        name="combine",
    )(pos, x, info, mod, fw, ys)


def _moe_layer(x, mod, nw, p, fw, tm):
    bsz, t, _ = x.shape
    n_tok = bsz * t
    tiles = n_tok // tm
    h, info, cnt = _router(x, mod, nw, p, tm)
    counts = cnt[:, :, 0, :N_EXPERTS].reshape(tiles, N_EXPERTS)
    base = jnp.cumsum(counts, axis=0) - counts
    slabs = (jnp.sum(counts, axis=0) + (SLAB - 1)) // SLAB
    slab_end = jnp.cumsum(slabs)
    start = (slab_end - slabs) * SLAB
    n_slabs = (2 * n_tok) // SLAB + N_EXPERTS
    experts = jnp.arange(N_EXPERTS, dtype=jnp.int32)
    slab_expert = jnp.minimum(jnp.sum(jnp.arange(n_slabs, dtype=jnp.int32)[:, None] >= slab_end[None, :], axis=1),
                              N_EXPERTS - 1).astype(jnp.int32)
    inf2 = info.reshape(tiles, tm, LANES)
    offs = (start[None, :] + base)[:, None, :]
    pos = []
    for k in range(2):
        e_k = inf2[:, :, k].astype(jnp.int32)
        off_k = jnp.sum(jnp.where(e_k[:, :, None] == experts[None, None, :], offs, 0), axis=-1)
        pos.append(off_k + inf2[:, :, 2 + k].astype(jnp.int32))
    pos = jnp.concatenate(pos, axis=1).reshape(tiles, 1, 2 * tm).astype(jnp.int32)
    partial = jnp.concatenate([jnp.maximum(slab_end - 1, 0), jnp.minimum(slab_end[-1] + experts, n_slabs - 1)])
    xs = _dispatch(h, pos, partial.astype(jnp.int32), n_slabs * SLAB, tm)
    ys = _slab_ffn(xs, slab_expert, slab_end[-1:].astype(jnp.int32), p['w_gate'], p['w_up'], p['w_down'])
    return _combine(x, info, mod, fw, ys, pos, tm)


def _cast_kernel(x_ref, o_ref):
    o_ref[...] = x_ref[...].astype(BF16)


def _to_bf16(a, tr=512):
    a2 = a.reshape(-1, a.shape[-1])
    rows, cols = a2.shape
    assert rows % tr == 0
    out = pl.pallas_call(
        _cast_kernel,
        out_shape=jax.ShapeDtypeStruct(a2.shape, BF16),
        grid=(rows // tr,),
        in_specs=[pl.BlockSpec((tr, cols), lambda i: (i, 0))],
        out_specs=pl.BlockSpec((tr, cols), lambda i: (i, 0)),
        compiler_params=pltpu.CompilerParams(
            dimension_semantics=("arbitrary",), vmem_limit_bytes=VMEM_LIMIT),
        name="to_bf16",
    )(a2)
    return out.reshape(a.shape)


def _pad_lanes(a):
    return jnp.pad(a, [(0, 0)] * (a.ndim - 1) + [(0, LANES - a.shape[-1])])


def _one_hot_rows(n, offset):
    r = lax.broadcasted_iota(jnp.int32, (n, LANES), 0)
    c = lax.broadcasted_iota(jnp.int32, (n, LANES), 1)
    return (c == r + offset).astype(BF16)


def _prep_ssm(w_in, conv_w, conv_b, dt_bias, a_log, d_skip, norm_w, w_out):
    d_in = SSM_GROUPS * SSM_HPG * SSM_HEAD_DIM
    cdim = conv_w.shape[1]
    heads = SSM_GROUPS * SSM_HPG
    r = lax.broadcasted_iota(jnp.int32, (LANES, d_in), 0)
    c = lax.broadcasted_iota(jnp.int32, (LANES, d_in), 1)
    return dict(
        wz=w_in[:, :d_in].astype(BF16), wx=w_in[:, d_in:d_in + cdim].astype(BF16),
        wdt=_pad_lanes(w_in[:, d_in + cdim:]).astype(BF16),
        conv_w=conv_w, conv_b=conv_b.reshape(1, cdim),
        dt_bias=_pad_lanes(dt_bias.reshape(1, heads)), a_log=_pad_lanes(a_log.reshape(1, heads)),
        d_full=jnp.repeat(d_skip, SSM_HEAD_DIM).reshape(1, d_in),
        norm=norm_w.reshape(1, d_in), w_out=w_out.astype(BF16),
        eye=_one_hot_rows(heads, 0), expand=(c // SSM_HEAD_DIM == r).astype(BF16))


def _prep_mlstm(w_in, conv_w, conv_b, w_q, w_k, w_v, w_ig, b_ig, w_fg, b_fg, norm_w, skip, w_out):
    inner = conv_w.shape[1]
    hd = inner // ML_HEADS
    wg = jnp.concatenate([w_ig, w_fg], axis=-1).reshape(ML_HEADS, 3, hd, 2 * ML_HEADS)
    part = lambda j: _pad_lanes(wg[:, j].reshape(inner, 2 * ML_HEADS)).astype(BF16)
    return dict(
        wxm=w_in[:, :inner].astype(BF16), wo=w_in[:, inner:].astype(BF16),
        conv_w=conv_w, conv_b=conv_b.reshape(1, inner),
        w_q=w_q.astype(BF16), w_k=w_k.astype(BF16), w_kt=jnp.swapaxes(w_k, 1, 2).astype(BF16),
        w_v=w_v.astype(BF16),
        wgq=part(0), wgk=part(1), wgv=part(2),
        bg=_pad_lanes(jnp.concatenate([b_ig, b_fg]).reshape(1, 2 * ML_HEADS)),
        norm=norm_w.reshape(1, inner), skip=skip.reshape(1, inner), w_out=w_out.astype(BF16),
        eye_i=_one_hot_rows(ML_HEADS, 0), eye_f=_one_hot_rows(ML_HEADS, ML_HEADS))


def _trunk(x, mod, ssm_conv, ssm_state, ml_conv, ml_c, ml_n, ml_m, p):
    bsz, t, d = x.shape
    L = SCAN_CHUNK if t % SCAN_CHUNK == 0 else t
    R = 2 if bsz % 2 == 0 else 1
    tm = min(t, 512)
    n_heads = SSM_GROUPS * SSM_HPG
    h0 = ssm_state.reshape(bsz, SSM_GROUPS, SSM_HPG, SSM_HEAD_DIM, SSM_STATE)
    h0 = h0.transpose(0, 1, 4, 2, 3).reshape(bsz, SSM_GROUPS, SSM_STATE, SSM_HPG * SSM_HEAD_DIM)
    x, conv_s, h_s = _ssd_layer(x, mod[0], p['norm_mix'][0], ssm_conv, h0, p['ssm'], R, L)
    h_s = h_s.reshape(bsz, SSM_GROUPS, SSM_STATE, SSM_HPG, SSM_HEAD_DIM).transpose(0, 1, 3, 4, 2)
    h_s = h_s.reshape(bsz, n_heads, SSM_HEAD_DIM, SSM_STATE)
    x = _ffn_layer(x, mod[0], p['norm_ffn'][0], p['ffn'], tm)
    m0 = _pad_lanes(ml_m).reshape(bsz, 1, LANES)
    x, conv_m, c_m, n_m, m_m = _mlstm_layer(x, mod[1], p['norm_mix'][1], ml_conv, ml_c, ml_n, m0, p['ml'], R, L)
    y = _moe_layer(x, mod[1], p['norm_ffn'][1], p['moe'], p['norm_final'], tm)
    return (y, conv_s[None], h_s[None], conv_m[None], c_m[None], n_m[None], m_m[:, 0, :ML_HEADS][None])


def kernel(x_prompt, x_sample, c_prompt, c_sample, state_ssm_conv, state_ssm, state_mlstm_conv, state_mlstm_C, state_mlstm_n, state_mlstm_m, w_ada, b_ada, norm_mix, norm_ffn, norm_final, ssm_w_in, ssm_conv_w, ssm_conv_b, ssm_dt_bias, ssm_a_log, ssm_d, ssm_norm, ssm_w_out, ml_w_in, ml_conv_w, ml_conv_b, ml_w_q, ml_w_k, ml_w_v, ml_w_igate, ml_b_igate, ml_w_fgate, ml_b_fgate, ml_norm, ml_skip, ml_w_out, ffn_w_gate, ffn_w_up, ffn_w_down, moe_w_router, moe_b_router, moe_w_gate, moe_w_up, moe_w_down):
    depth, d, _ = w_ada.shape
    assert depth == 2 and state_ssm.shape[0] == 1 and state_mlstm_C.shape[0] == 1
    bp, bs = x_prompt.shape[0], x_sample.shape[0]
    p = dict(
        norm_mix=norm_mix.reshape(depth, 1, d), norm_ffn=norm_ffn.reshape(depth, 1, d),
        norm_final=norm_final.reshape(1, d),
        ssm=_prep_ssm(ssm_w_in[0], ssm_conv_w[0], ssm_conv_b[0], ssm_dt_bias[0], ssm_a_log[0], ssm_d[0],
                      ssm_norm[0], ssm_w_out[0]),
        ml=_prep_mlstm(ml_w_in[0], ml_conv_w[0], ml_conv_b[0], ml_w_q[0], ml_w_k[0], ml_w_v[0], ml_w_igate[0],
                       ml_b_igate[0], ml_w_fgate[0], ml_b_fgate[0], ml_norm[0], ml_skip[0], ml_w_out[0]),
        ffn=dict(w_gate=ffn_w_gate[0].astype(BF16), w_up=ffn_w_up[0].astype(BF16),
                 w_down=ffn_w_down[0].astype(BF16)),
        moe=dict(w_router=_pad_lanes(moe_w_router[0]), b_router=_pad_lanes(moe_b_router[0].reshape(1, -1)),
                 w_gate=_to_bf16(moe_w_gate[0]), w_up=_to_bf16(moe_w_up[0]), w_down=_to_bf16(moe_w_down[0])))
    mod = _ada(jnp.concatenate([c_prompt, c_sample], axis=0), w_ada, b_ada)
    mod = mod.reshape(depth, bp + bs, 6, d)

    f = F32
    zeros = lambda a, b: jnp.zeros((b,) + a.shape[2:], f)
    out_p = _trunk(x_prompt, mod[:, :bp], zeros(state_ssm_conv, bp), zeros(state_ssm, bp),
                   zeros(state_mlstm_conv, bp), zeros(state_mlstm_C, bp), zeros(state_mlstm_n, bp),
                   zeros(state_mlstm_m, bp), p)
    out_s = _trunk(x_sample, mod[:, bp:], state_ssm_conv[0], state_ssm[0], state_mlstm_conv[0],
                   state_mlstm_C[0], state_mlstm_n[0], state_mlstm_m[0], p)
    return (out_p[0], out_s[0]) + tuple(out_p[1:]) + tuple(out_s[1:])
```

```python
import functools

import jax
import jax.numpy as jnp
from jax import lax
from jax.experimental import pallas as pl
from jax.experimental.pallas import tpu as pltpu

F32 = jnp.float32
BF16 = jnp.bfloat16
RMS_EPS = 1e-6
CONV_W = 4
LANES = 128
SCAN_CHUNK = LANES
TAIL0 = 8 - (CONV_W - 1)
VMEM_LIMIT = 60 * 1024 * 1024

SSM_GROUPS = 4
SSM_HPG = 8
SSM_HEAD_DIM = 64
SSM_STATE = 128
ML_HEADS = 8
N_EXPERTS = 8


def _dot(a, b):
    return jnp.dot(a, b, preferred_element_type=F32)


def _dot_nt(a, b):
    return lax.dot_general(a, b, (((1,), (1,)), ((), ())), preferred_element_type=F32)


def _dot_tn(a, b):
    return lax.dot_general(a, b, (((0,), (0,)), ((), ())), preferred_element_type=F32)


def _split3(x):
    h1 = x.astype(BF16)
    r = x - h1.astype(F32)
    h2 = r.astype(BF16)
    r = r - h2.astype(F32)
    return h1, h2, r.astype(BF16)


def _dot3_l(sel, x):
    return sum(_dot(sel, p) for p in _split3(x))


def _dot3_r(x, sel):
    return sum(_dot(p, sel) for p in _split3(x))


def _transpose_rows(x, eye):
    return sum(_dot_nt(eye, p) for p in _split3(x))


def _sigmoid(x):
    return 1.0 / (1.0 + jnp.exp(-x))


def _silu(x):
    return x * _sigmoid(x)


def _softplus(x):
    return jnp.maximum(x, 0.0) + jnp.log1p(jnp.exp(-jnp.abs(x)))


def _rms(x, g):
    return x * lax.rsqrt(jnp.mean(x * x, axis=-1, keepdims=True) + RMS_EPS) * g


def _tri_mask(n):
    r = lax.broadcasted_iota(jnp.int32, (n, n), 0)
    c = lax.broadcasted_iota(jnp.int32, (n, n), 1)
    return r >= c


def _causal_conv_silu(buf, dst, tt, cw_ref, cb_ref):
    xn = buf[8:8 + tt, :]
    x1 = pltpu.roll(xn, 1, axis=0)
    w = [cw_ref[k:k + 1, :] for k in range(CONV_W)]
    y = cb_ref[...] + xn * w[3] + x1 * w[2] + pltpu.roll(xn * w[1] + x1 * w[0], 2, axis=0)
    dst[...] = _silu(y)
    head = cb_ref[...]
    for k in range(CONV_W):
        head = head + buf[TAIL0 + k:TAIL0 + k + 8, :] * cw_ref[k:k + 1, :]
    dst[0:8, :] = _silu(head)


def _ada_kernel(c_ref, w_ref, b_ref, o_ref):
    ca = _silu(c_ref[...]).astype(BF16)
    o_ref[0] = _dot(ca, w_ref[0].astype(BF16)) + b_ref[0]


def _ada(c_all, w_ada, b_ada):
    depth, d, n = w_ada.shape
    bt = c_all.shape[0]
    tn = n // 4
    return pl.pallas_call(
        _ada_kernel,
        out_shape=jax.ShapeDtypeStruct((depth, bt, n), F32),
        grid=(depth, n // tn),
        in_specs=[pl.BlockSpec((bt, d), lambda i, j: (0, 0)),
                  pl.BlockSpec((1, d, tn), lambda i, j: (i, 0, j)),
                  pl.BlockSpec((1, 1, tn), lambda i, j: (i, 0, j))],
        out_specs=pl.BlockSpec((1, bt, tn), lambda i, j: (i, 0, j)),
        compiler_params=pltpu.CompilerParams(
            dimension_semantics=("arbitrary", "arbitrary"), vmem_limit_bytes=VMEM_LIMIT),
        name="ada",
    )(c_all, w_ada, b_ada.reshape(depth, 1, n))


def _const_spec(shape):
    nd = len(shape)
    return pl.BlockSpec(shape, lambda *_: (0,) * nd, pipeline_mode=pl.Buffered(1))


def _ssd_kernel(x_ref, mod_ref, nw_ref, conv0_ref, h0_ref, wz_ref, wx_ref, wdt_ref, cw_ref, cb_ref,
                dtb_ref, alog_ref, dfull_ref, gnw_ref, wout_ref, eye_ref, expand_ref,
                xo_ref, convo_ref, ho_ref,
                xbc_buf, act_buf, z_buf, y_buf, xw_buf, *, R, L):
    t = pl.program_id(1)
    inner = SSM_HPG * SSM_HEAD_DIM
    d_in = SSM_GROUPS * inner
    gn = SSM_GROUPS * SSM_STATE

    @pl.when(t == 0)
    def _():
        for r in range(R):
            xbc_buf[r, TAIL0:8, :] = conv0_ref[r]
        ho_ref[...] = h0_ref[...]

    nw = nw_ref[...]
    hn = jnp.concatenate(
        [(_rms(x_ref[r], nw) * (1.0 + mod_ref[r][1:2]) + mod_ref[r][0:1]).astype(BF16) for r in range(R)], axis=0)
    z_buf[...] = _dot(hn, wz_ref[...])
    xbc = _dot(hn, wx_ref[...])
    dt = _softplus(_dot(hn, wdt_ref[...]) + dtb_ref[...])
    a = -jnp.exp(alog_ref[...])
    tri = _tri_mask(L)
    tri_b = jnp.where(tri, 1.0, 0.0).astype(BF16)
    eye = eye_ref[...]
    expand = expand_ref[...]
    left = lax.broadcasted_iota(jnp.int32, (1, LANES), 1) < SSM_HEAD_DIM

    for r in range(R):
        xbc_buf[r, 8:8 + L, :] = xbc[r * L:(r + 1) * L]
        _causal_conv_silu(xbc_buf.at[r], act_buf.at[r], L, cw_ref, cb_ref)
        tail = xbc_buf[r, L + TAIL0:L + 8, :]
        xbc_buf[r, TAIL0:8, :] = tail
        convo_ref[r] = tail

    for r in range(R):
        dtc = dt[r * L:(r + 1) * L]
        cum = _dot3_l(tri_b, dtc * a)
        cum_t = _transpose_rows(cum, eye)
        dt_t = _transpose_rows(dtc, eye)
        cum_last = cum[L - 1:L, :]
        wend = (jnp.exp(cum_last - cum) * dtc).astype(BF16)
        xw_buf[r] = (act_buf[r, :, :d_in] * _dot(wend, expand)).astype(BF16)
        dec_last = jnp.exp(_dot3_r(cum[L - 8:L, :], expand)[7:8, :])
        for g in range(SSM_GROUPS):
            bg = act_buf[r, :, d_in + g * SSM_STATE:d_in + (g + 1) * SSM_STATE].astype(BF16)
            cg = act_buf[r, :, d_in + gn + g * SSM_STATE:d_in + gn + (g + 1) * SSM_STATE].astype(BF16)
            cb = _dot_nt(cg, bg)
            hg = ho_ref[r, g]
            y_int = _dot(cg, hg.astype(BF16))
            for pr in range(SSM_HPG // 2):
                hd0 = g * SSM_HPG + 2 * pr
                lo = hd0 * SSM_HEAD_DIM
                xp = act_buf[r, :, lo:lo + LANES]
                x2 = jnp.concatenate([jnp.where(left, xp, 0.0), jnp.where(left, 0.0, xp)], axis=0).astype(BF16)
                wms, es = [], []
                for hd in (hd0, hd0 + 1):
                    ccol = jnp.broadcast_to(cum[:, hd:hd + 1], (L, LANES))
                    dec = jnp.exp(jnp.where(tri, ccol[:, :L] - cum_t[hd:hd + 1, :], -jnp.inf))
                    wms.append((cb * dec * dt_t[hd:hd + 1, :]).astype(BF16))
                    es.append(jnp.exp(ccol))
                yi = y_int[:, pr * LANES:(pr + 1) * LANES]
                y_buf[r, :, lo:lo + LANES] = (_dot(jnp.concatenate(wms, axis=1), x2)
                                              + yi * jnp.where(left, es[0], es[1]))
            ho_ref[r, g] = (hg * dec_last[:, g * inner:(g + 1) * inner]
                           + _dot_tn(bg, xw_buf[r, :, g * inner:(g + 1) * inner]))

    gnw = gnw_ref[...]
    rows = []
    for r in range(R):
        y = y_buf[r] + dfull_ref[...] * act_buf[r, :, :d_in]
        y = y * _silu(z_buf[r * L:(r + 1) * L, :])
        rows.append(jnp.concatenate(
            [_rms(y[:, g * inner:(g + 1) * inner], gnw[:, g * inner:(g + 1) * inner]).astype(BF16)
             for g in range(SSM_GROUPS)], axis=-1))
    out = _dot(jnp.concatenate(rows, axis=0), wout_ref[...])
    for r in range(R):
        xo_ref[r] = x_ref[r] + mod_ref[r][2:3] * out[r * L:(r + 1) * L]


def _ssd_layer(x, mod, nw, conv0, h0, p, R, L):
    bsz, t, d = x.shape
    d_in = SSM_GROUPS * SSM_HPG * SSM_HEAD_DIM
    cdim = p['conv_w'].shape[1]
    blk = lambda b, i: (b, 0, 0)
    st_blk = (R, SSM_GROUPS, SSM_STATE, SSM_HPG * SSM_HEAD_DIM)
    names = ['wz', 'wx', 'wdt', 'conv_w', 'conv_b', 'dt_bias', 'a_log', 'd_full', 'norm', 'w_out', 'eye', 'expand']
    in_specs = [
        pl.BlockSpec((R, L, d), lambda b, i: (b, i, 0)),
        pl.BlockSpec((R, 6, d), blk),
        _const_spec((1, d)),
        pl.BlockSpec((R, CONV_W - 1, cdim), blk),
        pl.BlockSpec(st_blk, lambda b, i: (b, 0, 0, 0)),
    ] + [_const_spec(p[n].shape) for n in names]
    out_shape = (jax.ShapeDtypeStruct((bsz, t, d), F32),
                 jax.ShapeDtypeStruct((bsz, CONV_W - 1, cdim), F32),
                 jax.ShapeDtypeStruct(h0.shape, F32))
    out_specs = (pl.BlockSpec((R, L, d), lambda b, i: (b, i, 0)),
                 pl.BlockSpec((R, CONV_W - 1, cdim), blk),
                 pl.BlockSpec(st_blk, lambda b, i: (b, 0, 0, 0)))
    scratch = [pltpu.VMEM((R, 8 + L, cdim), F32), pltpu.VMEM((R, L, cdim), F32), pltpu.VMEM((R * L, d_in), F32),
               pltpu.VMEM((R, L, d_in), F32), pltpu.VMEM((R, L, d_in), BF16)]
    return pl.pallas_call(
        functools.partial(_ssd_kernel, R=R, L=L),
        out_shape=out_shape, grid=(bsz // R, t // L), in_specs=in_specs, out_specs=out_specs,
        scratch_shapes=scratch,
        compiler_params=pltpu.CompilerParams(
            dimension_semantics=("arbitrary", "arbitrary"), vmem_limit_bytes=VMEM_LIMIT),
        name="ssd_mixer",
    )(x, mod, nw, conv0, h0, *[p[n] for n in names])


def _ffn_kernel(x_ref, mod_ref, nw_ref, wg_ref, wu_ref, wd_ref, o_ref, *, nchunk):
    x = x_ref[0]
    mod = mod_ref[0]
    hn = (_rms(x, nw_ref[...]) * (1.0 + mod[4:5]) + mod[3:4]).astype(BF16)
    fc = wg_ref.shape[1] // nchunk
    acc = None
    for j in range(nchunk):
        g = _dot(hn, wg_ref[:, j * fc:(j + 1) * fc])
        u = _dot(hn, wu_ref[:, j * fc:(j + 1) * fc])
        part = _dot((_silu(g) * u).astype(BF16), wd_ref[j * fc:(j + 1) * fc, :])
        acc = part if acc is None else acc + part
    o_ref[0] = x + mod[5:6] * acc


def _ffn_layer(x, mod, nw, p, tm):
    bsz, t, d = x.shape
    tm = min(t, 2 * tm)
    return pl.pallas_call(
        functools.partial(_ffn_kernel, nchunk=p['w_gate'].shape[1] // (2 * LANES)),
        out_shape=jax.ShapeDtypeStruct(x.shape, F32),
        grid=(bsz, t // tm),
        in_specs=[pl.BlockSpec((1, tm, d), lambda b, i: (b, i, 0)),
                  pl.BlockSpec((1, 6, d), lambda b, i: (b, 0, 0)),
                  _const_spec((1, d)),
                  _const_spec(p['w_gate'].shape), _const_spec(p['w_up'].shape), _const_spec(p['w_down'].shape)],
        out_specs=pl.BlockSpec((1, tm, d), lambda b, i: (b, i, 0)),
        compiler_params=pltpu.CompilerParams(
            dimension_semantics=("arbitrary", "arbitrary"), vmem_limit_bytes=VMEM_LIMIT),
        name="ffn",
    )(x, mod, nw, p['w_gate'], p['w_up'], p['w_down'])


def _mlstm_kernel(x_ref, mod_ref, nw_ref, conv0_ref, c0_ref, n0_ref, m0_ref,
                  wxm_ref, wo_ref, cw_ref, cb_ref, wq_ref, wk_ref, wkt_ref, wv_ref, wgq_ref, wgk_ref, wgv_ref, bg_ref,
                  gnw_ref, skip_ref, wout_ref, eyei_ref, eyef_ref,
                  xo_ref, convo_ref, co_ref, no_ref, mo_ref,
                  xm_buf, xc_buf, q_buf, k_buf, kt_buf, v_buf, op_buf, hh_buf,
                  *, R, L):
    t = pl.program_id(1)
    hd_dim = wq_ref.shape[1]
    rep = hd_dim // LANES
    k_scale = hd_dim ** -0.5

    @pl.when(t == 0)
    def _():
        for r in range(R):
            xm_buf[r, TAIL0:8, :] = conv0_ref[r]
        co_ref[...] = c0_ref[...]
        no_ref[...] = n0_ref[...]
        mo_ref[...] = m0_ref[...]

    nw = nw_ref[...]
    hn = jnp.concatenate(
        [(_rms(x_ref[r], nw) * (1.0 + mod_ref[r][1:2]) + mod_ref[r][0:1]).astype(BF16) for r in range(R)], axis=0)
    xm = _dot(hn, wxm_ref[...])
    for r in range(R):
        xm_buf[r, 8:8 + L, :] = xm[r * L:(r + 1) * L]
    op_buf[...] = _dot(hn, wo_ref[...])
    for r in range(R):
        _causal_conv_silu(xm_buf.at[r], xc_buf.at[r], L, cw_ref, cb_ref)
        tail = xm_buf[r, L + TAIL0:L + 8, :]
        xm_buf[r, TAIL0:8, :] = tail
        convo_ref[r] = tail

    gates = bg_ref[...]
    for h in range(ML_HEADS):
        sl = slice(h * hd_dim, (h + 1) * hd_dim)
        xc_h = jnp.concatenate([xc_buf[r, :, sl].astype(BF16) for r in range(R)], axis=0)
        xm_h = jnp.concatenate([xm_buf[r, 8:8 + L, sl].astype(BF16) for r in range(R)], axis=0)
        q = _dot(xc_h, wq_ref[h])
        k = _dot(xc_h, wk_ref[h])
        v = _dot(xm_h, wv_ref[h])
        gates = gates + _dot(q.astype(BF16), wgq_ref[sl, :]) + _dot(k.astype(BF16), wgk_ref[sl, :]) \
            + _dot(v.astype(BF16), wgv_ref[sl, :])
        q_buf[:, sl] = q.astype(BF16)
        k_buf[:, sl] = (k * k_scale).astype(BF16)
        v_buf[:, sl] = v.astype(BF16)
        for r in range(R):
            kt_buf[r, h] = _dot_nt(wkt_ref[h], xc_h[r * L:(r + 1) * L]) * k_scale
    lf = jnp.minimum(gates, 0.0) - jnp.log1p(jnp.exp(-jnp.abs(gates)))
    tri = _tri_mask(L)
    tri_b = jnp.where(tri, 1.0, 0.0).astype(BF16)
    lane = lax.broadcasted_iota(jnp.int32, (1, LANES), 1)
    wide = lambda a: jnp.concatenate([a] * rep, axis=1)

    for r in range(R):
        rs = slice(r * L, (r + 1) * L)
        gi = gates[rs]
        bcum = _dot3_l(tri_b, lf[rs])
        li_t = _transpose_rows(gi, eyei_ref[...])
        b_t = _transpose_rows(bcum, eyef_ref[...])
        m_prev = mo_ref[r]
        m_next = m_prev
        for h in range(ML_HEADS):
            sl = slice(h * hd_dim, (h + 1) * hd_dim)
            bcol = jnp.broadcast_to(bcum[:, ML_HEADS + h:ML_HEADS + h + 1], (L, LANES))
            mp = jnp.broadcast_to(m_prev[:, h:h + 1], (1, LANES))
            logw_t = li_t[h:h + 1, :] - b_t[h:h + 1, :]
            dmat = jnp.where(tri, bcol[:, :L] + logw_t, -jnp.inf)
            a_inter = bcol + mp
            m_t = jnp.maximum(a_inter, jnp.max(dmat, axis=1, keepdims=True))
            qb = q_buf[rs, sl]
            vb = v_buf[rs, sl]
            s_mat = jnp.exp(dmat - m_t[:, :L]) * _dot_nt(qb, k_buf[rs, sl])
            w_inter = jnp.exp(a_inter - m_t)
            ch = co_ref[r, h]
            nh = no_ref[r, h:h + 1, :]
            num = _dot(s_mat.astype(BF16), vb) + wide(w_inter) * _dot(qb, ch.astype(BF16))
            qn = jnp.sum(qb.astype(F32) * nh, axis=1, keepdims=True)
            den = jnp.sum(s_mat, axis=1, keepdims=True) + w_inter * qn
            inv = 1.0 / jnp.maximum(jnp.abs(den), jnp.exp(-m_t))
            hh_buf[r, :, sl] = num * wide(inv)
            m_new = m_t[L - 1:L, :]
            b_last = bcol[L - 1:L, :]
            w_end = jnp.exp(b_last[:, :L] + logw_t - m_new[:, :L])
            w_old = wide(jnp.exp(b_last + mp - m_new))
            kw_t = (kt_buf[r, h] * w_end).astype(BF16)
            co_ref[r, h] = w_old * ch + _dot(kw_t, vb)
            w8 = jnp.broadcast_to(w_end, (8, L)).astype(BF16)
            no_ref[r, h:h + 1, :] = w_old * nh + _dot(w8, k_buf[rs, sl])[0:1, :]
            m_next = jnp.where(lane == h, m_new, m_next)
        mo_ref[r] = m_next

    gnw = gnw_ref[...]
    skip = skip_ref[...]
    rows = []
    for r in range(R):
        parts = []
        for h in range(ML_HEADS):
            sl = slice(h * hd_dim, (h + 1) * hd_dim)
            hn_h = _rms(hh_buf[r, :, sl], gnw[:, sl])
            parts.append(((hn_h + skip[:, sl] * xc_buf[r, :, sl])
                          * _sigmoid(op_buf[r * L:(r + 1) * L, sl])).astype(BF16))
        rows.append(jnp.concatenate(parts, axis=-1))
    out = _dot(jnp.concatenate(rows, axis=0), wout_ref[...])
    for r in range(R):
        xo_ref[r] = x_ref[r] + mod_ref[r][2:3] * out[r * L:(r + 1) * L]


def _mlstm_layer(x, mod, nw, conv0, c0, n0, m0, p, R, L):
    bsz, t, d = x.shape
    inner = p['conv_w'].shape[1]
    hd = inner // ML_HEADS
    blk = lambda b, i: (b, 0, 0)
    blk4 = lambda b, i: (b, 0, 0, 0)
    names = ['wxm', 'wo', 'conv_w', 'conv_b', 'w_q', 'w_k', 'w_kt', 'w_v', 'wgq', 'wgk', 'wgv', 'bg',
             'norm', 'skip', 'w_out', 'eye_i', 'eye_f']
    in_specs = [
        pl.BlockSpec((R, L, d), lambda b, i: (b, i, 0)),
        pl.BlockSpec((R, 6, d), blk),
        _const_spec((1, d)),
        pl.BlockSpec((R, CONV_W - 1, inner), blk),
        pl.BlockSpec((R, ML_HEADS, hd, hd), blk4, pipeline_mode=pl.Buffered(1)),
        pl.BlockSpec((R, ML_HEADS, hd), blk),
        pl.BlockSpec((R, 1, LANES), blk),
    ] + [_const_spec(p[n].shape) for n in names]
    out_shape = (jax.ShapeDtypeStruct((bsz, t, d), F32),
                 jax.ShapeDtypeStruct((bsz, CONV_W - 1, inner), F32),
                 jax.ShapeDtypeStruct((bsz, ML_HEADS, hd, hd), F32),
                 jax.ShapeDtypeStruct((bsz, ML_HEADS, hd), F32),
                 jax.ShapeDtypeStruct((bsz, 1, LANES), F32))
    out_specs = (pl.BlockSpec((R, L, d), lambda b, i: (b, i, 0)),
                 pl.BlockSpec((R, CONV_W - 1, inner), blk),
                 pl.BlockSpec((R, ML_HEADS, hd, hd), blk4),
                 pl.BlockSpec((R, ML_HEADS, hd), blk),
                 pl.BlockSpec((R, 1, LANES), blk))
    scratch = [pltpu.VMEM((R, 8 + L, inner), F32), pltpu.VMEM((R, L, inner), F32),
               pltpu.VMEM((R * L, inner), BF16), pltpu.VMEM((R * L, inner), BF16),
               pltpu.VMEM((R, ML_HEADS, hd, L), F32), pltpu.VMEM((R * L, inner), BF16),
               pltpu.VMEM((R * L, inner), F32), pltpu.VMEM((R, L, inner), F32)]
    return pl.pallas_call(
        functools.partial(_mlstm_kernel, R=R, L=L),
        out_shape=out_shape, grid=(bsz // R, t // L), in_specs=in_specs, out_specs=out_specs,
        scratch_shapes=scratch,
        compiler_params=pltpu.CompilerParams(
            dimension_semantics=("arbitrary", "arbitrary"), vmem_limit_bytes=VMEM_LIMIT),
        name="mlstm_mixer",
    )(x, mod, nw, conv0, c0, n0, m0, *[p[n] for n in names])


def _router_kernel(x_ref, mod_ref, nw_ref, wr_ref, br_ref, h_ref, info_ref, cnt_ref, *, tm):
    x = x_ref[0]
    mod = mod_ref[0]
    h = _rms(x, nw_ref[...]) * (1.0 + mod[4:5]) + mod[3:4]
    h_ref[0] = h
    h1, h2, _ = _split3(h)
    w1, w2, _ = _split3(wr_ref[...])
    logits = (_dot(h1, w1) + (_dot(h1, w2) + _dot(h2, w1))) + br_ref[...]
    lane = lax.broadcasted_iota(jnp.int32, (tm, LANES), 1)
    lg = jnp.where(lane < N_EXPERTS, logits, -jnp.inf)
    m1 = jnp.max(lg, axis=1, keepdims=True)
    i1 = jnp.min(jnp.where(lg == m1, lane, LANES), axis=1, keepdims=True)
    lg2 = jnp.where(lane == i1, -jnp.inf, lg)
    m2 = jnp.max(lg2, axis=1, keepdims=True)
    i2 = jnp.min(jnp.where(lg2 == m2, lane, LANES), axis=1, keepdims=True)
    e2 = jnp.exp(m2 - m1)
    w_top1 = 1.0 / (1.0 + e2)
    w_top2 = e2 / (1.0 + e2)
    sel = jnp.logical_or(lane == i1, lane == i2)
    mask = jnp.where(sel, 1.0, 0.0).astype(BF16)
    r = lax.broadcasted_iota(jnp.int32, (tm, tm), 0)
    c = lax.broadcasted_iota(jnp.int32, (tm, tm), 1)
    before = jnp.where(c < r, 1.0, 0.0).astype(BF16)
    rank = _dot(before, mask)
    r1 = jnp.sum(jnp.where(lane == i1, rank, 0.0), axis=1, keepdims=True)
    r2 = jnp.sum(jnp.where(lane == i2, rank, 0.0), axis=1, keepdims=True)
    cols = (i1.astype(F32), i2.astype(F32), r1, r2, w_top1, w_top2)
    info = jnp.zeros((tm, LANES), F32)
    for k, col in enumerate(cols):
        info = jnp.where(lane == k, col, info)
    info_ref[0] = info
    cnt = jnp.sum(mask.astype(F32), axis=0, keepdims=True)
    cnt_ref[0, 0] = jnp.broadcast_to(cnt, (8, LANES)).astype(jnp.int32)


def _router(x, mod, nw, p, tm):
    bsz, t, d = x.shape
    nt = t // tm
    ti = lambda b, i: (b, i, 0)
    t4 = lambda b, i: (b, i, 0, 0)
    out_shape = (jax.ShapeDtypeStruct((bsz, t, d), F32),
                 jax.ShapeDtypeStruct((bsz, t, LANES), F32),
                 jax.ShapeDtypeStruct((bsz, nt, 8, LANES), jnp.int32))
    out_specs = (pl.BlockSpec((1, tm, d), ti), pl.BlockSpec((1, tm, LANES), ti),
                 pl.BlockSpec((1, 1, 8, LANES), t4))
    return pl.pallas_call(
        functools.partial(_router_kernel, tm=tm),
        out_shape=out_shape, grid=(bsz, nt),
        in_specs=[pl.BlockSpec((1, tm, d), ti), pl.BlockSpec((1, 6, d), lambda b, i: (b, 0, 0)),
                  _const_spec((1, d)), _const_spec(p['w_router'].shape), _const_spec(p['b_router'].shape)],
        out_specs=out_specs,
        compiler_params=pltpu.CompilerParams(
            dimension_semantics=("arbitrary", "arbitrary"), vmem_limit_bytes=VMEM_LIMIT),
        name="router",
    )(x, mod, nw, p['w_router'], p['b_router'])


SLAB = 512
DMA_UNROLL = 8


def _dispatch_kernel(ids_ref, pos_ref, h_ref, xs_ref, zbuf, sem, *, tm):
    @pl.when(jnp.logical_and(pl.program_id(0) == 0, pl.program_id(1) == 0))
    def _():
        zbuf[...] = jnp.zeros(zbuf.shape, F32)
        for i in range(2 * N_EXPERTS):
            c = pltpu.make_async_copy(zbuf, xs_ref.at[pl.ds(ids_ref[i] * SLAB, SLAB), :], sem)
            c.start()
            c.wait()

    def send(t, carry):
        for k in range(2):
            pltpu.make_async_copy(h_ref.at[0, pl.ds(t, 1), :],
                                  xs_ref.at[pl.ds(pos_ref[0, 0, k * tm + t], 1), :], sem).start(priority=k)
        return carry

    lax.fori_loop(0, tm, send, 0, unroll=DMA_UNROLL)
    for k in range(2):
        pltpu.make_async_copy(h_ref.at[0], xs_ref.at[pl.ds(0, tm), :], sem).wait()


def _dispatch(h, pos, slab_ids, n_rows, tm):
    bsz, t, d = h.shape
    nt = t // tm
    return pl.pallas_call(
        functools.partial(_dispatch_kernel, tm=tm),
        out_shape=jax.ShapeDtypeStruct((n_rows, d), F32),
        grid_spec=pltpu.PrefetchScalarGridSpec(
            num_scalar_prefetch=1, grid=(bsz, nt),
            in_specs=[pl.BlockSpec((1, 1, 2 * tm), lambda b, i, ids: (b * nt + i, 0, 0), memory_space=pltpu.SMEM),
                      pl.BlockSpec((1, tm, d), lambda b, i, ids: (b, i, 0))],
            out_specs=pl.BlockSpec(memory_space=pl.ANY),
            scratch_shapes=[pltpu.VMEM((SLAB, d), F32), pltpu.SemaphoreType.DMA(())]),
        compiler_params=pltpu.CompilerParams(
            dimension_semantics=("arbitrary", "arbitrary"), vmem_limit_bytes=VMEM_LIMIT),
        name="dispatch",
    )(slab_ids, pos, h)


def _slab_ffn_kernel(se_ref, nu_ref, x_ref, wg_ref, wu_ref, wd_ref, y_ref, *, nchunk):
    s = pl.program_id(0)

    @pl.when(s < nu_ref[0])
    def _():
        xb = x_ref[...].astype(BF16)
        fc = wg_ref.shape[2] // nchunk
        acc = None
        for j in range(nchunk):
            g = _dot(xb, wg_ref[0, :, j * fc:(j + 1) * fc])
            u = _dot(xb, wu_ref[0, :, j * fc:(j + 1) * fc])
            part = _dot((_silu(g) * u).astype(BF16), wd_ref[0, j * fc:(j + 1) * fc, :])
            acc = part if acc is None else acc + part
        y_ref[...] = acc

    @pl.when(s >= nu_ref[0])
    def _():
        y_ref[...] = jnp.zeros(y_ref.shape, F32)


def _slab_ffn(xs, slab_expert, n_used, wg, wu, wd):
    n_rows, d = xs.shape
    f = wg.shape[2]
    w_idx = lambda s, se, nu: (se[s], 0, 0)
    grid_spec = pltpu.PrefetchScalarGridSpec(
        num_scalar_prefetch=2, grid=(n_rows // SLAB,),
        in_specs=[pl.BlockSpec((SLAB, d), lambda s, se, nu: (s, 0)),
                  pl.BlockSpec((1, d, f), w_idx), pl.BlockSpec((1, d, f), w_idx), pl.BlockSpec((1, f, d), w_idx)],
        out_specs=pl.BlockSpec((SLAB, d), lambda s, se, nu: (s, 0)))
    return pl.pallas_call(
        functools.partial(_slab_ffn_kernel, nchunk=f // (2 * LANES)),
        out_shape=jax.ShapeDtypeStruct((n_rows, d), F32),
        grid_spec=grid_spec,
        compiler_params=pltpu.CompilerParams(dimension_semantics=("arbitrary",), vmem_limit_bytes=VMEM_LIMIT),
        name="slab_ffn",
    )(slab_expert, n_used, xs, wg, wu, wd)


def _combine_kernel(pos_ref, pos_next_ref, x_ref, info_ref, mod_ref, fw_ref, ys_ref, o_ref, buf, sem, *, tm):
    step = pl.program_id(0) * pl.num_programs(1) + pl.program_id(1)
    last = pl.num_programs(0) * pl.num_programs(1) - 1
    slot = lax.rem(step, 2)

    def request(p_ref, sl):
        def fetch(t, carry):
            for k in range(2):
                pltpu.make_async_copy(ys_ref.at[pl.ds(p_ref[0, 0, k * tm + t], 1), :],
                                      buf.at[sl, k, pl.ds(t, 1), :], sem.at[sl]).start(priority=k)
            return carry
        lax.fori_loop(0, tm, fetch, 0, unroll=DMA_UNROLL)

    @pl.when(step == 0)
    def _():
        request(pos_ref, 0)

    @pl.when(step < last)
    def _():
        request(pos_next_ref, 1 - slot)

    for k in range(2):
        pltpu.make_async_copy(ys_ref.at[pl.ds(0, tm), :], buf.at[slot, k], sem.at[slot]).wait()
    info = info_ref[0]
    y = x_ref[0] + mod_ref[0][5:6] * (info[:, 4:5] * buf[slot, 0] + info[:, 5:6] * buf[slot, 1])
    o_ref[0] = _rms(y, fw_ref[...])


def _combine(x, info, mod, fw, ys, pos, tm):
    bsz, t, d = x.shape
    nt = t // tm
    ti = lambda b, i: (b, i, 0)
    tiles = bsz * nt
    return pl.pallas_call(
        functools.partial(_combine_kernel, tm=tm),
        out_shape=jax.ShapeDtypeStruct(x.shape, F32),
        grid=(bsz, nt),
        in_specs=[pl.BlockSpec((1, 1, 2 * tm), lambda b, i: (b * nt + i, 0, 0), memory_space=pltpu.SMEM),
                  pl.BlockSpec((1, 1, 2 * tm), lambda b, i: (jnp.minimum(b * nt + i + 1, tiles - 1), 0, 0),
                               memory_space=pltpu.SMEM),
                  pl.BlockSpec((1, tm, d), ti), pl.BlockSpec((1, tm, LANES), ti),
                  pl.BlockSpec((1, 6, d), lambda b, i: (b, 0, 0)), _const_spec(fw.shape),
                  pl.BlockSpec(memory_space=pl.ANY)],
        out_specs=pl.BlockSpec((1, tm, d), ti),
        scratch_shapes=[pltpu.VMEM((2, 2, tm, d), F32), pltpu.SemaphoreType.DMA((2,))],
        compiler_params=pltpu.CompilerParams(
            dimension_semantics=("arbitrary", "arbitrary"), vmem_limit_bytes=VMEM_LIMIT),
        name="combine",
    )(pos, pos, x, info, mod, fw, ys)


def _moe_layer(x, mod, nw, p, fw, tm):
    bsz, t, _ = x.shape
    n_tok = bsz * t
    tiles = n_tok // tm
    h, info, cnt = _router(x, mod, nw, p, tm)
    counts = cnt[:, :, 0, :N_EXPERTS].reshape(tiles, N_EXPERTS)
    base = jnp.cumsum(counts, axis=0) - counts
    slabs = (jnp.sum(counts, axis=0) + (SLAB - 1)) // SLAB
    slab_end = jnp.cumsum(slabs)
    start = (slab_end - slabs) * SLAB
    n_slabs = (2 * n_tok) // SLAB + N_EXPERTS
    experts = jnp.arange(N_EXPERTS, dtype=jnp.int32)
    slab_expert = jnp.minimum(jnp.sum(jnp.arange(n_slabs, dtype=jnp.int32)[:, None] >= slab_end[None, :], axis=1),
                              N_EXPERTS - 1).astype(jnp.int32)
    inf2 = info.reshape(tiles, tm, LANES)
    offs = (start[None, :] + base)[:, None, :]
    pos = []
    for k in range(2):
        e_k = inf2[:, :, k].astype(jnp.int32)
        off_k = jnp.sum(jnp.where(e_k[:, :, None] == experts[None, None, :], offs, 0), axis=-1)
        pos.append(off_k + inf2[:, :, 2 + k].astype(jnp.int32))
    pos = jnp.concatenate(pos, axis=1).reshape(tiles, 1, 2 * tm).astype(jnp.int32)
    partial = jnp.concatenate([jnp.maximum(slab_end - 1, 0), jnp.minimum(slab_end[-1] + experts, n_slabs - 1)])
    xs = _dispatch(h, pos, partial.astype(jnp.int32), n_slabs * SLAB, tm)
    ys = _slab_ffn(xs, slab_expert, slab_end[-1:].astype(jnp.int32), p['w_gate'], p['w_up'], p['w_down'])
    return _combine(x, info, mod, fw, ys, pos, tm)


def _cast_kernel(x_ref, o_ref):
    o_ref[...] = x_ref[...].astype(BF16)


def _to_bf16(a, tr=512):
    a2 = a.reshape(-1, a.shape[-1])
    rows, cols = a2.shape
    assert rows % tr == 0
    out = pl.pallas_call(
        _cast_kernel,
        out_shape=jax.ShapeDtypeStruct(a2.shape, BF16),
        grid=(rows // tr,),
        in_specs=[pl.BlockSpec((tr, cols), lambda i: (i, 0))],
        out_specs=pl.BlockSpec((tr, cols), lambda i: (i, 0)),
        compiler_params=pltpu.CompilerParams(
            dimension_semantics=("arbitrary",), vmem_limit_bytes=VMEM_LIMIT),
        name="to_bf16",
    )(a2)
    return out.reshape(a.shape)


def _pad_lanes(a):
    return jnp.pad(a, [(0, 0)] * (a.ndim - 1) + [(0, LANES - a.shape[-1])])


def _one_hot_rows(n, offset):
    r = lax.broadcasted_iota(jnp.int32, (n, LANES), 0)
    c = lax.broadcasted_iota(jnp.int32, (n, LANES), 1)
    return (c == r + offset).astype(BF16)


def _prep_ssm(w_in, conv_w, conv_b, dt_bias, a_log, d_skip, norm_w, w_out):
    d_in = SSM_GROUPS * SSM_HPG * SSM_HEAD_DIM
    cdim = conv_w.shape[1]
    heads = SSM_GROUPS * SSM_HPG
    r = lax.broadcasted_iota(jnp.int32, (LANES, d_in), 0)
    c = lax.broadcasted_iota(jnp.int32, (LANES, d_in), 1)
    return dict(
        wz=w_in[:, :d_in].astype(BF16), wx=w_in[:, d_in:d_in + cdim].astype(BF16),
        wdt=_pad_lanes(w_in[:, d_in + cdim:]).astype(BF16),
        conv_w=conv_w, conv_b=conv_b.reshape(1, cdim),
        dt_bias=_pad_lanes(dt_bias.reshape(1, heads)), a_log=_pad_lanes(a_log.reshape(1, heads)),
        d_full=jnp.repeat(d_skip, SSM_HEAD_DIM).reshape(1, d_in),
        norm=norm_w.reshape(1, d_in), w_out=w_out.astype(BF16),
        eye=_one_hot_rows(heads, 0), expand=(c // SSM_HEAD_DIM == r).astype(BF16))


def _prep_mlstm(w_in, conv_w, conv_b, w_q, w_k, w_v, w_ig, b_ig, w_fg, b_fg, norm_w, skip, w_out):
    inner = conv_w.shape[1]
    hd = inner // ML_HEADS
    wg = jnp.concatenate([w_ig, w_fg], axis=-1).reshape(ML_HEADS, 3, hd, 2 * ML_HEADS)
    part = lambda j: _pad_lanes(wg[:, j].reshape(inner, 2 * ML_HEADS)).astype(BF16)
    return dict(
        wxm=w_in[:, :inner].astype(BF16), wo=w_in[:, inner:].astype(BF16),
        conv_w=conv_w, conv_b=conv_b.reshape(1, inner),
        w_q=w_q.astype(BF16), w_k=w_k.astype(BF16), w_kt=jnp.swapaxes(w_k, 1, 2).astype(BF16),
        w_v=w_v.astype(BF16),
        wgq=part(0), wgk=part(1), wgv=part(2),
        bg=_pad_lanes(jnp.concatenate([b_ig, b_fg]).reshape(1, 2 * ML_HEADS)),
        norm=norm_w.reshape(1, inner), skip=skip.reshape(1, inner), w_out=w_out.astype(BF16),
        eye_i=_one_hot_rows(ML_HEADS, 0), eye_f=_one_hot_rows(ML_HEADS, ML_HEADS))


def _trunk(x, mod, ssm_conv, ssm_state, ml_conv, ml_c, ml_n, ml_m, p):
    bsz, t, d = x.shape
    L = SCAN_CHUNK if t % SCAN_CHUNK == 0 else t
    R = 2 if bsz % 2 == 0 else 1
    tm = min(t, 512)
    n_heads = SSM_GROUPS * SSM_HPG
    h0 = ssm_state.reshape(bsz, SSM_GROUPS, SSM_HPG, SSM_HEAD_DIM, SSM_STATE)
    h0 = h0.transpose(0, 1, 4, 2, 3).reshape(bsz, SSM_GROUPS, SSM_STATE, SSM_HPG * SSM_HEAD_DIM)
    x, conv_s, h_s = _ssd_layer(x, mod[0], p['norm_mix'][0], ssm_conv, h0, p['ssm'], R, L)
    h_s = h_s.reshape(bsz, SSM_GROUPS, SSM_STATE, SSM_HPG, SSM_HEAD_DIM).transpose(0, 1, 3, 4, 2)
    h_s = h_s.reshape(bsz, n_heads, SSM_HEAD_DIM, SSM_STATE)
    x = _ffn_layer(x, mod[0], p['norm_ffn'][0], p['ffn'], tm)
    m0 = _pad_lanes(ml_m).reshape(bsz, 1, LANES)
    x, conv_m, c_m, n_m, m_m = _mlstm_layer(x, mod[1], p['norm_mix'][1], ml_conv, ml_c, ml_n, m0, p['ml'], R, L)
    y = _moe_layer(x, mod[1], p['norm_ffn'][1], p['moe'], p['norm_final'], tm)
    return (y, conv_s[None], h_s[None], conv_m[None], c_m[None], n_m[None], m_m[:, 0, :ML_HEADS][None])


def kernel(x_prompt, x_sample, c_prompt, c_sample, state_ssm_conv, state_ssm, state_mlstm_conv, state_mlstm_C, state_mlstm_n, state_mlstm_m, w_ada, b_ada, norm_mix, norm_ffn, norm_final, ssm_w_in, ssm_conv_w, ssm_conv_b, ssm_dt_bias, ssm_a_log, ssm_d, ssm_norm, ssm_w_out, ml_w_in, ml_conv_w, ml_conv_b, ml_w_q, ml_w_k, ml_w_v, ml_w_igate, ml_b_igate, ml_w_fgate, ml_b_fgate, ml_norm, ml_skip, ml_w_out, ffn_w_gate, ffn_w_up, ffn_w_down, moe_w_router, moe_b_router, moe_w_gate, moe_w_up, moe_w_down):
    depth, d, _ = w_ada.shape
    assert depth == 2 and state_ssm.shape[0] == 1 and state_mlstm_C.shape[0] == 1
    bp, bs = x_prompt.shape[0], x_sample.shape[0]
    p = dict(
        norm_mix=norm_mix.reshape(depth, 1, d), norm_ffn=norm_ffn.reshape(depth, 1, d),
        norm_final=norm_final.reshape(1, d),
        ssm=_prep_ssm(ssm_w_in[0], ssm_conv_w[0], ssm_conv_b[0], ssm_dt_bias[0], ssm_a_log[0], ssm_d[0],
                      ssm_norm[0], ssm_w_out[0]),
        ml=_prep_mlstm(ml_w_in[0], ml_conv_w[0], ml_conv_b[0], ml_w_q[0], ml_w_k[0], ml_w_v[0], ml_w_igate[0],
                       ml_b_igate[0], ml_w_fgate[0], ml_b_fgate[0], ml_norm[0], ml_skip[0], ml_w_out[0]),
        ffn=dict(w_gate=ffn_w_gate[0].astype(BF16), w_up=ffn_w_up[0].astype(BF16),
                 w_down=ffn_w_down[0].astype(BF16)),
        moe=dict(w_router=_pad_lanes(moe_w_router[0]), b_router=_pad_lanes(moe_b_router[0].reshape(1, -1)),
                 w_gate=_to_bf16(moe_w_gate[0]), w_up=_to_bf16(moe_w_up[0]), w_down=_to_bf16(moe_w_down[0])))
    mod = _ada(jnp.concatenate([c_prompt, c_sample], axis=0), w_ada, b_ada)
    mod = mod.reshape(depth, bp + bs, 6, d)

    f = F32
    zeros = lambda a, b: jnp.zeros((b,) + a.shape[2:], f)
    out_p = _trunk(x_prompt, mod[:, :bp], zeros(state_ssm_conv, bp), zeros(state_ssm, bp),
                   zeros(state_mlstm_conv, bp), zeros(state_mlstm_C, bp), zeros(state_mlstm_n, bp),
                   zeros(state_mlstm_m, bp), p)
    out_s = _trunk(x_sample, mod[:, bp:], state_ssm_conv[0], state_ssm[0], state_mlstm_conv[0],
                   state_mlstm_C[0], state_mlstm_n[0], state_mlstm_m[0], p)
    return (out_p[0], out_s[0]) + tuple(out_p[1:]) + tuple(out_s[1:])
```

```python
import functools

import jax
import jax.numpy as jnp
from jax import lax
from jax.experimental import pallas as pl
from jax.experimental.pallas import tpu as pltpu

F32 = jnp.float32
BF16 = jnp.bfloat16
RMS_EPS = 1e-6
CONV_W = 4
LANES = 128
SCAN_CHUNK = LANES
TAIL0 = 8 - (CONV_W - 1)
VMEM_LIMIT = 60 * 1024 * 1024

SSM_GROUPS = 4
SSM_HPG = 8
SSM_HEAD_DIM = 64
SSM_STATE = 128
ML_HEADS = 8
N_EXPERTS = 8


def _dot(a, b):
    return jnp.dot(a, b, preferred_element_type=F32)


def _dot_nt(a, b):
    return lax.dot_general(a, b, (((1,), (1,)), ((), ())), preferred_element_type=F32)


def _dot_tn(a, b):
    return lax.dot_general(a, b, (((0,), (0,)), ((), ())), preferred_element_type=F32)


def _split3(x):
    h1 = x.astype(BF16)
    r = x - h1.astype(F32)
    h2 = r.astype(BF16)
    r = r - h2.astype(F32)
    return h1, h2, r.astype(BF16)


def _dot3_l(sel, x):
    return sum(_dot(sel, p) for p in _split3(x))


def _dot3_r(x, sel):
    return sum(_dot(p, sel) for p in _split3(x))


def _transpose_rows(x, eye):
    return sum(_dot_nt(eye, p) for p in _split3(x))


def _sigmoid(x):
    return 1.0 / (1.0 + jnp.exp(-x))


def _silu(x):
    return x * _sigmoid(x)


def _softplus(x):
    return jnp.maximum(x, 0.0) + jnp.log1p(jnp.exp(-jnp.abs(x)))


def _rms(x, g):
    return x * lax.rsqrt(jnp.mean(x * x, axis=-1, keepdims=True) + RMS_EPS) * g


def _tri_mask(n):
    r = lax.broadcasted_iota(jnp.int32, (n, n), 0)
    c = lax.broadcasted_iota(jnp.int32, (n, n), 1)
    return r >= c


def _causal_conv_silu(buf, dst, tt, cw_ref, cb_ref):
    xn = buf[8:8 + tt, :]
    x1 = pltpu.roll(xn, 1, axis=0)
    w = [cw_ref[k:k + 1, :] for k in range(CONV_W)]
    y = cb_ref[...] + xn * w[3] + x1 * w[2] + pltpu.roll(xn * w[1] + x1 * w[0], 2, axis=0)
    dst[...] = _silu(y)
    head = cb_ref[...]
    for k in range(CONV_W):
        head = head + buf[TAIL0 + k:TAIL0 + k + 8, :] * cw_ref[k:k + 1, :]
    dst[0:8, :] = _silu(head)


def _ada_kernel(c_ref, w_ref, b_ref, o_ref):
    ca = _silu(c_ref[...]).astype(BF16)
    o_ref[0] = _dot(ca, w_ref[0].astype(BF16)) + b_ref[0]


def _ada(c_all, w_ada, b_ada):
    depth, d, n = w_ada.shape
    bt = c_all.shape[0]
    tn = n // 4
    return pl.pallas_call(
        _ada_kernel,
        out_shape=jax.ShapeDtypeStruct((depth, bt, n), F32),
        grid=(depth, n // tn),
        in_specs=[pl.BlockSpec((bt, d), lambda i, j: (0, 0)),
                  pl.BlockSpec((1, d, tn), lambda i, j: (i, 0, j)),
                  pl.BlockSpec((1, 1, tn), lambda i, j: (i, 0, j))],
        out_specs=pl.BlockSpec((1, bt, tn), lambda i, j: (i, 0, j)),
        compiler_params=pltpu.CompilerParams(
            dimension_semantics=("arbitrary", "arbitrary"), vmem_limit_bytes=VMEM_LIMIT),
        name="ada",
    )(c_all, w_ada, b_ada.reshape(depth, 1, n))


def _const_spec(shape):
    nd = len(shape)
    return pl.BlockSpec(shape, lambda *_: (0,) * nd, pipeline_mode=pl.Buffered(1))


def _ssd_kernel(x_ref, mod_ref, nw_ref, conv0_ref, h0_ref, wz_ref, wx_ref, wdt_ref, cw_ref, cb_ref,
                dtb_ref, alog_ref, dfull_ref, gnw_ref, wout_ref, eye_ref, expand_ref,
                xo_ref, convo_ref, ho_ref,
                xbc_buf, act_buf, z_buf, y_buf, xw_buf, *, R, L):
    t = pl.program_id(1)
    inner = SSM_HPG * SSM_HEAD_DIM
    d_in = SSM_GROUPS * inner
    gn = SSM_GROUPS * SSM_STATE

    @pl.when(t == 0)
    def _():
        for r in range(R):
            xbc_buf[r, TAIL0:8, :] = conv0_ref[r]
        ho_ref[...] = h0_ref[...]

    nw = nw_ref[...]
    hn = jnp.concatenate(
        [(_rms(x_ref[r], nw) * (1.0 + mod_ref[r][1:2]) + mod_ref[r][0:1]).astype(BF16) for r in range(R)], axis=0)
    z_buf[...] = _dot(hn, wz_ref[...])
    xbc = _dot(hn, wx_ref[...])
    dt = _softplus(_dot(hn, wdt_ref[...]) + dtb_ref[...])
    a = -jnp.exp(alog_ref[...])
    tri = _tri_mask(L)
    tri_b = jnp.where(tri, 1.0, 0.0).astype(BF16)
    eye = eye_ref[...]
    expand = expand_ref[...]
    left = lax.broadcasted_iota(jnp.int32, (1, LANES), 1) < SSM_HEAD_DIM

    for r in range(R):
        xbc_buf[r, 8:8 + L, :] = xbc[r * L:(r + 1) * L]
        _causal_conv_silu(xbc_buf.at[r], act_buf.at[r], L, cw_ref, cb_ref)
        tail = xbc_buf[r, L + TAIL0:L + 8, :]
        xbc_buf[r, TAIL0:8, :] = tail
        convo_ref[r] = tail

    for r in range(R):
        dtc = dt[r * L:(r + 1) * L]
        cum = _dot3_l(tri_b, dtc * a)
        cum_t = _transpose_rows(cum, eye)
        dt_t = _transpose_rows(dtc, eye)
        cum_last = cum[L - 1:L, :]
        wend = (jnp.exp(cum_last - cum) * dtc).astype(BF16)
        xw_buf[r] = (act_buf[r, :, :d_in] * _dot(wend, expand)).astype(BF16)
        dec_last = jnp.exp(_dot3_r(cum[L - 8:L, :], expand)[7:8, :])
        for g in range(SSM_GROUPS):
            bg = act_buf[r, :, d_in + g * SSM_STATE:d_in + (g + 1) * SSM_STATE].astype(BF16)
            cg = act_buf[r, :, d_in + gn + g * SSM_STATE:d_in + gn + (g + 1) * SSM_STATE].astype(BF16)
            cb = _dot_nt(cg, bg)
            hg = ho_ref[r, g]
            y_int = _dot(cg, hg.astype(BF16))
            for pr in range(SSM_HPG // 2):
                hd0 = g * SSM_HPG + 2 * pr
                lo = hd0 * SSM_HEAD_DIM
                xp = act_buf[r, :, lo:lo + LANES]
                x2 = jnp.concatenate([jnp.where(left, xp, 0.0), jnp.where(left, 0.0, xp)], axis=0).astype(BF16)
                wms, es = [], []
                for hd in (hd0, hd0 + 1):
                    ccol = jnp.broadcast_to(cum[:, hd:hd + 1], (L, LANES))
                    dec = jnp.exp(jnp.where(tri, ccol[:, :L] - cum_t[hd:hd + 1, :], -jnp.inf))
                    wms.append((cb * dec * dt_t[hd:hd + 1, :]).astype(BF16))
                    es.append(jnp.exp(ccol))
                yi = y_int[:, pr * LANES:(pr + 1) * LANES]
                y_buf[r, :, lo:lo + LANES] = (_dot(jnp.concatenate(wms, axis=1), x2)
                                              + yi * jnp.where(left, es[0], es[1]))
            ho_ref[r, g] = (hg * dec_last[:, g * inner:(g + 1) * inner]
                           + _dot_tn(bg, xw_buf[r, :, g * inner:(g + 1) * inner]))

    gnw = gnw_ref[...]
    rows = []
    for r in range(R):
        y = y_buf[r] + dfull_ref[...] * act_buf[r, :, :d_in]
        y = y * _silu(z_buf[r * L:(r + 1) * L, :])
        rows.append(jnp.concatenate(
            [_rms(y[:, g * inner:(g + 1) * inner], gnw[:, g * inner:(g + 1) * inner]).astype(BF16)
             for g in range(SSM_GROUPS)], axis=-1))
    out = _dot(jnp.concatenate(rows, axis=0), wout_ref[...])
    for r in range(R):
        xo_ref[r] = x_ref[r] + mod_ref[r][2:3] * out[r * L:(r + 1) * L]


def _ssd_layer(x, mod, nw, conv0, h0, p, R, L):
    bsz, t, d = x.shape
    d_in = SSM_GROUPS * SSM_HPG * SSM_HEAD_DIM
    cdim = p['conv_w'].shape[1]
    blk = lambda b, i: (b, 0, 0)
    st_blk = (R, SSM_GROUPS, SSM_STATE, SSM_HPG * SSM_HEAD_DIM)
    names = ['wz', 'wx', 'wdt', 'conv_w', 'conv_b', 'dt_bias', 'a_log', 'd_full', 'norm', 'w_out', 'eye', 'expand']
    in_specs = [
        pl.BlockSpec((R, L, d), lambda b, i: (b, i, 0)),
        pl.BlockSpec((R, 6, d), blk),
        _const_spec((1, d)),
        pl.BlockSpec((R, CONV_W - 1, cdim), blk),
        pl.BlockSpec(st_blk, lambda b, i: (b, 0, 0, 0)),
    ] + [_const_spec(p[n].shape) for n in names]
    out_shape = (jax.ShapeDtypeStruct((bsz, t, d), F32),
                 jax.ShapeDtypeStruct((bsz, CONV_W - 1, cdim), F32),
                 jax.ShapeDtypeStruct(h0.shape, F32))
    out_specs = (pl.BlockSpec((R, L, d), lambda b, i: (b, i, 0)),
                 pl.BlockSpec((R, CONV_W - 1, cdim), blk),
                 pl.BlockSpec(st_blk, lambda b, i: (b, 0, 0, 0)))
    scratch = [pltpu.VMEM((R, 8 + L, cdim), F32), pltpu.VMEM((R, L, cdim), F32), pltpu.VMEM((R * L, d_in), F32),
               pltpu.VMEM((R, L, d_in), F32), pltpu.VMEM((R, L, d_in), BF16)]
    return pl.pallas_call(
        functools.partial(_ssd_kernel, R=R, L=L),
        out_shape=out_shape, grid=(bsz // R, t // L), in_specs=in_specs, out_specs=out_specs,
        scratch_shapes=scratch,
        compiler_params=pltpu.CompilerParams(
            dimension_semantics=("arbitrary", "arbitrary"), vmem_limit_bytes=VMEM_LIMIT),
        name="ssd_mixer",
    )(x, mod, nw, conv0, h0, *[p[n] for n in names])


def _ffn_kernel(x_ref, mod_ref, nw_ref, wg_ref, wu_ref, wd_ref, o_ref, *, nchunk):
    x = x_ref[0]
    mod = mod_ref[0]
    hn = (_rms(x, nw_ref[...]) * (1.0 + mod[4:5]) + mod[3:4]).astype(BF16)
    fc = wg_ref.shape[1] // nchunk
    acc = None
    for j in range(nchunk):
        g = _dot(hn, wg_ref[:, j * fc:(j + 1) * fc])
        u = _dot(hn, wu_ref[:, j * fc:(j + 1) * fc])
        part = _dot((_silu(g) * u).astype(BF16), wd_ref[j * fc:(j + 1) * fc, :])
        acc = part if acc is None else acc + part
    o_ref[0] = x + mod[5:6] * acc


def _ffn_layer(x, mod, nw, p, tm):
    bsz, t, d = x.shape
    tm = min(t, 2 * tm)
    return pl.pallas_call(
        functools.partial(_ffn_kernel, nchunk=p['w_gate'].shape[1] // (2 * LANES)),
        out_shape=jax.ShapeDtypeStruct(x.shape, F32),
        grid=(bsz, t // tm),
        in_specs=[pl.BlockSpec((1, tm, d), lambda b, i: (b, i, 0)),
                  pl.BlockSpec((1, 6, d), lambda b, i: (b, 0, 0)),
                  _const_spec((1, d)),
                  _const_spec(p['w_gate'].shape), _const_spec(p['w_up'].shape), _const_spec(p['w_down'].shape)],
        out_specs=pl.BlockSpec((1, tm, d), lambda b, i: (b, i, 0)),
        compiler_params=pltpu.CompilerParams(
            dimension_semantics=("arbitrary", "arbitrary"), vmem_limit_bytes=VMEM_LIMIT),
        name="ffn",
    )(x, mod, nw, p['w_gate'], p['w_up'], p['w_down'])


def _mlstm_kernel(x_ref, mod_ref, nw_ref, conv0_ref, c0_ref, n0_ref, m0_ref,
                  wxm_ref, wo_ref, cw_ref, cb_ref, wq_ref, wk_ref, wkt_ref, wv_ref, wgq_ref, wgk_ref, wgv_ref, bg_ref,
                  gnw_ref, skip_ref, wout_ref, eyei_ref, eyef_ref,
                  xo_ref, convo_ref, co_ref, no_ref, mo_ref,
                  xm_buf, xc_buf, q_buf, k_buf, kt_buf, v_buf, op_buf, hh_buf,
                  *, R, L):
    t = pl.program_id(1)
    hd_dim = wq_ref.shape[1]
    rep = hd_dim // LANES
    k_scale = hd_dim ** -0.5

    @pl.when(t == 0)
    def _():
        for r in range(R):
            xm_buf[r, TAIL0:8, :] = conv0_ref[r]
        co_ref[...] = c0_ref[...]
        no_ref[...] = n0_ref[...]
        mo_ref[...] = m0_ref[...]

    nw = nw_ref[...]
    hn = jnp.concatenate(
        [(_rms(x_ref[r], nw) * (1.0 + mod_ref[r][1:2]) + mod_ref[r][0:1]).astype(BF16) for r in range(R)], axis=0)
    xm = _dot(hn, wxm_ref[...])
    for r in range(R):
        xm_buf[r, 8:8 + L, :] = xm[r * L:(r + 1) * L]
    op_buf[...] = _dot(hn, wo_ref[...])
    for r in range(R):
        _causal_conv_silu(xm_buf.at[r], xc_buf.at[r], L, cw_ref, cb_ref)
        tail = xm_buf[r, L + TAIL0:L + 8, :]
        xm_buf[r, TAIL0:8, :] = tail
        convo_ref[r] = tail

    gates = bg_ref[...]
    for h in range(ML_HEADS):
        sl = slice(h * hd_dim, (h + 1) * hd_dim)
        xc_h = jnp.concatenate([xc_buf[r, :, sl].astype(BF16) for r in range(R)], axis=0)
        xm_h = jnp.concatenate([xm_buf[r, 8:8 + L, sl].astype(BF16) for r in range(R)], axis=0)
        q = _dot(xc_h, wq_ref[h])
        k = _dot(xc_h, wk_ref[h])
        v = _dot(xm_h, wv_ref[h])
        gates = gates + _dot(q.astype(BF16), wgq_ref[sl, :]) + _dot(k.astype(BF16), wgk_ref[sl, :]) \
            + _dot(v.astype(BF16), wgv_ref[sl, :])
        q_buf[:, sl] = q.astype(BF16)
        k_buf[:, sl] = (k * k_scale).astype(BF16)
        v_buf[:, sl] = v.astype(BF16)
        for r in range(R):
            kt_buf[r, h] = _dot_nt(wkt_ref[h], xc_h[r * L:(r + 1) * L]) * k_scale
    lf = jnp.minimum(gates, 0.0) - jnp.log1p(jnp.exp(-jnp.abs(gates)))
    tri = _tri_mask(L)
    tri_b = jnp.where(tri, 1.0, 0.0).astype(BF16)
    lane = lax.broadcasted_iota(jnp.int32, (1, LANES), 1)
    wide = lambda a: jnp.concatenate([a] * rep, axis=1)

    for r in range(R):
        rs = slice(r * L, (r + 1) * L)
        gi = gates[rs]
        bcum = _dot3_l(tri_b, lf[rs])
        li_t = _transpose_rows(gi, eyei_ref[...])
        b_t = _transpose_rows(bcum, eyef_ref[...])
        m_prev = mo_ref[r]
        m_next = m_prev
        for h in range(ML_HEADS):
            sl = slice(h * hd_dim, (h + 1) * hd_dim)
            bcol = jnp.broadcast_to(bcum[:, ML_HEADS + h:ML_HEADS + h + 1], (L, LANES))
            mp = jnp.broadcast_to(m_prev[:, h:h + 1], (1, LANES))
            logw_t = li_t[h:h + 1, :] - b_t[h:h + 1, :]
            dmat = jnp.where(tri, bcol[:, :L] + logw_t, -jnp.inf)
            a_inter = bcol + mp
            m_t = jnp.maximum(a_inter, jnp.max(dmat, axis=1, keepdims=True))
            qb = q_buf[rs, sl]
            vb = v_buf[rs, sl]
            s_mat = jnp.exp(dmat - m_t[:, :L]) * _dot_nt(qb, k_buf[rs, sl])
            w_inter = jnp.exp(a_inter - m_t)
            ch = co_ref[r, h]
            nh = no_ref[r, h:h + 1, :]
            num = _dot(s_mat.astype(BF16), vb) + wide(w_inter) * _dot(qb, ch.astype(BF16))
            qn = jnp.sum(qb.astype(F32) * nh, axis=1, keepdims=True)
            den = jnp.sum(s_mat, axis=1, keepdims=True) + w_inter * qn
            inv = 1.0 / jnp.maximum(jnp.abs(den), jnp.exp(-m_t))
            hh_buf[r, :, sl] = num * wide(inv)
            m_new = m_t[L - 1:L, :]
            b_last = bcol[L - 1:L, :]
            w_end = jnp.exp(b_last[:, :L] + logw_t - m_new[:, :L])
            w_old = wide(jnp.exp(b_last + mp - m_new))
            kw_t = (kt_buf[r, h] * w_end).astype(BF16)
            co_ref[r, h] = w_old * ch + _dot(kw_t, vb)
            w8 = jnp.broadcast_to(w_end, (8, L)).astype(BF16)
            no_ref[r, h:h + 1, :] = w_old * nh + _dot(w8, k_buf[rs, sl])[0:1, :]
            m_next = jnp.where(lane == h, m_new, m_next)
        mo_ref[r] = m_next

    gnw = gnw_ref[...]
    skip = skip_ref[...]
    rows = []
    for r in range(R):
        parts = []
        for h in range(ML_HEADS):
            sl = slice(h * hd_dim, (h + 1) * hd_dim)
            hn_h = _rms(hh_buf[r, :, sl], gnw[:, sl])
            parts.append(((hn_h + skip[:, sl] * xc_buf[r, :, sl])
                          * _sigmoid(op_buf[r * L:(r + 1) * L, sl])).astype(BF16))
        rows.append(jnp.concatenate(parts, axis=-1))
    out = _dot(jnp.concatenate(rows, axis=0), wout_ref[...])
    for r in range(R):
        xo_ref[r] = x_ref[r] + mod_ref[r][2:3] * out[r * L:(r + 1) * L]


def _mlstm_layer(x, mod, nw, conv0, c0, n0, m0, p, R, L):
    bsz, t, d = x.shape
    inner = p['conv_w'].shape[1]
    hd = inner // ML_HEADS
    blk = lambda b, i: (b, 0, 0)
    blk4 = lambda b, i: (b, 0, 0, 0)
    names = ['wxm', 'wo', 'conv_w', 'conv_b', 'w_q', 'w_k', 'w_kt', 'w_v', 'wgq', 'wgk', 'wgv', 'bg',
             'norm', 'skip', 'w_out', 'eye_i', 'eye_f']
    in_specs = [
        pl.BlockSpec((R, L, d), lambda b, i: (b, i, 0)),
        pl.BlockSpec((R, 6, d), blk),
        _const_spec((1, d)),
        pl.BlockSpec((R, CONV_W - 1, inner), blk),
        pl.BlockSpec((R, ML_HEADS, hd, hd), blk4, pipeline_mode=pl.Buffered(1)),
        pl.BlockSpec((R, ML_HEADS, hd), blk),
        pl.BlockSpec((R, 1, LANES), blk),
    ] + [_const_spec(p[n].shape) for n in names]
    out_shape = (jax.ShapeDtypeStruct((bsz, t, d), F32),
                 jax.ShapeDtypeStruct((bsz, CONV_W - 1, inner), F32),
                 jax.ShapeDtypeStruct((bsz, ML_HEADS, hd, hd), F32),
                 jax.ShapeDtypeStruct((bsz, ML_HEADS, hd), F32),
                 jax.ShapeDtypeStruct((bsz, 1, LANES), F32))
    out_specs = (pl.BlockSpec((R, L, d), lambda b, i: (b, i, 0)),
                 pl.BlockSpec((R, CONV_W - 1, inner), blk),
                 pl.BlockSpec((R, ML_HEADS, hd, hd), blk4),
                 pl.BlockSpec((R, ML_HEADS, hd), blk),
                 pl.BlockSpec((R, 1, LANES), blk))
    scratch = [pltpu.VMEM((R, 8 + L, inner), F32), pltpu.VMEM((R, L, inner), F32),
               pltpu.VMEM((R * L, inner), BF16), pltpu.VMEM((R * L, inner), BF16),
               pltpu.VMEM((R, ML_HEADS, hd, L), F32), pltpu.VMEM((R * L, inner), BF16),
               pltpu.VMEM((R * L, inner), F32), pltpu.VMEM((R, L, inner), F32)]
    return pl.pallas_call(
        functools.partial(_mlstm_kernel, R=R, L=L),
        out_shape=out_shape, grid=(bsz // R, t // L), in_specs=in_specs, out_specs=out_specs,
        scratch_shapes=scratch,
        compiler_params=pltpu.CompilerParams(
            dimension_semantics=("arbitrary", "arbitrary"), vmem_limit_bytes=VMEM_LIMIT),
        name="mlstm_mixer",
    )(x, mod, nw, conv0, c0, n0, m0, *[p[n] for n in names])


def _router_kernel(x_ref, mod_ref, nw_ref, wr_ref, br_ref, h_ref, info_ref, cnt_ref, *, tm):
    x = x_ref[0]
    mod = mod_ref[0]
    h = _rms(x, nw_ref[...]) * (1.0 + mod[4:5]) + mod[3:4]
    h_ref[0] = h
    h1, h2, _ = _split3(h)
    w1, w2, _ = _split3(wr_ref[...])
    logits = (_dot(h1, w1) + (_dot(h1, w2) + _dot(h2, w1))) + br_ref[...]
    lane = lax.broadcasted_iota(jnp.int32, (tm, LANES), 1)
    lg = jnp.where(lane < N_EXPERTS, logits, -jnp.inf)
    m1 = jnp.max(lg, axis=1, keepdims=True)
    i1 = jnp.min(jnp.where(lg == m1, lane, LANES), axis=1, keepdims=True)
    lg2 = jnp.where(lane == i1, -jnp.inf, lg)
    m2 = jnp.max(lg2, axis=1, keepdims=True)
    i2 = jnp.min(jnp.where(lg2 == m2, lane, LANES), axis=1, keepdims=True)
    e2 = jnp.exp(m2 - m1)
    w_top1 = 1.0 / (1.0 + e2)
    w_top2 = e2 / (1.0 + e2)
    sel = jnp.logical_or(lane == i1, lane == i2)
    mask = jnp.where(sel, 1.0, 0.0).astype(BF16)
    r = lax.broadcasted_iota(jnp.int32, (tm, tm), 0)
    c = lax.broadcasted_iota(jnp.int32, (tm, tm), 1)
    before = jnp.where(c < r, 1.0, 0.0).astype(BF16)
    rank = _dot(before, mask)
    r1 = jnp.sum(jnp.where(lane == i1, rank, 0.0), axis=1, keepdims=True)
    r2 = jnp.sum(jnp.where(lane == i2, rank, 0.0), axis=1, keepdims=True)
    cols = (i1.astype(F32), i2.astype(F32), r1, r2, w_top1, w_top2)
    info = jnp.zeros((tm, LANES), F32)
    for k, col in enumerate(cols):
        info = jnp.where(lane == k, col, info)
    info_ref[0] = info
    cnt = jnp.sum(mask.astype(F32), axis=0, keepdims=True)
    cnt_ref[0, 0] = jnp.broadcast_to(cnt, (8, LANES)).astype(jnp.int32)


def _router(x, mod, nw, p, tm):
    bsz, t, d = x.shape
    nt = t // tm
    ti = lambda b, i: (b, i, 0)
    t4 = lambda b, i: (b, i, 0, 0)
    out_shape = (jax.ShapeDtypeStruct((bsz, t, d), F32),
                 jax.ShapeDtypeStruct((bsz, t, LANES), F32),
                 jax.ShapeDtypeStruct((bsz, nt, 8, LANES), jnp.int32))
    out_specs = (pl.BlockSpec((1, tm, d), ti), pl.BlockSpec((1, tm, LANES), ti),
                 pl.BlockSpec((1, 1, 8, LANES), t4))
    return pl.pallas_call(
        functools.partial(_router_kernel, tm=tm),
        out_shape=out_shape, grid=(bsz, nt),
        in_specs=[pl.BlockSpec((1, tm, d), ti), pl.BlockSpec((1, 6, d), lambda b, i: (b, 0, 0)),
                  _const_spec((1, d)), _const_spec(p['w_router'].shape), _const_spec(p['b_router'].shape)],
        out_specs=out_specs,
        compiler_params=pltpu.CompilerParams(
            dimension_semantics=("arbitrary", "arbitrary"), vmem_limit_bytes=VMEM_LIMIT),
        name="router",
    )(x, mod, nw, p['w_router'], p['b_router'])


SLAB = 512
DMA_UNROLL = 8


def _dispatch_kernel(ids_ref, pos_ref, h_ref, xs_ref, zbuf, sem, *, tm):
    @pl.when(jnp.logical_and(pl.program_id(0) == 0, pl.program_id(1) == 0))
    def _():
        zbuf[...] = jnp.zeros(zbuf.shape, F32)
        for i in range(2 * N_EXPERTS):
            c = pltpu.make_async_copy(zbuf, xs_ref.at[pl.ds(ids_ref[i] * SLAB, SLAB), :], sem)
            c.start()
            c.wait()

    def send(t, carry):
        for k in range(2):
            pltpu.make_async_copy(h_ref.at[0, pl.ds(t, 1), :],
                                  xs_ref.at[pl.ds(pos_ref[0, 0, k * tm + t], 1), :], sem).start(priority=k)
        return carry

    lax.fori_loop(0, tm, send, 0, unroll=DMA_UNROLL)
    for k in range(2):
        pltpu.make_async_copy(h_ref.at[0], xs_ref.at[pl.ds(0, tm), :], sem).wait()


def _dispatch(h, pos, slab_ids, n_rows, tm):
    bsz, t, d = h.shape
    nt = t // tm
    return pl.pallas_call(
        functools.partial(_dispatch_kernel, tm=tm),
        out_shape=jax.ShapeDtypeStruct((n_rows, d), F32),
        grid_spec=pltpu.PrefetchScalarGridSpec(
            num_scalar_prefetch=1, grid=(bsz, nt),
            in_specs=[pl.BlockSpec((1, 1, 2 * tm), lambda b, i, ids: (b * nt + i, 0, 0), memory_space=pltpu.SMEM),
                      pl.BlockSpec((1, tm, d), lambda b, i, ids: (b, i, 0))],
            out_specs=pl.BlockSpec(memory_space=pl.ANY),
            scratch_shapes=[pltpu.VMEM((SLAB, d), F32), pltpu.SemaphoreType.DMA(())]),
        compiler_params=pltpu.CompilerParams(
            dimension_semantics=("arbitrary", "arbitrary"), vmem_limit_bytes=VMEM_LIMIT),
        name="dispatch",
    )(slab_ids, pos, h)


def _slab_ffn_kernel(se_ref, nu_ref, x_ref, wg_ref, wu_ref, wd_ref, y_ref, *, nchunk):
    s = pl.program_id(0)

    @pl.when(s < nu_ref[0])
    def _():
        xb = x_ref[...].astype(BF16)
        fc = wg_ref.shape[2] // nchunk
        acc = None
        for j in range(nchunk):
            g = _dot(xb, wg_ref[0, :, j * fc:(j + 1) * fc])
            u = _dot(xb, wu_ref[0, :, j * fc:(j + 1) * fc])
            part = _dot((_silu(g) * u).astype(BF16), wd_ref[0, j * fc:(j + 1) * fc, :])
            acc = part if acc is None else acc + part
        y_ref[...] = acc

    @pl.when(s >= nu_ref[0])
    def _():
        y_ref[...] = jnp.zeros(y_ref.shape, F32)


def _slab_ffn(xs, slab_expert, n_used, wg, wu, wd):
    n_rows, d = xs.shape
    f = wg.shape[2]
    w_idx = lambda s, se, nu: (se[s], 0, 0)
    grid_spec = pltpu.PrefetchScalarGridSpec(
        num_scalar_prefetch=2, grid=(n_rows // SLAB,),
        in_specs=[pl.BlockSpec((SLAB, d), lambda s, se, nu: (s, 0)),
                  pl.BlockSpec((1, d, f), w_idx), pl.BlockSpec((1, d, f), w_idx), pl.BlockSpec((1, f, d), w_idx)],
        out_specs=pl.BlockSpec((SLAB, d), lambda s, se, nu: (s, 0)))
    return pl.pallas_call(
        functools.partial(_slab_ffn_kernel, nchunk=f // (2 * LANES)),
        out_shape=jax.ShapeDtypeStruct((n_rows, d), F32),
        grid_spec=grid_spec,
        compiler_params=pltpu.CompilerParams(dimension_semantics=("arbitrary",), vmem_limit_bytes=VMEM_LIMIT),
        name="slab_ffn",
    )(slab_expert, n_used, xs, wg, wu, wd)


def _combine_kernel(pos_ref, pos_next_ref, x_ref, info_ref, mod_ref, fw_ref, ys_ref, o_ref, buf, sem, *, tm):
    step = pl.program_id(0) * pl.num_programs(1) + pl.program_id(1)
    last = pl.num_programs(0) * pl.num_programs(1) - 1
    slot = lax.rem(step, 2)

    def request(p_ref, sl):
        def fetch(t, carry):
            for k in range(2):
                pltpu.make_async_copy(ys_ref.at[pl.ds(p_ref[0, 0, k * tm + t], 1), :],
                                      buf.at[sl, k, pl.ds(t, 1), :], sem.at[sl]).start(priority=k)
            return carry
        lax.fori_loop(0, tm, fetch, 0, unroll=DMA_UNROLL)

    @pl.when(step == 0)
    def _():
        request(pos_ref, 0)

    @pl.when(step < last)
    def _():
        request(pos_next_ref, 1 - slot)

    for k in range(2):
        pltpu.make_async_copy(ys_ref.at[pl.ds(0, tm), :], buf.at[slot, k], sem.at[slot]).wait()
    info = info_ref[0]
    y = x_ref[0] + mod_ref[0][5:6] * (info[:, 4:5] * buf[slot, 0] + info[:, 5:6] * buf[slot, 1])
    o_ref[0] = _rms(y, fw_ref[...])


def _combine(x, info, mod, fw, ys, pos, tm):
    bsz, t, d = x.shape
    nt = t // tm
    ti = lambda b, i: (b, i, 0)
    tiles = bsz * nt
    return pl.pallas_call(
        functools.partial(_combine_kernel, tm=tm),
        out_shape=jax.ShapeDtypeStruct(x.shape, F32),
        grid=(bsz, nt),
        in_specs=[pl.BlockSpec((1, 1, 2 * tm), lambda b, i: (b * nt + i, 0, 0), memory_space=pltpu.SMEM),
                  pl.BlockSpec((1, 1, 2 * tm), lambda b, i: (jnp.minimum(b * nt + i + 1, tiles - 1), 0, 0),
                               memory_space=pltpu.SMEM),
                  pl.BlockSpec((1, tm, d), ti), pl.BlockSpec((1, tm, LANES), ti),
                  pl.BlockSpec((1, 6, d), lambda b, i: (b, 0, 0)), _const_spec(fw.shape),
                  pl.BlockSpec(memory_space=pl.ANY)],
        out_specs=pl.BlockSpec((1, tm, d), ti),
        scratch_shapes=[pltpu.VMEM((2, 2, tm, d), F32), pltpu.SemaphoreType.DMA((2,))],
        compiler_params=pltpu.CompilerParams(
            dimension_semantics=("arbitrary", "arbitrary"), vmem_limit_bytes=VMEM_LIMIT),
        name="combine",
    )(pos, pos, x, info, mod, fw, ys)


def _moe_layer(x, mod, nw, p, fw, tm):
    bsz, t, _ = x.shape
    n_tok = bsz * t
    tiles = n_tok // tm
    h, info, cnt = _router(x, mod, nw, p, tm)
    counts = cnt[:, :, 0, :N_EXPERTS].reshape(tiles, N_EXPERTS)
    base = jnp.cumsum(counts, axis=0) - counts
    slabs = (jnp.sum(counts, axis=0) + (SLAB - 1)) // SLAB
    slab_end = jnp.cumsum(slabs)
    start = (slab_end - slabs) * SLAB
    n_slabs = (2 * n_tok) // SLAB + N_EXPERTS
    experts = jnp.arange(N_EXPERTS, dtype=jnp.int32)
    slab_expert = jnp.minimum(jnp.sum(jnp.arange(n_slabs, dtype=jnp.int32)[:, None] >= slab_end[None, :], axis=1),
                              N_EXPERTS - 1).astype(jnp.int32)
    inf2 = info.reshape(tiles, tm, LANES)
    offs = start[None, :] + base
    pos = []
    for k in range(2):
        e_k = inf2[:, :, k].astype(jnp.int32)
        off_k = sum(jnp.where(e_k == e, offs[:, e:e + 1], 0) for e in range(N_EXPERTS))
        pos.append(off_k + inf2[:, :, 2 + k].astype(jnp.int32))
    td = min(t, 4 * tm)
    pos_d = jnp.concatenate([q.reshape(n_tok // td, td) for q in pos], axis=1)
    pos_d = pos_d.reshape(n_tok // td, 1, 2 * td).astype(jnp.int32)
    pos = jnp.concatenate(pos, axis=1).reshape(tiles, 1, 2 * tm).astype(jnp.int32)
    partial = jnp.concatenate([jnp.maximum(slab_end - 1, 0), jnp.minimum(slab_end[-1] + experts, n_slabs - 1)])
    xs = _dispatch(h, pos_d, partial.astype(jnp.int32), n_slabs * SLAB, td)
    ys = _slab_ffn(xs, slab_expert, slab_end[-1:].astype(jnp.int32), p['w_gate'], p['w_up'], p['w_down'])
    return _combine(x, info, mod, fw, ys, pos, tm)


def _cast_kernel(x_ref, o_ref):
    o_ref[...] = x_ref[...].astype(BF16)


def _to_bf16(a, tr=512):
    a2 = a.reshape(-1, a.shape[-1])
    rows, cols = a2.shape
    assert rows % tr == 0
    out = pl.pallas_call(
        _cast_kernel,
        out_shape=jax.ShapeDtypeStruct(a2.shape, BF16),
        grid=(rows // tr,),
        in_specs=[pl.BlockSpec((tr, cols), lambda i: (i, 0))],
        out_specs=pl.BlockSpec((tr, cols), lambda i: (i, 0)),
        compiler_params=pltpu.CompilerParams(
            dimension_semantics=("arbitrary",), vmem_limit_bytes=VMEM_LIMIT),
        name="to_bf16",
    )(a2)
    return out.reshape(a.shape)


def _pad_lanes(a):
    return jnp.pad(a, [(0, 0)] * (a.ndim - 1) + [(0, LANES - a.shape[-1])])


def _one_hot_rows(n, offset):
    r = lax.broadcasted_iota(jnp.int32, (n, LANES), 0)
    c = lax.broadcasted_iota(jnp.int32, (n, LANES), 1)
    return (c == r + offset).astype(BF16)


def _prep_ssm(w_in, conv_w, conv_b, dt_bias, a_log, d_skip, norm_w, w_out):
    d_in = SSM_GROUPS * SSM_HPG * SSM_HEAD_DIM
    cdim = conv_w.shape[1]
    heads = SSM_GROUPS * SSM_HPG
    r = lax.broadcasted_iota(jnp.int32, (LANES, d_in), 0)
    c = lax.broadcasted_iota(jnp.int32, (LANES, d_in), 1)
    return dict(
        wz=w_in[:, :d_in].astype(BF16), wx=w_in[:, d_in:d_in + cdim].astype(BF16),
        wdt=_pad_lanes(w_in[:, d_in + cdim:]).astype(BF16),
        conv_w=conv_w, conv_b=conv_b.reshape(1, cdim),
        dt_bias=_pad_lanes(dt_bias.reshape(1, heads)), a_log=_pad_lanes(a_log.reshape(1, heads)),
        d_full=jnp.repeat(d_skip, SSM_HEAD_DIM).reshape(1, d_in),
        norm=norm_w.reshape(1, d_in), w_out=w_out.astype(BF16),
        eye=_one_hot_rows(heads, 0), expand=(c // SSM_HEAD_DIM == r).astype(BF16))


def _prep_mlstm(w_in, conv_w, conv_b, w_q, w_k, w_v, w_ig, b_ig, w_fg, b_fg, norm_w, skip, w_out):
    inner = conv_w.shape[1]
    hd = inner // ML_HEADS
    wg = jnp.concatenate([w_ig, w_fg], axis=-1).reshape(ML_HEADS, 3, hd, 2 * ML_HEADS)
    part = lambda j: _pad_lanes(wg[:, j].reshape(inner, 2 * ML_HEADS)).astype(BF16)
    return dict(
        wxm=w_in[:, :inner].astype(BF16), wo=w_in[:, inner:].astype(BF16),
        conv_w=conv_w, conv_b=conv_b.reshape(1, inner),
        w_q=w_q.astype(BF16), w_k=w_k.astype(BF16), w_kt=jnp.swapaxes(w_k, 1, 2).astype(BF16),
        w_v=w_v.astype(BF16),
        wgq=part(0), wgk=part(1), wgv=part(2),
        bg=_pad_lanes(jnp.concatenate([b_ig, b_fg]).reshape(1, 2 * ML_HEADS)),
        norm=norm_w.reshape(1, inner), skip=skip.reshape(1, inner), w_out=w_out.astype(BF16),
        eye_i=_one_hot_rows(ML_HEADS, 0), eye_f=_one_hot_rows(ML_HEADS, ML_HEADS))


def _trunk(x, mod, ssm_conv, ssm_state, ml_conv, ml_c, ml_n, ml_m, p):
    bsz, t, d = x.shape
    L = SCAN_CHUNK if t % SCAN_CHUNK == 0 else t
    R = 2 if bsz % 2 == 0 else 1
    tm = min(t, 512)
    n_heads = SSM_GROUPS * SSM_HPG
    h0 = ssm_state.reshape(bsz, SSM_GROUPS, SSM_HPG, SSM_HEAD_DIM, SSM_STATE)
    h0 = h0.transpose(0, 1, 4, 2, 3).reshape(bsz, SSM_GROUPS, SSM_STATE, SSM_HPG * SSM_HEAD_DIM)
    x, conv_s, h_s = _ssd_layer(x, mod[0], p['norm_mix'][0], ssm_conv, h0, p['ssm'], R, L)
    h_s = h_s.reshape(bsz, SSM_GROUPS, SSM_STATE, SSM_HPG, SSM_HEAD_DIM).transpose(0, 1, 3, 4, 2)
    h_s = h_s.reshape(bsz, n_heads, SSM_HEAD_DIM, SSM_STATE)
    x = _ffn_layer(x, mod[0], p['norm_ffn'][0], p['ffn'], tm)
    m0 = _pad_lanes(ml_m).reshape(bsz, 1, LANES)
    x, conv_m, c_m, n_m, m_m = _mlstm_layer(x, mod[1], p['norm_mix'][1], ml_conv, ml_c, ml_n, m0, p['ml'], R, L)
    y = _moe_layer(x, mod[1], p['norm_ffn'][1], p['moe'], p['norm_final'], tm)
    return (y, conv_s[None], h_s[None], conv_m[None], c_m[None], n_m[None], m_m[:, 0, :ML_HEADS][None])


def kernel(x_prompt, x_sample, c_prompt, c_sample, state_ssm_conv, state_ssm, state_mlstm_conv, state_mlstm_C, state_mlstm_n, state_mlstm_m, w_ada, b_ada, norm_mix, norm_ffn, norm_final, ssm_w_in, ssm_conv_w, ssm_conv_b, ssm_dt_bias, ssm_a_log, ssm_d, ssm_norm, ssm_w_out, ml_w_in, ml_conv_w, ml_conv_b, ml_w_q, ml_w_k, ml_w_v, ml_w_igate, ml_b_igate, ml_w_fgate, ml_b_fgate, ml_norm, ml_skip, ml_w_out, ffn_w_gate, ffn_w_up, ffn_w_down, moe_w_router, moe_b_router, moe_w_gate, moe_w_up, moe_w_down):
    depth, d, _ = w_ada.shape
    assert depth == 2 and state_ssm.shape[0] == 1 and state_mlstm_C.shape[0] == 1
    bp, bs = x_prompt.shape[0], x_sample.shape[0]
    p = dict(
        norm_mix=norm_mix.reshape(depth, 1, d), norm_ffn=norm_ffn.reshape(depth, 1, d),
        norm_final=norm_final.reshape(1, d),
        ssm=_prep_ssm(ssm_w_in[0], ssm_conv_w[0], ssm_conv_b[0], ssm_dt_bias[0], ssm_a_log[0], ssm_d[0],
                      ssm_norm[0], ssm_w_out[0]),
        ml=_prep_mlstm(ml_w_in[0], ml_conv_w[0], ml_conv_b[0], ml_w_q[0], ml_w_k[0], ml_w_v[0], ml_w_igate[0],
                       ml_b_igate[0], ml_w_fgate[0], ml_b_fgate[0], ml_norm[0], ml_skip[0], ml_w_out[0]),
        ffn=dict(w_gate=ffn_w_gate[0].astype(BF16), w_up=ffn_w_up[0].astype(BF16),
                 w_down=ffn_w_down[0].astype(BF16)),
        moe=dict(w_router=_pad_lanes(moe_w_router[0]), b_router=_pad_lanes(moe_b_router[0].reshape(1, -1)),
                 w_gate=_to_bf16(moe_w_gate[0]), w_up=_to_bf16(moe_w_up[0]), w_down=_to_bf16(moe_w_down[0])))
    mod = _ada(jnp.concatenate([c_prompt, c_sample], axis=0), w_ada, b_ada)
    mod = mod.reshape(depth, bp + bs, 6, d)

    f = F32
    zeros = lambda a, b: jnp.zeros((b,) + a.shape[2:], f)
    out_p = _trunk(x_prompt, mod[:, :bp], zeros(state_ssm_conv, bp), zeros(state_ssm, bp),
                   zeros(state_mlstm_conv, bp), zeros(state_mlstm_C, bp), zeros(state_mlstm_n, bp),
                   zeros(state_mlstm_m, bp), p)
    out_s = _trunk(x_sample, mod[:, bp:], state_ssm_conv[0], state_ssm[0], state_mlstm_conv[0],
                   state_mlstm_C[0], state_mlstm_n[0], state_mlstm_m[0], p)
    return (out_p[0], out_s[0]) + tuple(out_p[1:]) + tuple(out_s[1:])
```

```python
import functools

import jax
import jax.numpy as jnp
from jax import lax
from jax.experimental import pallas as pl
from jax.experimental.pallas import tpu as pltpu

F32 = jnp.float32
BF16 = jnp.bfloat16
RMS_EPS = 1e-6
CONV_W = 4
LANES = 128
SCAN_CHUNK = LANES
TAIL0 = 8 - (CONV_W - 1)
VMEM_LIMIT = 60 * 1024 * 1024

SSM_GROUPS = 4
SSM_HPG = 8
SSM_HEAD_DIM = 64
SSM_STATE = 128
ML_HEADS = 8
N_EXPERTS = 8


def _dot(a, b):
    return jnp.dot(a, b, preferred_element_type=F32)


def _dot_nt(a, b):
    return lax.dot_general(a, b, (((1,), (1,)), ((), ())), preferred_element_type=F32)


def _dot_tn(a, b):
    return lax.dot_general(a, b, (((0,), (0,)), ((), ())), preferred_element_type=F32)


def _split3(x):
    h1 = x.astype(BF16)
    r = x - h1.astype(F32)
    h2 = r.astype(BF16)
    r = r - h2.astype(F32)
    return h1, h2, r.astype(BF16)


def _dot3_l(sel, x):
    return sum(_dot(sel, p) for p in _split3(x))


def _dot3_r(x, sel):
    return sum(_dot(p, sel) for p in _split3(x))


def _transpose_rows(x, eye):
    return sum(_dot_nt(eye, p) for p in _split3(x))


def _sigmoid(x):
    return 1.0 / (1.0 + jnp.exp(-x))


def _silu(x):
    return x * _sigmoid(x)


def _softplus(x):
    return jnp.maximum(x, 0.0) + jnp.log1p(jnp.exp(-jnp.abs(x)))


def _rms(x, g):
    return x * lax.rsqrt(jnp.mean(x * x, axis=-1, keepdims=True) + RMS_EPS) * g


def _tri_mask(n):
    r = lax.broadcasted_iota(jnp.int32, (n, n), 0)
    c = lax.broadcasted_iota(jnp.int32, (n, n), 1)
    return r >= c


def _causal_conv_silu(buf, dst, tt, cw_ref, cb_ref):
    xn = buf[8:8 + tt, :]
    x1 = pltpu.roll(xn, 1, axis=0)
    w = [cw_ref[k:k + 1, :] for k in range(CONV_W)]
    y = cb_ref[...] + xn * w[3] + x1 * w[2] + pltpu.roll(xn * w[1] + x1 * w[0], 2, axis=0)
    dst[...] = _silu(y)
    head = cb_ref[...]
    for k in range(CONV_W):
        head = head + buf[TAIL0 + k:TAIL0 + k + 8, :] * cw_ref[k:k + 1, :]
    dst[0:8, :] = _silu(head)


def _ada_kernel(c_ref, w_ref, b_ref, o_ref):
    ca = _silu(c_ref[...]).astype(BF16)
    o_ref[0] = _dot(ca, w_ref[0].astype(BF16)) + b_ref[0]


def _ada(c_all, w_ada, b_ada):
    depth, d, n = w_ada.shape
    bt = c_all.shape[0]
    tn = n // 4
    return pl.pallas_call(
        _ada_kernel,
        out_shape=jax.ShapeDtypeStruct((depth, bt, n), F32),
        grid=(depth, n // tn),
        in_specs=[pl.BlockSpec((bt, d), lambda i, j: (0, 0)),
                  pl.BlockSpec((1, d, tn), lambda i, j: (i, 0, j)),
                  pl.BlockSpec((1, 1, tn), lambda i, j: (i, 0, j))],
        out_specs=pl.BlockSpec((1, bt, tn), lambda i, j: (i, 0, j)),
        compiler_params=pltpu.CompilerParams(
            dimension_semantics=("arbitrary", "arbitrary"), vmem_limit_bytes=VMEM_LIMIT),
        name="ada",
    )(c_all, w_ada, b_ada.reshape(depth, 1, n))


def _const_spec(shape):
    nd = len(shape)
    return pl.BlockSpec(shape, lambda *_: (0,) * nd, pipeline_mode=pl.Buffered(1))


def _ssd_kernel(x_ref, mod_ref, nw_ref, conv0_ref, h0_ref, wz_ref, wx_ref, wdt_ref, cw_ref, cb_ref,
                dtb_ref, alog_ref, dfull_ref, gnw_ref, wout_ref, eye_ref, expand_ref,
                xo_ref, convo_ref, ho_ref,
                xbc_buf, act_buf, z_buf, y_buf, xw_buf, *, R, L):
    t = pl.program_id(1)
    inner = SSM_HPG * SSM_HEAD_DIM
    d_in = SSM_GROUPS * inner
    gn = SSM_GROUPS * SSM_STATE

    @pl.when(t == 0)
    def _():
        for r in range(R):
            xbc_buf[r, TAIL0:8, :] = conv0_ref[r]
        ho_ref[...] = h0_ref[...]

    nw = nw_ref[...]
    hn = jnp.concatenate(
        [(_rms(x_ref[r], nw) * (1.0 + mod_ref[r][1:2]) + mod_ref[r][0:1]).astype(BF16) for r in range(R)], axis=0)
    z_buf[...] = _dot(hn, wz_ref[...])
    xbc = _dot(hn, wx_ref[...])
    dt = _softplus(_dot(hn, wdt_ref[...]) + dtb_ref[...])
    a = -jnp.exp(alog_ref[...])
    tri = _tri_mask(L)
    tri_b = jnp.where(tri, 1.0, 0.0).astype(BF16)
    eye = eye_ref[...]
    expand = expand_ref[...]
    left = lax.broadcasted_iota(jnp.int32, (1, LANES), 1) < SSM_HEAD_DIM

    for r in range(R):
        xbc_buf[r, 8:8 + L, :] = xbc[r * L:(r + 1) * L]
        _causal_conv_silu(xbc_buf.at[r], act_buf.at[r], L, cw_ref, cb_ref)
        tail = xbc_buf[r, L + TAIL0:L + 8, :]
        xbc_buf[r, TAIL0:8, :] = tail
        convo_ref[r] = tail

    for r in range(R):
        dtc = dt[r * L:(r + 1) * L]
        cum = _dot3_l(tri_b, dtc * a)
        cum_t = _transpose_rows(cum, eye)
        dt_t = _transpose_rows(dtc, eye)
        cum_last = cum[L - 1:L, :]
        wend = (jnp.exp(cum_last - cum) * dtc).astype(BF16)
        xw_buf[r] = (act_buf[r, :, :d_in] * _dot(wend, expand)).astype(BF16)
        dec_last = jnp.exp(_dot3_r(cum[L - 8:L, :], expand)[7:8, :])
        for g in range(SSM_GROUPS):
            bg = act_buf[r, :, d_in + g * SSM_STATE:d_in + (g + 1) * SSM_STATE].astype(BF16)
            cg = act_buf[r, :, d_in + gn + g * SSM_STATE:d_in + gn + (g + 1) * SSM_STATE].astype(BF16)
            cb = _dot_nt(cg, bg)
            hg = ho_ref[r, g]
            y_int = _dot(cg, hg.astype(BF16))
            for pr in range(SSM_HPG // 2):
                hd0 = g * SSM_HPG + 2 * pr
                lo = hd0 * SSM_HEAD_DIM
                xp = act_buf[r, :, lo:lo + LANES]
                x2 = jnp.concatenate([jnp.where(left, xp, 0.0), jnp.where(left, 0.0, xp)], axis=0).astype(BF16)
                wms, es = [], []
                for hd in (hd0, hd0 + 1):
                    ccol = jnp.broadcast_to(cum[:, hd:hd + 1], (L, LANES))
                    dec = jnp.exp(jnp.where(tri, ccol[:, :L] - cum_t[hd:hd + 1, :], -jnp.inf))
                    wms.append((cb * dec * dt_t[hd:hd + 1, :]).astype(BF16))
                    es.append(jnp.exp(ccol))
                yi = y_int[:, pr * LANES:(pr + 1) * LANES]
                y_buf[r, :, lo:lo + LANES] = (_dot(jnp.concatenate(wms, axis=1), x2)
                                              + yi * jnp.where(left, es[0], es[1]))
            ho_ref[r, g] = (hg * dec_last[:, g * inner:(g + 1) * inner]
                           + _dot_tn(bg, xw_buf[r, :, g * inner:(g + 1) * inner]))

    gnw = gnw_ref[...]
    rows = []
    for r in range(R):
        y = y_buf[r] + dfull_ref[...] * act_buf[r, :, :d_in]
        y = y * _silu(z_buf[r * L:(r + 1) * L, :])
        rows.append(jnp.concatenate(
            [_rms(y[:, g * inner:(g + 1) * inner], gnw[:, g * inner:(g + 1) * inner]).astype(BF16)
             for g in range(SSM_GROUPS)], axis=-1))
    out = _dot(jnp.concatenate(rows, axis=0), wout_ref[...])
    for r in range(R):
        xo_ref[r] = x_ref[r] + mod_ref[r][2:3] * out[r * L:(r + 1) * L]


def _ssd_layer(x, mod, nw, conv0, h0, p, R, L):
    bsz, t, d = x.shape
    d_in = SSM_GROUPS * SSM_HPG * SSM_HEAD_DIM
    cdim = p['conv_w'].shape[1]
    blk = lambda b, i: (b, 0, 0)
    st_blk = (R, SSM_GROUPS, SSM_STATE, SSM_HPG * SSM_HEAD_DIM)
    names = ['wz', 'wx', 'wdt', 'conv_w', 'conv_b', 'dt_bias', 'a_log', 'd_full', 'norm', 'w_out', 'eye', 'expand']
    in_specs = [
        pl.BlockSpec((R, L, d), lambda b, i: (b, i, 0)),
        pl.BlockSpec((R, 6, d), blk),
        _const_spec((1, d)),
        pl.BlockSpec((R, CONV_W - 1, cdim), blk),
        pl.BlockSpec(st_blk, lambda b, i: (b, 0, 0, 0)),
    ] + [_const_spec(p[n].shape) for n in names]
    out_shape = (jax.ShapeDtypeStruct((bsz, t, d), F32),
                 jax.ShapeDtypeStruct((bsz, CONV_W - 1, cdim), F32),
                 jax.ShapeDtypeStruct(h0.shape, F32))
    out_specs = (pl.BlockSpec((R, L, d), lambda b, i: (b, i, 0)),
                 pl.BlockSpec((R, CONV_W - 1, cdim), blk),
                 pl.BlockSpec(st_blk, lambda b, i: (b, 0, 0, 0)))
    scratch = [pltpu.VMEM((R, 8 + L, cdim), F32), pltpu.VMEM((R, L, cdim), F32), pltpu.VMEM((R * L, d_in), F32),
               pltpu.VMEM((R, L, d_in), F32), pltpu.VMEM((R, L, d_in), BF16)]
    return pl.pallas_call(
        functools.partial(_ssd_kernel, R=R, L=L),
        out_shape=out_shape, grid=(bsz // R, t // L), in_specs=in_specs, out_specs=out_specs,
        scratch_shapes=scratch,
        compiler_params=pltpu.CompilerParams(
            dimension_semantics=("arbitrary", "arbitrary"), vmem_limit_bytes=VMEM_LIMIT),
        name="ssd_mixer",
    )(x, mod, nw, conv0, h0, *[p[n] for n in names])


def _ffn_kernel(x_ref, mod_ref, nw_ref, wg_ref, wu_ref, wd_ref, o_ref, *, nchunk):
    x = x_ref[0]
    mod = mod_ref[0]
    hn = (_rms(x, nw_ref[...]) * (1.0 + mod[4:5]) + mod[3:4]).astype(BF16)
    fc = wg_ref.shape[1] // nchunk
    acc = None
    for j in range(nchunk):
        g = _dot(hn, wg_ref[:, j * fc:(j + 1) * fc])
        u = _dot(hn, wu_ref[:, j * fc:(j + 1) * fc])
        part = _dot((_silu(g) * u).astype(BF16), wd_ref[j * fc:(j + 1) * fc, :])
        acc = part if acc is None else acc + part
    o_ref[0] = x + mod[5:6] * acc


def _ffn_layer(x, mod, nw, p, tm):
    bsz, t, d = x.shape
    tm = min(t, 2 * tm)
    return pl.pallas_call(
        functools.partial(_ffn_kernel, nchunk=p['w_gate'].shape[1] // (2 * LANES)),
        out_shape=jax.ShapeDtypeStruct(x.shape, F32),
        grid=(bsz, t // tm),
        in_specs=[pl.BlockSpec((1, tm, d), lambda b, i: (b, i, 0)),
                  pl.BlockSpec((1, 6, d), lambda b, i: (b, 0, 0)),
                  _const_spec((1, d)),
                  _const_spec(p['w_gate'].shape), _const_spec(p['w_up'].shape), _const_spec(p['w_down'].shape)],
        out_specs=pl.BlockSpec((1, tm, d), lambda b, i: (b, i, 0)),
        compiler_params=pltpu.CompilerParams(
            dimension_semantics=("arbitrary", "arbitrary"), vmem_limit_bytes=VMEM_LIMIT),
        name="ffn",
    )(x, mod, nw, p['w_gate'], p['w_up'], p['w_down'])


def _mlstm_kernel(x_ref, mod_ref, nw_ref, conv0_ref, c0_ref, n0_ref, m0_ref,
                  wxm_ref, wo_ref, cw_ref, cb_ref, wq_ref, wk_ref, wkt_ref, wv_ref, wgq_ref, wgk_ref, wgv_ref, bg_ref,
                  gnw_ref, skip_ref, wout_ref, eyei_ref, eyef_ref,
                  xo_ref, convo_ref, co_ref, no_ref, mo_ref,
                  xm_buf, xc_buf, q_buf, k_buf, kt_buf, v_buf, op_buf, hh_buf,
                  *, R, L):
    t = pl.program_id(1)
    hd_dim = wq_ref.shape[1]
    rep = hd_dim // LANES
    k_scale = hd_dim ** -0.5

    @pl.when(t == 0)
    def _():
        for r in range(R):
            xm_buf[r, TAIL0:8, :] = conv0_ref[r]
        co_ref[...] = c0_ref[...]
        no_ref[...] = n0_ref[...]
        mo_ref[...] = m0_ref[...]

    nw = nw_ref[...]
    hn = jnp.concatenate(
        [(_rms(x_ref[r], nw) * (1.0 + mod_ref[r][1:2]) + mod_ref[r][0:1]).astype(BF16) for r in range(R)], axis=0)
    xm = _dot(hn, wxm_ref[...])
    for r in range(R):
        xm_buf[r, 8:8 + L, :] = xm[r * L:(r + 1) * L]
    op_buf[...] = _dot(hn, wo_ref[...])
    for r in range(R):
        _causal_conv_silu(xm_buf.at[r], xc_buf.at[r], L, cw_ref, cb_ref)
        tail = xm_buf[r, L + TAIL0:L + 8, :]
        xm_buf[r, TAIL0:8, :] = tail
        convo_ref[r] = tail

    gates = bg_ref[...]
    for h in range(ML_HEADS):
        sl = slice(h * hd_dim, (h + 1) * hd_dim)
        xc_h = jnp.concatenate([xc_buf[r, :, sl].astype(BF16) for r in range(R)], axis=0)
        xm_h = jnp.concatenate([xm_buf[r, 8:8 + L, sl].astype(BF16) for r in range(R)], axis=0)
        q = _dot(xc_h, wq_ref[h])
        k = _dot(xc_h, wk_ref[h])
        v = _dot(xm_h, wv_ref[h])
        gates = gates + _dot(q.astype(BF16), wgq_ref[sl, :]) + _dot(k.astype(BF16), wgk_ref[sl, :]) \
            + _dot(v.astype(BF16), wgv_ref[sl, :])
        q_buf[:, sl] = q.astype(BF16)
        k_buf[:, sl] = (k * k_scale).astype(BF16)
        v_buf[:, sl] = v.astype(BF16)
        for r in range(R):
            if L % LANES == 0:
                kt_buf[r, h] = (k[r * L:(r + 1) * L] * k_scale).T
            else:
                kt_buf[r, h] = _dot_nt(wkt_ref[h], xc_h[r * L:(r + 1) * L]) * k_scale
    lf = jnp.minimum(gates, 0.0) - jnp.log1p(jnp.exp(-jnp.abs(gates)))
    tri = _tri_mask(L)
    tri_b = jnp.where(tri, 1.0, 0.0).astype(BF16)
    lane = lax.broadcasted_iota(jnp.int32, (1, LANES), 1)
    wide = lambda a: jnp.concatenate([a] * rep, axis=1)

    for r in range(R):
        rs = slice(r * L, (r + 1) * L)
        gi = gates[rs]
        bcum = _dot3_l(tri_b, lf[rs])
        li_t = _transpose_rows(gi, eyei_ref[...])
        b_t = _transpose_rows(bcum, eyef_ref[...])
        m_prev = mo_ref[r]
        m_next = m_prev
        for h in range(ML_HEADS):
            sl = slice(h * hd_dim, (h + 1) * hd_dim)
            bcol = jnp.broadcast_to(bcum[:, ML_HEADS + h:ML_HEADS + h + 1], (L, LANES))
            mp = jnp.broadcast_to(m_prev[:, h:h + 1], (1, LANES))
            logw_t = li_t[h:h + 1, :] - b_t[h:h + 1, :]
            dmat = jnp.where(tri, bcol[:, :L] + logw_t, -jnp.inf)
            a_inter = bcol + mp
            m_t = jnp.maximum(a_inter, jnp.max(dmat, axis=1, keepdims=True))
            qb = q_buf[rs, sl]
            vb = v_buf[rs, sl]
            s_mat = jnp.exp(dmat - m_t[:, :L]) * _dot_nt(qb, k_buf[rs, sl])
            w_inter = jnp.exp(a_inter - m_t)
            ch = co_ref[r, h]
            nh = no_ref[r, h:h + 1, :]
            num = _dot(s_mat.astype(BF16), vb) + wide(w_inter) * _dot(qb, ch.astype(BF16))
            qn = jnp.sum(qb.astype(F32) * nh, axis=1, keepdims=True)
            den = jnp.sum(s_mat, axis=1, keepdims=True) + w_inter * qn
            inv = 1.0 / jnp.maximum(jnp.abs(den), jnp.exp(-m_t))
            hh_buf[r, :, sl] = num * wide(inv)
            m_new = m_t[L - 1:L, :]
            b_last = bcol[L - 1:L, :]
            w_end = jnp.exp(b_last[:, :L] + logw_t - m_new[:, :L])
            w_old = wide(jnp.exp(b_last + mp - m_new))
            kw_t = (kt_buf[r, h] * w_end).astype(BF16)
            co_ref[r, h] = w_old * ch + _dot(kw_t, vb)
            w8 = jnp.broadcast_to(w_end, (8, L)).astype(BF16)
            no_ref[r, h:h + 1, :] = w_old * nh + _dot(w8, k_buf[rs, sl])[0:1, :]
            m_next = jnp.where(lane == h, m_new, m_next)
        mo_ref[r] = m_next

    gnw = gnw_ref[...]
    skip = skip_ref[...]
    rows = []
    for r in range(R):
        parts = []
        for h in range(ML_HEADS):
            sl = slice(h * hd_dim, (h + 1) * hd_dim)
            hn_h = _rms(hh_buf[r, :, sl], gnw[:, sl])
            parts.append(((hn_h + skip[:, sl] * xc_buf[r, :, sl])
                          * _sigmoid(op_buf[r * L:(r + 1) * L, sl])).astype(BF16))
        rows.append(jnp.concatenate(parts, axis=-1))
    out = _dot(jnp.concatenate(rows, axis=0), wout_ref[...])
    for r in range(R):
        xo_ref[r] = x_ref[r] + mod_ref[r][2:3] * out[r * L:(r + 1) * L]


def _mlstm_layer(x, mod, nw, conv0, c0, n0, m0, p, R, L):
    bsz, t, d = x.shape
    inner = p['conv_w'].shape[1]
    hd = inner // ML_HEADS
    blk = lambda b, i: (b, 0, 0)
    blk4 = lambda b, i: (b, 0, 0, 0)
    names = ['wxm', 'wo', 'conv_w', 'conv_b', 'w_q', 'w_k', 'w_kt', 'w_v', 'wgq', 'wgk', 'wgv', 'bg',
             'norm', 'skip', 'w_out', 'eye_i', 'eye_f']
    in_specs = [
        pl.BlockSpec((R, L, d), lambda b, i: (b, i, 0)),
        pl.BlockSpec((R, 6, d), blk),
        _const_spec((1, d)),
        pl.BlockSpec((R, CONV_W - 1, inner), blk),
        pl.BlockSpec((R, ML_HEADS, hd, hd), blk4, pipeline_mode=pl.Buffered(1)),
        pl.BlockSpec((R, ML_HEADS, hd), blk),
        pl.BlockSpec((R, 1, LANES), blk),
    ] + [_const_spec(p[n].shape) for n in names]
    out_shape = (jax.ShapeDtypeStruct((bsz, t, d), F32),
                 jax.ShapeDtypeStruct((bsz, CONV_W - 1, inner), F32),
                 jax.ShapeDtypeStruct((bsz, ML_HEADS, hd, hd), F32),
                 jax.ShapeDtypeStruct((bsz, ML_HEADS, hd), F32),
                 jax.ShapeDtypeStruct((bsz, 1, LANES), F32))
    out_specs = (pl.BlockSpec((R, L, d), lambda b, i: (b, i, 0)),
                 pl.BlockSpec((R, CONV_W - 1, inner), blk),
                 pl.BlockSpec((R, ML_HEADS, hd, hd), blk4),
                 pl.BlockSpec((R, ML_HEADS, hd), blk),
                 pl.BlockSpec((R, 1, LANES), blk))
    scratch = [pltpu.VMEM((R, 8 + L, inner), F32), pltpu.VMEM((R, L, inner), F32),
               pltpu.VMEM((R * L, inner), BF16), pltpu.VMEM((R * L, inner), BF16),
               pltpu.VMEM((R, ML_HEADS, hd, L), F32), pltpu.VMEM((R * L, inner), BF16),
               pltpu.VMEM((R * L, inner), F32), pltpu.VMEM((R, L, inner), F32)]
    return pl.pallas_call(
        functools.partial(_mlstm_kernel, R=R, L=L),
        out_shape=out_shape, grid=(bsz // R, t // L), in_specs=in_specs, out_specs=out_specs,
        scratch_shapes=scratch,
        compiler_params=pltpu.CompilerParams(
            dimension_semantics=("arbitrary", "arbitrary"), vmem_limit_bytes=VMEM_LIMIT),
        name="mlstm_mixer",
    )(x, mod, nw, conv0, c0, n0, m0, *[p[n] for n in names])


def _router_kernel(x_ref, mod_ref, nw_ref, wr_ref, br_ref, h_ref, info_ref, cnt_ref, *, tm):
    x = x_ref[0]
    mod = mod_ref[0]
    h = _rms(x, nw_ref[...]) * (1.0 + mod[4:5]) + mod[3:4]
    h_ref[0] = h
    h1, h2, _ = _split3(h)
    w1, w2, _ = _split3(wr_ref[...])
    logits = (_dot(h1, w1) + (_dot(h1, w2) + _dot(h2, w1))) + br_ref[...]
    lane = lax.broadcasted_iota(jnp.int32, (tm, LANES), 1)
    lg = jnp.where(lane < N_EXPERTS, logits, -jnp.inf)
    m1 = jnp.max(lg, axis=1, keepdims=True)
    i1 = jnp.min(jnp.where(lg == m1, lane, LANES), axis=1, keepdims=True)
    lg2 = jnp.where(lane == i1, -jnp.inf, lg)
    m2 = jnp.max(lg2, axis=1, keepdims=True)
    i2 = jnp.min(jnp.where(lg2 == m2, lane, LANES), axis=1, keepdims=True)
    e2 = jnp.exp(m2 - m1)
    w_top1 = 1.0 / (1.0 + e2)
    w_top2 = e2 / (1.0 + e2)
    sel = jnp.logical_or(lane == i1, lane == i2)
    mask = jnp.where(sel, 1.0, 0.0).astype(BF16)
    r = lax.broadcasted_iota(jnp.int32, (tm, tm), 0)
    c = lax.broadcasted_iota(jnp.int32, (tm, tm), 1)
    before = jnp.where(c < r, 1.0, 0.0).astype(BF16)
    rank = _dot(before, mask)
    r1 = jnp.sum(jnp.where(lane == i1, rank, 0.0), axis=1, keepdims=True)
    r2 = jnp.sum(jnp.where(lane == i2, rank, 0.0), axis=1, keepdims=True)
    cols = (i1.astype(F32), i2.astype(F32), r1, r2, w_top1, w_top2)
    info = jnp.zeros((tm, LANES), F32)
    for k, col in enumerate(cols):
        info = jnp.where(lane == k, col, info)
    info_ref[0] = info
    cnt = jnp.sum(mask.astype(F32), axis=0, keepdims=True)
    cnt_ref[0, 0] = jnp.broadcast_to(cnt, (8, LANES)).astype(jnp.int32)


def _router(x, mod, nw, p, tm):
    bsz, t, d = x.shape
    nt = t // tm
    ti = lambda b, i: (b, i, 0)
    t4 = lambda b, i: (b, i, 0, 0)
    out_shape = (jax.ShapeDtypeStruct((bsz, t, d), F32),
                 jax.ShapeDtypeStruct((bsz, t, LANES), F32),
                 jax.ShapeDtypeStruct((bsz, nt, 8, LANES), jnp.int32))
    out_specs = (pl.BlockSpec((1, tm, d), ti), pl.BlockSpec((1, tm, LANES), ti),
                 pl.BlockSpec((1, 1, 8, LANES), t4))
    return pl.pallas_call(
        functools.partial(_router_kernel, tm=tm),
        out_shape=out_shape, grid=(bsz, nt),
        in_specs=[pl.BlockSpec((1, tm, d), ti), pl.BlockSpec((1, 6, d), lambda b, i: (b, 0, 0)),
                  _const_spec((1, d)), _const_spec(p['w_router'].shape), _const_spec(p['b_router'].shape)],
        out_specs=out_specs,
        compiler_params=pltpu.CompilerParams(
            dimension_semantics=("arbitrary", "arbitrary"), vmem_limit_bytes=VMEM_LIMIT),
        name="router",
    )(x, mod, nw, p['w_router'], p['b_router'])


SLAB = 512
DMA_UNROLL = 8


def _dispatch_kernel(ids_ref, pos_ref, h_ref, xs_ref, zbuf, sem, *, tm):
    @pl.when(jnp.logical_and(pl.program_id(0) == 0, pl.program_id(1) == 0))
    def _():
        zbuf[...] = jnp.zeros(zbuf.shape, F32)
        for i in range(2 * N_EXPERTS):
            c = pltpu.make_async_copy(zbuf, xs_ref.at[pl.ds(ids_ref[i] * SLAB, SLAB), :], sem)
            c.start()
            c.wait()

    def send(t, carry):
        for k in range(2):
            pltpu.make_async_copy(h_ref.at[0, pl.ds(t, 1), :],
                                  xs_ref.at[pl.ds(pos_ref[0, 0, k * tm + t], 1), :], sem).start(priority=k)
        return carry

    lax.fori_loop(0, tm, send, 0, unroll=DMA_UNROLL)
    for k in range(2):
        pltpu.make_async_copy(h_ref.at[0], xs_ref.at[pl.ds(0, tm), :], sem).wait()


def _dispatch(h, pos, slab_ids, n_rows, tm):
    bsz, t, d = h.shape
    nt = t // tm
    return pl.pallas_call(
        functools.partial(_dispatch_kernel, tm=tm),
        out_shape=jax.ShapeDtypeStruct((n_rows, d), F32),
        grid_spec=pltpu.PrefetchScalarGridSpec(
            num_scalar_prefetch=1, grid=(bsz, nt),
            in_specs=[pl.BlockSpec((1, 1, 2 * tm), lambda b, i, ids: (b * nt + i, 0, 0), memory_space=pltpu.SMEM),
                      pl.BlockSpec((1, tm, d), lambda b, i, ids: (b, i, 0))],
            out_specs=pl.BlockSpec(memory_space=pl.ANY),
            scratch_shapes=[pltpu.VMEM((SLAB, d), F32), pltpu.SemaphoreType.DMA(())]),
        compiler_params=pltpu.CompilerParams(
            dimension_semantics=("arbitrary", "arbitrary"), vmem_limit_bytes=VMEM_LIMIT),
        name="dispatch",
    )(slab_ids, pos, h)


def _slab_ffn_kernel(se_ref, nu_ref, x_ref, wg_ref, wu_ref, wd_ref, y_ref, *, nchunk):
    s = pl.program_id(0)

    @pl.when(s < nu_ref[0])
    def _():
        xb = x_ref[...].astype(BF16)
        fc = wg_ref.shape[2] // nchunk
        acc = None
        for j in range(nchunk):
            g = _dot(xb, wg_ref[0, :, j * fc:(j + 1) * fc])
            u = _dot(xb, wu_ref[0, :, j * fc:(j + 1) * fc])
            part = _dot((_silu(g) * u).astype(BF16), wd_ref[0, j * fc:(j + 1) * fc, :])
            acc = part if acc is None else acc + part
        y_ref[...] = acc

    @pl.when(s >= nu_ref[0])
    def _():
        y_ref[...] = jnp.zeros(y_ref.shape, F32)


def _slab_ffn(xs, slab_expert, n_used, wg, wu, wd):
    n_rows, d = xs.shape
    f = wg.shape[2]
    w_idx = lambda s, se, nu: (se[s], 0, 0)
    grid_spec = pltpu.PrefetchScalarGridSpec(
        num_scalar_prefetch=2, grid=(n_rows // SLAB,),
        in_specs=[pl.BlockSpec((SLAB, d), lambda s, se, nu: (s, 0)),
                  pl.BlockSpec((1, d, f), w_idx), pl.BlockSpec((1, d, f), w_idx), pl.BlockSpec((1, f, d), w_idx)],
        out_specs=pl.BlockSpec((SLAB, d), lambda s, se, nu: (s, 0)))
    return pl.pallas_call(
        functools.partial(_slab_ffn_kernel, nchunk=f // (2 * LANES)),
        out_shape=jax.ShapeDtypeStruct((n_rows, d), F32),
        grid_spec=grid_spec,
        compiler_params=pltpu.CompilerParams(dimension_semantics=("arbitrary",), vmem_limit_bytes=VMEM_LIMIT),
        name="slab_ffn",
    )(slab_expert, n_used, xs, wg, wu, wd)


def _combine_kernel(pos_ref, pos_next_ref, x_ref, info_ref, mod_ref, fw_ref, ys_ref, o_ref, buf, sem, *, tm):
    step = pl.program_id(0) * pl.num_programs(1) + pl.program_id(1)
    last = pl.num_programs(0) * pl.num_programs(1) - 1
    slot = lax.rem(step, 2)

    def request(p_ref, sl):
        def fetch(t, carry):
            for k in range(2):
                pltpu.make_async_copy(ys_ref.at[pl.ds(p_ref[0, 0, k * tm + t], 1), :],
                                      buf.at[sl, k, pl.ds(t, 1), :], sem.at[sl]).start(priority=k)
            return carry
        lax.fori_loop(0, tm, fetch, 0, unroll=DMA_UNROLL)

    @pl.when(step == 0)
    def _():
        request(pos_ref, 0)

    @pl.when(step < last)
    def _():
        request(pos_next_ref, 1 - slot)

    for k in range(2):
        pltpu.make_async_copy(ys_ref.at[pl.ds(0, tm), :], buf.at[slot, k], sem.at[slot]).wait()
    info = info_ref[0]
    y = x_ref[0] + mod_ref[0][5:6] * (info[:, 4:5] * buf[slot, 0] + info[:, 5:6] * buf[slot, 1])
    o_ref[0] = _rms(y, fw_ref[...])


def _combine(x, info, mod, fw, ys, pos, tm):
    bsz, t, d = x.shape
    nt = t // tm
    ti = lambda b, i: (b, i, 0)
    tiles = bsz * nt
    return pl.pallas_call(
        functools.partial(_combine_kernel, tm=tm),
        out_shape=jax.ShapeDtypeStruct(x.shape, F32),
        grid=(bsz, nt),
        in_specs=[pl.BlockSpec((1, 1, 2 * tm), lambda b, i: (b * nt + i, 0, 0), memory_space=pltpu.SMEM),
                  pl.BlockSpec((1, 1, 2 * tm), lambda b, i: (jnp.minimum(b * nt + i + 1, tiles - 1), 0, 0),
                               memory_space=pltpu.SMEM),
                  pl.BlockSpec((1, tm, d), ti), pl.BlockSpec((1, tm, LANES), ti),
                  pl.BlockSpec((1, 6, d), lambda b, i: (b, 0, 0)), _const_spec(fw.shape),
                  pl.BlockSpec(memory_space=pl.ANY)],
        out_specs=pl.BlockSpec((1, tm, d), ti),
        scratch_shapes=[pltpu.VMEM((2, 2, tm, d), F32), pltpu.SemaphoreType.DMA((2,))],
        compiler_params=pltpu.CompilerParams(
            dimension_semantics=("arbitrary", "arbitrary"), vmem_limit_bytes=VMEM_LIMIT),
        name="combine",
    )(pos, pos, x, info, mod, fw, ys)


def _moe_layer(x, mod, nw, p, fw, tm):
    bsz, t, _ = x.shape
    n_tok = bsz * t
    tiles = n_tok // tm
    h, info, cnt = _router(x, mod, nw, p, tm)
    counts = cnt[:, :, 0, :N_EXPERTS].reshape(tiles, N_EXPERTS)
    base = jnp.cumsum(counts, axis=0) - counts
    slabs = (jnp.sum(counts, axis=0) + (SLAB - 1)) // SLAB
    slab_end = jnp.cumsum(slabs)
    start = (slab_end - slabs) * SLAB
    n_slabs = (2 * n_tok) // SLAB + N_EXPERTS
    experts = jnp.arange(N_EXPERTS, dtype=jnp.int32)
    slab_expert = jnp.minimum(jnp.sum(jnp.arange(n_slabs, dtype=jnp.int32)[:, None] >= slab_end[None, :], axis=1),
                              N_EXPERTS - 1).astype(jnp.int32)
    inf2 = info.reshape(tiles, tm, LANES)
    offs = start[None, :] + base
    pos = []
    for k in range(2):
        e_k = inf2[:, :, k].astype(jnp.int32)
        off_k = sum(jnp.where(e_k == e, offs[:, e:e + 1], 0) for e in range(N_EXPERTS))
        pos.append(off_k + inf2[:, :, 2 + k].astype(jnp.int32))
    td = min(t, 4 * tm)
    pos_d = jnp.concatenate([q.reshape(n_tok // td, td) for q in pos], axis=1)
    pos_d = pos_d.reshape(n_tok // td, 1, 2 * td).astype(jnp.int32)
    pos = jnp.concatenate(pos, axis=1).reshape(tiles, 1, 2 * tm).astype(jnp.int32)
    partial = jnp.concatenate([jnp.maximum(slab_end - 1, 0), jnp.minimum(slab_end[-1] + experts, n_slabs - 1)])
    xs = _dispatch(h, pos_d, partial.astype(jnp.int32), n_slabs * SLAB, td)
    ys = _slab_ffn(xs, slab_expert, slab_end[-1:].astype(jnp.int32), p['w_gate'], p['w_up'], p['w_down'])
    return _combine(x, info, mod, fw, ys, pos, tm)


def _cast_kernel(x_ref, o_ref):
    o_ref[...] = x_ref[...].astype(BF16)


def _to_bf16(a, tr=1024):
    a2 = a.reshape(-1, a.shape[-1])
    rows, cols = a2.shape
    assert rows % tr == 0
    out = pl.pallas_call(
        _cast_kernel,
        out_shape=jax.ShapeDtypeStruct(a2.shape, BF16),
        grid=(rows // tr,),
        in_specs=[pl.BlockSpec((tr, cols), lambda i: (i, 0))],
        out_specs=pl.BlockSpec((tr, cols), lambda i: (i, 0)),
        compiler_params=pltpu.CompilerParams(
            dimension_semantics=("arbitrary",), vmem_limit_bytes=VMEM_LIMIT),
        name="to_bf16",
    )(a2)
    return out.reshape(a.shape)


def _pad_lanes(a):
    return jnp.pad(a, [(0, 0)] * (a.ndim - 1) + [(0, LANES - a.shape[-1])])


def _one_hot_rows(n, offset):
    r = lax.broadcasted_iota(jnp.int32, (n, LANES), 0)
    c = lax.broadcasted_iota(jnp.int32, (n, LANES), 1)
    return (c == r + offset).astype(BF16)


def _prep_ssm(w_in, conv_w, conv_b, dt_bias, a_log, d_skip, norm_w, w_out):
    d_in = SSM_GROUPS * SSM_HPG * SSM_HEAD_DIM
    cdim = conv_w.shape[1]
    heads = SSM_GROUPS * SSM_HPG
    r = lax.broadcasted_iota(jnp.int32, (LANES, d_in), 0)
    c = lax.broadcasted_iota(jnp.int32, (LANES, d_in), 1)
    return dict(
        wz=w_in[:, :d_in].astype(BF16), wx=w_in[:, d_in:d_in + cdim].astype(BF16),
        wdt=_pad_lanes(w_in[:, d_in + cdim:]).astype(BF16),
        conv_w=conv_w, conv_b=conv_b.reshape(1, cdim),
        dt_bias=_pad_lanes(dt_bias.reshape(1, heads)), a_log=_pad_lanes(a_log.reshape(1, heads)),
        d_full=jnp.repeat(d_skip, SSM_HEAD_DIM).reshape(1, d_in),
        norm=norm_w.reshape(1, d_in), w_out=w_out.astype(BF16),
        eye=_one_hot_rows(heads, 0), expand=(c // SSM_HEAD_DIM == r).astype(BF16))


def _prep_mlstm(w_in, conv_w, conv_b, w_q, w_k, w_v, w_ig, b_ig, w_fg, b_fg, norm_w, skip, w_out):
    inner = conv_w.shape[1]
    hd = inner // ML_HEADS
    wg = jnp.concatenate([w_ig, w_fg], axis=-1).reshape(ML_HEADS, 3, hd, 2 * ML_HEADS)
    part = lambda j: _pad_lanes(wg[:, j].reshape(inner, 2 * ML_HEADS)).astype(BF16)
    return dict(
        wxm=w_in[:, :inner].astype(BF16), wo=w_in[:, inner:].astype(BF16),
        conv_w=conv_w, conv_b=conv_b.reshape(1, inner),
        w_q=w_q.astype(BF16), w_k=w_k.astype(BF16), w_kt=jnp.swapaxes(w_k, 1, 2).astype(BF16),
        w_v=w_v.astype(BF16),
        wgq=part(0), wgk=part(1), wgv=part(2),
        bg=_pad_lanes(jnp.concatenate([b_ig, b_fg]).reshape(1, 2 * ML_HEADS)),
        norm=norm_w.reshape(1, inner), skip=skip.reshape(1, inner), w_out=w_out.astype(BF16),
        eye_i=_one_hot_rows(ML_HEADS, 0), eye_f=_one_hot_rows(ML_HEADS, ML_HEADS))


def _trunk(x, mod, ssm_conv, ssm_state, ml_conv, ml_c, ml_n, ml_m, p):
    bsz, t, d = x.shape
    L = SCAN_CHUNK if t % SCAN_CHUNK == 0 else t
    R = 2 if bsz % 2 == 0 else 1
    tm = min(t, 512)
    n_heads = SSM_GROUPS * SSM_HPG
    h0 = ssm_state.reshape(bsz, SSM_GROUPS, SSM_HPG, SSM_HEAD_DIM, SSM_STATE)
    h0 = h0.transpose(0, 1, 4, 2, 3).reshape(bsz, SSM_GROUPS, SSM_STATE, SSM_HPG * SSM_HEAD_DIM)
    x, conv_s, h_s = _ssd_layer(x, mod[0], p['norm_mix'][0], ssm_conv, h0, p['ssm'], R, L)
    h_s = h_s.reshape(bsz, SSM_GROUPS, SSM_STATE, SSM_HPG, SSM_HEAD_DIM).transpose(0, 1, 3, 4, 2)
    h_s = h_s.reshape(bsz, n_heads, SSM_HEAD_DIM, SSM_STATE)
    x = _ffn_layer(x, mod[0], p['norm_ffn'][0], p['ffn'], tm)
    m0 = _pad_lanes(ml_m).reshape(bsz, 1, LANES)
    x, conv_m, c_m, n_m, m_m = _mlstm_layer(x, mod[1], p['norm_mix'][1], ml_conv, ml_c, ml_n, m0, p['ml'], R, L)
    y = _moe_layer(x, mod[1], p['norm_ffn'][1], p['moe'], p['norm_final'], tm)
    return (y, conv_s[None], h_s[None], conv_m[None], c_m[None], n_m[None], m_m[:, 0, :ML_HEADS][None])


def kernel(x_prompt, x_sample, c_prompt, c_sample, state_ssm_conv, state_ssm, state_mlstm_conv, state_mlstm_C, state_mlstm_n, state_mlstm_m, w_ada, b_ada, norm_mix, norm_ffn, norm_final, ssm_w_in, ssm_conv_w, ssm_conv_b, ssm_dt_bias, ssm_a_log, ssm_d, ssm_norm, ssm_w_out, ml_w_in, ml_conv_w, ml_conv_b, ml_w_q, ml_w_k, ml_w_v, ml_w_igate, ml_b_igate, ml_w_fgate, ml_b_fgate, ml_norm, ml_skip, ml_w_out, ffn_w_gate, ffn_w_up, ffn_w_down, moe_w_router, moe_b_router, moe_w_gate, moe_w_up, moe_w_down):
    depth, d, _ = w_ada.shape
    assert depth == 2 and state_ssm.shape[0] == 1 and state_mlstm_C.shape[0] == 1
    bp, bs = x_prompt.shape[0], x_sample.shape[0]
    p = dict(
        norm_mix=norm_mix.reshape(depth, 1, d), norm_ffn=norm_ffn.reshape(depth, 1, d),
        norm_final=norm_final.reshape(1, d),
        ssm=_prep_ssm(ssm_w_in[0], ssm_conv_w[0], ssm_conv_b[0], ssm_dt_bias[0], ssm_a_log[0], ssm_d[0],
                      ssm_norm[0], ssm_w_out[0]),
        ml=_prep_mlstm(ml_w_in[0], ml_conv_w[0], ml_conv_b[0], ml_w_q[0], ml_w_k[0], ml_w_v[0], ml_w_igate[0],
                       ml_b_igate[0], ml_w_fgate[0], ml_b_fgate[0], ml_norm[0], ml_skip[0], ml_w_out[0]),
        ffn=dict(w_gate=ffn_w_gate[0].astype(BF16), w_up=ffn_w_up[0].astype(BF16),
                 w_down=ffn_w_down[0].astype(BF16)),
        moe=dict(w_router=_pad_lanes(moe_w_router[0]), b_router=_pad_lanes(moe_b_router[0].reshape(1, -1)),
                 w_gate=_to_bf16(moe_w_gate[0]), w_up=_to_bf16(moe_w_up[0]), w_down=_to_bf16(moe_w_down[0])))
    mod = _ada(jnp.concatenate([c_prompt, c_sample], axis=0), w_ada, b_ada)
    mod = mod.reshape(depth, bp + bs, 6, d)

    f = F32
    zeros = lambda a, b: jnp.zeros((b,) + a.shape[2:], f)
    out_p = _trunk(x_prompt, mod[:, :bp], zeros(state_ssm_conv, bp), zeros(state_ssm, bp),
                   zeros(state_mlstm_conv, bp), zeros(state_mlstm_C, bp), zeros(state_mlstm_n, bp),
                   zeros(state_mlstm_m, bp), p)
    out_s = _trunk(x_sample, mod[:, bp:], state_ssm_conv[0], state_ssm[0], state_mlstm_conv[0],
                   state_mlstm_C[0], state_mlstm_n[0], state_mlstm_m[0], p)
    return (out_p[0], out_s[0]) + tuple(out_p[1:]) + tuple(out_s[1:])
```

```python
import functools

import jax
import jax.numpy as jnp
from jax import lax
from jax.experimental import pallas as pl
from jax.experimental.pallas import tpu as pltpu

F32 = jnp.float32
BF16 = jnp.bfloat16
RMS_EPS = 1e-6
CONV_W = 4
LANES = 128
SCAN_CHUNK = LANES
TAIL0 = 8 - (CONV_W - 1)
VMEM_LIMIT = 60 * 1024 * 1024

SSM_GROUPS = 4
SSM_HPG = 8
SSM_HEAD_DIM = 64
SSM_STATE = 128
ML_HEADS = 8
N_EXPERTS = 8


def _dot(a, b):
    return jnp.dot(a, b, preferred_element_type=F32)


def _dot_nt(a, b):
    return lax.dot_general(a, b, (((1,), (1,)), ((), ())), preferred_element_type=F32)


def _dot_tn(a, b):
    return lax.dot_general(a, b, (((0,), (0,)), ((), ())), preferred_element_type=F32)


def _split3(x):
    h1 = x.astype(BF16)
    r = x - h1.astype(F32)
    h2 = r.astype(BF16)
    r = r - h2.astype(F32)
    return h1, h2, r.astype(BF16)


def _dot3_l(sel, x):
    return sum(_dot(sel, p) for p in _split3(x))


def _dot3_r(x, sel):
    return sum(_dot(p, sel) for p in _split3(x))


def _transpose_rows(x, eye):
    return sum(_dot_nt(eye, p) for p in _split3(x))


def _sigmoid(x):
    return 1.0 / (1.0 + jnp.exp(-x))


def _silu(x):
    return x * _sigmoid(x)


def _softplus(x):
    return jnp.maximum(x, 0.0) + jnp.log1p(jnp.exp(-jnp.abs(x)))


def _rms(x, g):
    return x * lax.rsqrt(jnp.mean(x * x, axis=-1, keepdims=True) + RMS_EPS) * g


def _tri_mask(n):
    r = lax.broadcasted_iota(jnp.int32, (n, n), 0)
    c = lax.broadcasted_iota(jnp.int32, (n, n), 1)
    return r >= c


def _causal_conv_silu(buf, dst, tt, cw_ref, cb_ref):
    xn = buf[8:8 + tt, :]
    x1 = pltpu.roll(xn, 1, axis=0)
    w = [cw_ref[k:k + 1, :] for k in range(CONV_W)]
    y = cb_ref[...] + xn * w[3] + x1 * w[2] + pltpu.roll(xn * w[1] + x1 * w[0], 2, axis=0)
    dst[...] = _silu(y)
    head = cb_ref[...]
    for k in range(CONV_W):
        head = head + buf[TAIL0 + k:TAIL0 + k + 8, :] * cw_ref[k:k + 1, :]
    dst[0:8, :] = _silu(head)


def _ada_kernel(c_ref, w_ref, b_ref, o_ref):
    ca = _silu(c_ref[...]).astype(BF16)
    o_ref[0] = _dot(ca, w_ref[0].astype(BF16)) + b_ref[0]


def _ada(c_all, w_ada, b_ada):
    depth, d, n = w_ada.shape
    bt = c_all.shape[0]
    tn = n // 4
    return pl.pallas_call(
        _ada_kernel,
        out_shape=jax.ShapeDtypeStruct((depth, bt, n), F32),
        grid=(depth, n // tn),
        in_specs=[pl.BlockSpec((bt, d), lambda i, j: (0, 0)),
                  pl.BlockSpec((1, d, tn), lambda i, j: (i, 0, j)),
                  pl.BlockSpec((1, 1, tn), lambda i, j: (i, 0, j))],
        out_specs=pl.BlockSpec((1, bt, tn), lambda i, j: (i, 0, j)),
        compiler_params=pltpu.CompilerParams(
            dimension_semantics=("arbitrary", "arbitrary"), vmem_limit_bytes=VMEM_LIMIT),
        name="ada",
    )(c_all, w_ada, b_ada.reshape(depth, 1, n))


def _const_spec(shape):
    nd = len(shape)
    return pl.BlockSpec(shape, lambda *_: (0,) * nd, pipeline_mode=pl.Buffered(1))


def _ssd_kernel(x_ref, mod_ref, nw_ref, conv0_ref, h0_ref, wz_ref, wx_ref, wdt_ref, cw_ref, cb_ref,
                dtb_ref, alog_ref, dfull_ref, gnw_ref, wout_ref, eye_ref, expand_ref,
                xo_ref, convo_ref, ho_ref,
                xbc_buf, act_buf, z_buf, y_buf, xw_buf, *, R, L):
    t = pl.program_id(1)
    inner = SSM_HPG * SSM_HEAD_DIM
    d_in = SSM_GROUPS * inner
    gn = SSM_GROUPS * SSM_STATE

    @pl.when(t == 0)
    def _():
        for r in range(R):
            xbc_buf[r, TAIL0:8, :] = conv0_ref[r]
        ho_ref[...] = h0_ref[...]

    nw = nw_ref[...]
    hn = jnp.concatenate(
        [(_rms(x_ref[r], nw) * (1.0 + mod_ref[r][1:2]) + mod_ref[r][0:1]).astype(BF16) for r in range(R)], axis=0)
    z_buf[...] = _dot(hn, wz_ref[...])
    xbc = _dot(hn, wx_ref[...])
    dt = _softplus(_dot(hn, wdt_ref[...]) + dtb_ref[...])
    a = -jnp.exp(alog_ref[...])
    tri = _tri_mask(L)
    tri_b = jnp.where(tri, 1.0, 0.0).astype(BF16)
    eye = eye_ref[...]
    expand = expand_ref[...]
    left = lax.broadcasted_iota(jnp.int32, (1, LANES), 1) < SSM_HEAD_DIM

    for r in range(R):
        xbc_buf[r, 8:8 + L, :] = xbc[r * L:(r + 1) * L]
        _causal_conv_silu(xbc_buf.at[r], act_buf.at[r], L, cw_ref, cb_ref)
        tail = xbc_buf[r, L + TAIL0:L + 8, :]
        xbc_buf[r, TAIL0:8, :] = tail
        convo_ref[r] = tail

    for r in range(R):
        dtc = dt[r * L:(r + 1) * L]
        cum = _dot3_l(tri_b, dtc * a)
        cum_t = _transpose_rows(cum, eye)
        dt_t = _transpose_rows(dtc, eye)
        cum_last = cum[L - 1:L, :]
        wend = (jnp.exp(cum_last - cum) * dtc).astype(BF16)
        xw_buf[r] = (act_buf[r, :, :d_in] * _dot(wend, expand)).astype(BF16)
        dec_last = jnp.exp(_dot3_r(cum[L - 8:L, :], expand)[7:8, :])
        for g in range(SSM_GROUPS):
            bg = act_buf[r, :, d_in + g * SSM_STATE:d_in + (g + 1) * SSM_STATE].astype(BF16)
            cg = act_buf[r, :, d_in + gn + g * SSM_STATE:d_in + gn + (g + 1) * SSM_STATE].astype(BF16)
            cb = _dot_nt(cg, bg)
            hg = ho_ref[r, g]
            y_int = _dot(cg, hg.astype(BF16))
            for pr in range(SSM_HPG // 2):
                hd0 = g * SSM_HPG + 2 * pr
                lo = hd0 * SSM_HEAD_DIM
                xp = act_buf[r, :, lo:lo + LANES]
                x2 = jnp.concatenate([jnp.where(left, xp, 0.0), jnp.where(left, 0.0, xp)], axis=0).astype(BF16)
                wms, es = [], []
                for hd in (hd0, hd0 + 1):
                    ccol = jnp.broadcast_to(cum[:, hd:hd + 1], (L, LANES))
                    dec = jnp.exp(jnp.where(tri, ccol[:, :L] - cum_t[hd:hd + 1, :], -jnp.inf))
                    wms.append((cb * dec * dt_t[hd:hd + 1, :]).astype(BF16))
                    es.append(jnp.exp(ccol))
                yi = y_int[:, pr * LANES:(pr + 1) * LANES]
                y_buf[r, :, lo:lo + LANES] = (_dot(jnp.concatenate(wms, axis=1), x2)
                                              + yi * jnp.where(left, es[0], es[1]))
            ho_ref[r, g] = (hg * dec_last[:, g * inner:(g + 1) * inner]
                           + _dot_tn(bg, xw_buf[r, :, g * inner:(g + 1) * inner]))

    gnw = gnw_ref[...]
    rows = []
    for r in range(R):
        y = y_buf[r] + dfull_ref[...] * act_buf[r, :, :d_in]
        y = y * _silu(z_buf[r * L:(r + 1) * L, :])
        rows.append(jnp.concatenate(
            [_rms(y[:, g * inner:(g + 1) * inner], gnw[:, g * inner:(g + 1) * inner]).astype(BF16)
             for g in range(SSM_GROUPS)], axis=-1))
    out = _dot(jnp.concatenate(rows, axis=0), wout_ref[...])
    for r in range(R):
        xo_ref[r] = x_ref[r] + mod_ref[r][2:3] * out[r * L:(r + 1) * L]


def _ssd_layer(x, mod, nw, conv0, h0, p, R, L):
    bsz, t, d = x.shape
    d_in = SSM_GROUPS * SSM_HPG * SSM_HEAD_DIM
    cdim = p['conv_w'].shape[1]
    blk = lambda b, i: (b, 0, 0)
    st_blk = (R, SSM_GROUPS, SSM_STATE, SSM_HPG * SSM_HEAD_DIM)
    names = ['wz', 'wx', 'wdt', 'conv_w', 'conv_b', 'dt_bias', 'a_log', 'd_full', 'norm', 'w_out', 'eye', 'expand']
    in_specs = [
        pl.BlockSpec((R, L, d), lambda b, i: (b, i, 0)),
        pl.BlockSpec((R, 6, d), blk),
        _const_spec((1, d)),
        pl.BlockSpec((R, CONV_W - 1, cdim), blk),
        pl.BlockSpec(st_blk, lambda b, i: (b, 0, 0, 0)),
    ] + [_const_spec(p[n].shape) for n in names]
    out_shape = (jax.ShapeDtypeStruct((bsz, t, d), F32),
                 jax.ShapeDtypeStruct((bsz, CONV_W - 1, cdim), F32),
                 jax.ShapeDtypeStruct(h0.shape, F32))
    out_specs = (pl.BlockSpec((R, L, d), lambda b, i: (b, i, 0)),
                 pl.BlockSpec((R, CONV_W - 1, cdim), blk),
                 pl.BlockSpec(st_blk, lambda b, i: (b, 0, 0, 0)))
    scratch = [pltpu.VMEM((R, 8 + L, cdim), F32), pltpu.VMEM((R, L, cdim), F32), pltpu.VMEM((R * L, d_in), F32),
               pltpu.VMEM((R, L, d_in), F32), pltpu.VMEM((R, L, d_in), BF16)]
    return pl.pallas_call(
        functools.partial(_ssd_kernel, R=R, L=L),
        out_shape=out_shape, grid=(bsz // R, t // L), in_specs=in_specs, out_specs=out_specs,
        scratch_shapes=scratch,
        compiler_params=pltpu.CompilerParams(
            dimension_semantics=("arbitrary", "arbitrary"), vmem_limit_bytes=VMEM_LIMIT),
        name="ssd_mixer",
    )(x, mod, nw, conv0, h0, *[p[n] for n in names])


def _ffn_kernel(x_ref, mod_ref, nw_ref, wg_ref, wu_ref, wd_ref, o_ref, *, nchunk):
    x = x_ref[0]
    mod = mod_ref[0]
    hn = (_rms(x, nw_ref[...]) * (1.0 + mod[4:5]) + mod[3:4]).astype(BF16)
    fc = wg_ref.shape[1] // nchunk
    acc = None
    for j in range(nchunk):
        g = _dot(hn, wg_ref[:, j * fc:(j + 1) * fc])
        u = _dot(hn, wu_ref[:, j * fc:(j + 1) * fc])
        part = _dot((_silu(g) * u).astype(BF16), wd_ref[j * fc:(j + 1) * fc, :])
        acc = part if acc is None else acc + part
    o_ref[0] = x + mod[5:6] * acc


def _ffn_layer(x, mod, nw, p, tm):
    bsz, t, d = x.shape
    tm = min(t, 2 * tm)
    return pl.pallas_call(
        functools.partial(_ffn_kernel, nchunk=p['w_gate'].shape[1] // (2 * LANES)),
        out_shape=jax.ShapeDtypeStruct(x.shape, F32),
        grid=(bsz, t // tm),
        in_specs=[pl.BlockSpec((1, tm, d), lambda b, i: (b, i, 0)),
                  pl.BlockSpec((1, 6, d), lambda b, i: (b, 0, 0)),
                  _const_spec((1, d)),
                  _const_spec(p['w_gate'].shape), _const_spec(p['w_up'].shape), _const_spec(p['w_down'].shape)],
        out_specs=pl.BlockSpec((1, tm, d), lambda b, i: (b, i, 0)),
        compiler_params=pltpu.CompilerParams(
            dimension_semantics=("arbitrary", "arbitrary"), vmem_limit_bytes=VMEM_LIMIT),
        name="ffn",
    )(x, mod, nw, p['w_gate'], p['w_up'], p['w_down'])


def _mlstm_kernel(x_ref, mod_ref, nw_ref, conv0_ref, c0_ref, n0_ref, m0_ref,
                  wxm_ref, wo_ref, cw_ref, cb_ref, wq_ref, wk_ref, wkt_ref, wv_ref, wgq_ref, wgk_ref, wgv_ref, bg_ref,
                  gnw_ref, skip_ref, wout_ref, eyei_ref, eyef_ref,
                  xo_ref, convo_ref, co_ref, no_ref, mo_ref,
                  xm_buf, xc_buf, q_buf, k_buf, kt_buf, v_buf, op_buf, hh_buf,
                  *, R, L):
    t = pl.program_id(1)
    hd_dim = wq_ref.shape[1]
    rep = hd_dim // LANES
    k_scale = hd_dim ** -0.5

    @pl.when(t == 0)
    def _():
        for r in range(R):
            xm_buf[r, TAIL0:8, :] = conv0_ref[r]
        co_ref[...] = c0_ref[...]
        no_ref[...] = n0_ref[...]
        mo_ref[...] = m0_ref[...]

    nw = nw_ref[...]
    hn = jnp.concatenate(
        [(_rms(x_ref[r], nw) * (1.0 + mod_ref[r][1:2]) + mod_ref[r][0:1]).astype(BF16) for r in range(R)], axis=0)
    xm = _dot(hn, wxm_ref[...])
    for r in range(R):
        xm_buf[r, 8:8 + L, :] = xm[r * L:(r + 1) * L]
    op_buf[...] = _dot(hn, wo_ref[...])
    for r in range(R):
        _causal_conv_silu(xm_buf.at[r], xc_buf.at[r], L, cw_ref, cb_ref)
        tail = xm_buf[r, L + TAIL0:L + 8, :]
        xm_buf[r, TAIL0:8, :] = tail
        convo_ref[r] = tail

    gates = bg_ref[...]
    for h in range(ML_HEADS):
        sl = slice(h * hd_dim, (h + 1) * hd_dim)
        xc_h = jnp.concatenate([xc_buf[r, :, sl].astype(BF16) for r in range(R)], axis=0)
        xm_h = jnp.concatenate([xm_buf[r, 8:8 + L, sl].astype(BF16) for r in range(R)], axis=0)
        q = _dot(xc_h, wq_ref[h])
        k = _dot(xc_h, wk_ref[h])
        v = _dot(xm_h, wv_ref[h])
        gates = gates + _dot(q.astype(BF16), wgq_ref[sl, :]) + _dot(k.astype(BF16), wgk_ref[sl, :]) \
            + _dot(v.astype(BF16), wgv_ref[sl, :])
        q_buf[:, sl] = q.astype(BF16)
        k_buf[:, sl] = (k * k_scale).astype(BF16)
        v_buf[:, sl] = v.astype(BF16)
        for r in range(R):
            kt_buf[r, h] = _dot_nt(wkt_ref[h], xc_h[r * L:(r + 1) * L]) * k_scale
    lf = jnp.minimum(gates, 0.0) - jnp.log1p(jnp.exp(-jnp.abs(gates)))
    tri = _tri_mask(L)
    tri_b = jnp.where(tri, 1.0, 0.0).astype(BF16)
    lane = lax.broadcasted_iota(jnp.int32, (1, LANES), 1)
    wide = lambda a: jnp.concatenate([a] * rep, axis=1)

    for r in range(R):
        rs = slice(r * L, (r + 1) * L)
        gi = gates[rs]
        bcum = _dot3_l(tri_b, lf[rs])
        li_t = _transpose_rows(gi, eyei_ref[...])
        b_t = _transpose_rows(bcum, eyef_ref[...])
        m_prev = mo_ref[r]
        m_next = m_prev
        for h in range(ML_HEADS):
            sl = slice(h * hd_dim, (h + 1) * hd_dim)
            bcol = jnp.broadcast_to(bcum[:, ML_HEADS + h:ML_HEADS + h + 1], (L, LANES))
            mp = jnp.broadcast_to(m_prev[:, h:h + 1], (1, LANES))
            logw_t = li_t[h:h + 1, :] - b_t[h:h + 1, :]
            dmat = jnp.where(tri, bcol[:, :L] + logw_t, -jnp.inf)
            a_inter = bcol + mp
            m_t = jnp.maximum(a_inter, jnp.max(dmat, axis=1, keepdims=True))
            qb = q_buf[rs, sl]
            vb = v_buf[rs, sl]
            s_mat = jnp.exp(dmat - m_t[:, :L]) * _dot_nt(qb, k_buf[rs, sl])
            w_inter = jnp.exp(a_inter - m_t)
            ch = co_ref[r, h]
            nh = no_ref[r, h:h + 1, :]
            num = _dot(s_mat.astype(BF16), vb) + wide(w_inter) * _dot(qb, ch.astype(BF16))
            qn = jnp.sum(qb.astype(F32) * nh, axis=1, keepdims=True)
            den = jnp.sum(s_mat, axis=1, keepdims=True) + w_inter * qn
            inv = 1.0 / jnp.maximum(jnp.abs(den), jnp.exp(-m_t))
            hh_buf[r, :, sl] = num * wide(inv)
            m_new = m_t[L - 1:L, :]
            b_last = bcol[L - 1:L, :]
            w_end = jnp.exp(b_last[:, :L] + logw_t - m_new[:, :L])
            w_old = wide(jnp.exp(b_last + mp - m_new))
            kw_t = (kt_buf[r, h] * w_end).astype(BF16)
            co_ref[r, h] = w_old * ch + _dot(kw_t, vb)
            w8 = jnp.broadcast_to(w_end, (8, L)).astype(BF16)
            no_ref[r, h:h + 1, :] = w_old * nh + _dot(w8, k_buf[rs, sl])[0:1, :]
            m_next = jnp.where(lane == h, m_new, m_next)
        mo_ref[r] = m_next

    gnw = gnw_ref[...]
    skip = skip_ref[...]
    rows = []
    for r in range(R):
        parts = []
        for h in range(ML_HEADS):
            sl = slice(h * hd_dim, (h + 1) * hd_dim)
            hn_h = _rms(hh_buf[r, :, sl], gnw[:, sl])
            parts.append(((hn_h + skip[:, sl] * xc_buf[r, :, sl])
                          * _sigmoid(op_buf[r * L:(r + 1) * L, sl])).astype(BF16))
        rows.append(jnp.concatenate(parts, axis=-1))
    out = _dot(jnp.concatenate(rows, axis=0), wout_ref[...])
    for r in range(R):
        xo_ref[r] = x_ref[r] + mod_ref[r][2:3] * out[r * L:(r + 1) * L]


def _mlstm_layer(x, mod, nw, conv0, c0, n0, m0, p, R, L):
    bsz, t, d = x.shape
    inner = p['conv_w'].shape[1]
    hd = inner // ML_HEADS
    blk = lambda b, i: (b, 0, 0)
    blk4 = lambda b, i: (b, 0, 0, 0)
    names = ['wxm', 'wo', 'conv_w', 'conv_b', 'w_q', 'w_k', 'w_kt', 'w_v', 'wgq', 'wgk', 'wgv', 'bg',
             'norm', 'skip', 'w_out', 'eye_i', 'eye_f']
    in_specs = [
        pl.BlockSpec((R, L, d), lambda b, i: (b, i, 0)),
        pl.BlockSpec((R, 6, d), blk),
        _const_spec((1, d)),
        pl.BlockSpec((R, CONV_W - 1, inner), blk),
        pl.BlockSpec((R, ML_HEADS, hd, hd), blk4, pipeline_mode=pl.Buffered(1)),
        pl.BlockSpec((R, ML_HEADS, hd), blk),
        pl.BlockSpec((R, 1, LANES), blk),
    ] + [_const_spec(p[n].shape) for n in names]
    out_shape = (jax.ShapeDtypeStruct((bsz, t, d), F32),
                 jax.ShapeDtypeStruct((bsz, CONV_W - 1, inner), F32),
                 jax.ShapeDtypeStruct((bsz, ML_HEADS, hd, hd), F32),
                 jax.ShapeDtypeStruct((bsz, ML_HEADS, hd), F32),
                 jax.ShapeDtypeStruct((bsz, 1, LANES), F32))
    out_specs = (pl.BlockSpec((R, L, d), lambda b, i: (b, i, 0)),
                 pl.BlockSpec((R, CONV_W - 1, inner), blk),
                 pl.BlockSpec((R, ML_HEADS, hd, hd), blk4),
                 pl.BlockSpec((R, ML_HEADS, hd), blk),
                 pl.BlockSpec((R, 1, LANES), blk))
    scratch = [pltpu.VMEM((R, 8 + L, inner), F32), pltpu.VMEM((R, L, inner), F32),
               pltpu.VMEM((R * L, inner), BF16), pltpu.VMEM((R * L, inner), BF16),
               pltpu.VMEM((R, ML_HEADS, hd, L), F32), pltpu.VMEM((R * L, inner), BF16),
               pltpu.VMEM((R * L, inner), F32), pltpu.VMEM((R, L, inner), F32)]
    return pl.pallas_call(
        functools.partial(_mlstm_kernel, R=R, L=L),
        out_shape=out_shape, grid=(bsz // R, t // L), in_specs=in_specs, out_specs=out_specs,
        scratch_shapes=scratch,
        compiler_params=pltpu.CompilerParams(
            dimension_semantics=("arbitrary", "arbitrary"), vmem_limit_bytes=VMEM_LIMIT),
        name="mlstm_mixer",
    )(x, mod, nw, conv0, c0, n0, m0, *[p[n] for n in names])


def _router_kernel(x_ref, mod_ref, nw_ref, wr_ref, br_ref, eye_ref, h_ref, info_ref, meta_ref, cnt_ref, *, tm):
    x = x_ref[0]
    mod = mod_ref[0]
    h = _rms(x, nw_ref[...]) * (1.0 + mod[4:5]) + mod[3:4]
    h_ref[0] = h
    h1, h2, _ = _split3(h)
    w1, w2, _ = _split3(wr_ref[...])
    logits = (_dot(h1, w1) + (_dot(h1, w2) + _dot(h2, w1))) + br_ref[...]
    lane = lax.broadcasted_iota(jnp.int32, (tm, LANES), 1)
    lg = jnp.where(lane < N_EXPERTS, logits, -jnp.inf)
    m1 = jnp.max(lg, axis=1, keepdims=True)
    i1 = jnp.min(jnp.where(lg == m1, lane, LANES), axis=1, keepdims=True)
    lg2 = jnp.where(lane == i1, -jnp.inf, lg)
    m2 = jnp.max(lg2, axis=1, keepdims=True)
    i2 = jnp.min(jnp.where(lg2 == m2, lane, LANES), axis=1, keepdims=True)
    e2 = jnp.exp(m2 - m1)
    w_top1 = 1.0 / (1.0 + e2)
    w_top2 = e2 / (1.0 + e2)
    sel = jnp.logical_or(lane == i1, lane == i2)
    mask = jnp.where(sel, 1.0, 0.0).astype(BF16)
    r = lax.broadcasted_iota(jnp.int32, (tm, tm), 0)
    c = lax.broadcasted_iota(jnp.int32, (tm, tm), 1)
    before = jnp.where(c < r, 1.0, 0.0).astype(BF16)
    rank = _dot(before, mask)
    r1 = jnp.sum(jnp.where(lane == i1, rank, 0.0), axis=1, keepdims=True)
    r2 = jnp.sum(jnp.where(lane == i2, rank, 0.0), axis=1, keepdims=True)
    cols = (i1.astype(F32), i2.astype(F32), r1, r2, w_top1, w_top2)
    info = jnp.zeros((tm, LANES), F32)
    for k, col in enumerate(cols):
        info = jnp.where(lane == k, col, info)
    info_ref[0] = info
    meta_ref[0, 0] = _transpose_rows(info, eye_ref[...])
    cnt = jnp.sum(mask.astype(F32), axis=0, keepdims=True)
    cnt_ref[0, 0] = jnp.broadcast_to(cnt, (8, LANES)).astype(jnp.int32)


def _router(x, mod, nw, p, tm):
    bsz, t, d = x.shape
    nt = t // tm
    ti = lambda b, i: (b, i, 0)
    t4 = lambda b, i: (b, i, 0, 0)
    out_shape = (jax.ShapeDtypeStruct((bsz, t, d), F32),
                 jax.ShapeDtypeStruct((bsz, t, LANES), F32),
                 jax.ShapeDtypeStruct((bsz, nt, 8, tm), F32),
                 jax.ShapeDtypeStruct((bsz, nt, 8, LANES), jnp.int32))
    out_specs = (pl.BlockSpec((1, tm, d), ti), pl.BlockSpec((1, tm, LANES), ti),
                 pl.BlockSpec((1, 1, 8, tm), t4), pl.BlockSpec((1, 1, 8, LANES), t4))
    return pl.pallas_call(
        functools.partial(_router_kernel, tm=tm),
        out_shape=out_shape, grid=(bsz, nt),
        in_specs=[pl.BlockSpec((1, tm, d), ti), pl.BlockSpec((1, 6, d), lambda b, i: (b, 0, 0)),
                  _const_spec((1, d)), _const_spec(p['w_router'].shape), _const_spec(p['b_router'].shape),
                  _const_spec(p['eye8'].shape)],
        out_specs=out_specs,
        compiler_params=pltpu.CompilerParams(
            dimension_semantics=("arbitrary", "arbitrary"), vmem_limit_bytes=VMEM_LIMIT),
        name="router",
    )(x, mod, nw, p['w_router'], p['b_router'], p['eye8'])


SLAB = 512
DMA_UNROLL = 8


def _dispatch_kernel(ids_ref, pos_ref, h_ref, xs_ref, zbuf, sem, *, tm):
    @pl.when(jnp.logical_and(pl.program_id(0) == 0, pl.program_id(1) == 0))
    def _():
        zbuf[...] = jnp.zeros(zbuf.shape, F32)
        for i in range(2 * N_EXPERTS):
            c = pltpu.make_async_copy(zbuf, xs_ref.at[pl.ds(ids_ref[i] * SLAB, SLAB), :], sem)
            c.start()
            c.wait()

    def send(t, carry):
        for k in range(2):
            pltpu.make_async_copy(h_ref.at[0, pl.ds(t, 1), :],
                                  xs_ref.at[pl.ds(pos_ref[0, 0, k * tm + t], 1), :], sem).start(priority=k)
        return carry

    lax.fori_loop(0, tm, send, 0, unroll=DMA_UNROLL)
    for k in range(2):
        pltpu.make_async_copy(h_ref.at[0], xs_ref.at[pl.ds(0, tm), :], sem).wait()


def _dispatch(h, pos, slab_ids, n_rows, tm):
    bsz, t, d = h.shape
    nt = t // tm
    return pl.pallas_call(
        functools.partial(_dispatch_kernel, tm=tm),
        out_shape=jax.ShapeDtypeStruct((n_rows, d), F32),
        grid_spec=pltpu.PrefetchScalarGridSpec(
            num_scalar_prefetch=1, grid=(bsz, nt),
            in_specs=[pl.BlockSpec((1, 1, 2 * tm), lambda b, i, ids: (b * nt + i, 0, 0), memory_space=pltpu.SMEM),
                      pl.BlockSpec((1, tm, d), lambda b, i, ids: (b, i, 0))],
            out_specs=pl.BlockSpec(memory_space=pl.ANY),
            scratch_shapes=[pltpu.VMEM((SLAB, d), F32), pltpu.SemaphoreType.DMA(())]),
        compiler_params=pltpu.CompilerParams(
            dimension_semantics=("arbitrary", "arbitrary"), vmem_limit_bytes=VMEM_LIMIT),
        name="dispatch",
    )(slab_ids, pos, h)


def _slab_ffn_kernel(se_ref, nu_ref, x_ref, wg_ref, wu_ref, wd_ref, y_ref, *, nchunk):
    s = pl.program_id(0)

    @pl.when(s < nu_ref[0])
    def _():
        xb = x_ref[...].astype(BF16)
        fc = wg_ref.shape[2] // nchunk
        acc = None
        for j in range(nchunk):
            g = _dot(xb, wg_ref[0, :, j * fc:(j + 1) * fc])
            u = _dot(xb, wu_ref[0, :, j * fc:(j + 1) * fc])
            part = _dot((_silu(g) * u).astype(BF16), wd_ref[0, j * fc:(j + 1) * fc, :])
            acc = part if acc is None else acc + part
        y_ref[...] = acc

    @pl.when(s >= nu_ref[0])
    def _():
        y_ref[...] = jnp.zeros(y_ref.shape, F32)


def _slab_ffn(xs, slab_expert, n_used, wg, wu, wd):
    n_rows, d = xs.shape
    f = wg.shape[2]
    w_idx = lambda s, se, nu: (se[s], 0, 0)
    grid_spec = pltpu.PrefetchScalarGridSpec(
        num_scalar_prefetch=2, grid=(n_rows // SLAB,),
        in_specs=[pl.BlockSpec((SLAB, d), lambda s, se, nu: (s, 0)),
                  pl.BlockSpec((1, d, f), w_idx), pl.BlockSpec((1, d, f), w_idx), pl.BlockSpec((1, f, d), w_idx)],
        out_specs=pl.BlockSpec((SLAB, d), lambda s, se, nu: (s, 0)))
    return pl.pallas_call(
        functools.partial(_slab_ffn_kernel, nchunk=f // (2 * LANES)),
        out_shape=jax.ShapeDtypeStruct((n_rows, d), F32),
        grid_spec=grid_spec,
        compiler_params=pltpu.CompilerParams(dimension_semantics=("arbitrary",), vmem_limit_bytes=VMEM_LIMIT),
        name="slab_ffn",
    )(slab_expert, n_used, xs, wg, wu, wd)


def _combine_kernel(pos_ref, pos_next_ref, x_ref, info_ref, mod_ref, fw_ref, ys_ref, o_ref, buf, sem, *, tm):
    step = pl.program_id(0) * pl.num_programs(1) + pl.program_id(1)
    last = pl.num_programs(0) * pl.num_programs(1) - 1
    slot = lax.rem(step, 2)

    def request(p_ref, sl):
        def fetch(t, carry):
            for k in range(2):
                pltpu.make_async_copy(ys_ref.at[pl.ds(p_ref[0, 0, k * tm + t], 1), :],
                                      buf.at[sl, k, pl.ds(t, 1), :], sem.at[sl]).start(priority=k)
            return carry
        lax.fori_loop(0, tm, fetch, 0, unroll=DMA_UNROLL)

    @pl.when(step == 0)
    def _():
        request(pos_ref, 0)

    @pl.when(step < last)
    def _():
        request(pos_next_ref, 1 - slot)

    for k in range(2):
        pltpu.make_async_copy(ys_ref.at[pl.ds(0, tm), :], buf.at[slot, k], sem.at[slot]).wait()
    info = info_ref[0]
    y = x_ref[0] + mod_ref[0][5:6] * (info[:, 4:5] * buf[slot, 0] + info[:, 5:6] * buf[slot, 1])
    o_ref[0] = _rms(y, fw_ref[...])


def _combine(x, info, mod, fw, ys, pos, tm):
    bsz, t, d = x.shape
    nt = t // tm
    ti = lambda b, i: (b, i, 0)
    tiles = bsz * nt
    return pl.pallas_call(
        functools.partial(_combine_kernel, tm=tm),
        out_shape=jax.ShapeDtypeStruct(x.shape, F32),
        grid=(bsz, nt),
        in_specs=[pl.BlockSpec((1, 1, 2 * tm), lambda b, i: (b * nt + i, 0, 0), memory_space=pltpu.SMEM),
                  pl.BlockSpec((1, 1, 2 * tm), lambda b, i: (jnp.minimum(b * nt + i + 1, tiles - 1), 0, 0),
                               memory_space=pltpu.SMEM),
                  pl.BlockSpec((1, tm, d), ti), pl.BlockSpec((1, tm, LANES), ti),
                  pl.BlockSpec((1, 6, d), lambda b, i: (b, 0, 0)), _const_spec(fw.shape),
                  pl.BlockSpec(memory_space=pl.ANY)],
        out_specs=pl.BlockSpec((1, tm, d), ti),
        scratch_shapes=[pltpu.VMEM((2, 2, tm, d), F32), pltpu.SemaphoreType.DMA((2,))],
        compiler_params=pltpu.CompilerParams(
            dimension_semantics=("arbitrary", "arbitrary"), vmem_limit_bytes=VMEM_LIMIT),
        name="combine",
    )(pos, pos, x, info, mod, fw, ys)


def _moe_layer(x, mod, nw, p, fw, tm):
    bsz, t, _ = x.shape
    n_tok = bsz * t
    tiles = n_tok // tm
    h, info, meta, cnt = _router(x, mod, nw, p, tm)
    counts = cnt[:, :, 0, :N_EXPERTS].reshape(tiles, N_EXPERTS)
    base = jnp.cumsum(counts, axis=0) - counts
    slabs = (jnp.sum(counts, axis=0) + (SLAB - 1)) // SLAB
    slab_end = jnp.cumsum(slabs)
    start = (slab_end - slabs) * SLAB
    n_slabs = (2 * n_tok) // SLAB + N_EXPERTS
    experts = jnp.arange(N_EXPERTS, dtype=jnp.int32)
    slab_expert = jnp.minimum(jnp.sum(jnp.arange(n_slabs, dtype=jnp.int32)[:, None] >= slab_end[None, :], axis=1),
                              N_EXPERTS - 1).astype(jnp.int32)
    meta = meta.reshape(tiles, 8, tm).astype(jnp.int32)
    offs = start[None, :] + base
    pos = []
    for k in range(2):
        e_k = meta[:, k, :]
        off_k = sum(jnp.where(e_k == e, offs[:, e:e + 1], 0) for e in range(N_EXPERTS))
        pos.append(off_k + meta[:, 2 + k, :])
    td = min(t, 4 * tm)
    pos_d = jnp.concatenate([q.reshape(n_tok // td, td) for q in pos], axis=1)
    pos_d = pos_d.reshape(n_tok // td, 1, 2 * td).astype(jnp.int32)
    pos = jnp.concatenate(pos, axis=1).reshape(tiles, 1, 2 * tm).astype(jnp.int32)
    partial = jnp.concatenate([jnp.maximum(slab_end - 1, 0), jnp.minimum(slab_end[-1] + experts, n_slabs - 1)])
    xs = _dispatch(h, pos_d, partial.astype(jnp.int32), n_slabs * SLAB, td)
    ys = _slab_ffn(xs, slab_expert, slab_end[-1:].astype(jnp.int32), p['w_gate'], p['w_up'], p['w_down'])
    return _combine(x, info, mod, fw, ys, pos, tm)


def _cast_kernel(x_ref, o_ref):
    o_ref[...] = x_ref[...].astype(BF16)


def _to_bf16(a, tr=1024):
    a2 = a.reshape(-1, a.shape[-1])
    rows, cols = a2.shape
    assert rows % tr == 0
    out = pl.pallas_call(
        _cast_kernel,
        out_shape=jax.ShapeDtypeStruct(a2.shape, BF16),
        grid=(rows // tr,),
        in_specs=[pl.BlockSpec((tr, cols), lambda i: (i, 0))],
        out_specs=pl.BlockSpec((tr, cols), lambda i: (i, 0)),
        compiler_params=pltpu.CompilerParams(
            dimension_semantics=("arbitrary",), vmem_limit_bytes=VMEM_LIMIT),
        name="to_bf16",
    )(a2)
    return out.reshape(a.shape)


def _pad_lanes(a):
    return jnp.pad(a, [(0, 0)] * (a.ndim - 1) + [(0, LANES - a.shape[-1])])


def _one_hot_rows(n, offset):
    r = lax.broadcasted_iota(jnp.int32, (n, LANES), 0)
    c = lax.broadcasted_iota(jnp.int32, (n, LANES), 1)
    return (c == r + offset).astype(BF16)


def _prep_ssm(w_in, conv_w, conv_b, dt_bias, a_log, d_skip, norm_w, w_out):
    d_in = SSM_GROUPS * SSM_HPG * SSM_HEAD_DIM
    cdim = conv_w.shape[1]
    heads = SSM_GROUPS * SSM_HPG
    r = lax.broadcasted_iota(jnp.int32, (LANES, d_in), 0)
    c = lax.broadcasted_iota(jnp.int32, (LANES, d_in), 1)
    return dict(
        wz=w_in[:, :d_in].astype(BF16), wx=w_in[:, d_in:d_in + cdim].astype(BF16),
        wdt=_pad_lanes(w_in[:, d_in + cdim:]).astype(BF16),
        conv_w=conv_w, conv_b=conv_b.reshape(1, cdim),
        dt_bias=_pad_lanes(dt_bias.reshape(1, heads)), a_log=_pad_lanes(a_log.reshape(1, heads)),
        d_full=jnp.repeat(d_skip, SSM_HEAD_DIM).reshape(1, d_in),
        norm=norm_w.reshape(1, d_in), w_out=w_out.astype(BF16),
        eye=_one_hot_rows(heads, 0), expand=(c // SSM_HEAD_DIM == r).astype(BF16))


def _prep_mlstm(w_in, conv_w, conv_b, w_q, w_k, w_v, w_ig, b_ig, w_fg, b_fg, norm_w, skip, w_out):
    inner = conv_w.shape[1]
    hd = inner // ML_HEADS
    wg = jnp.concatenate([w_ig, w_fg], axis=-1).reshape(ML_HEADS, 3, hd, 2 * ML_HEADS)
    part = lambda j: _pad_lanes(wg[:, j].reshape(inner, 2 * ML_HEADS)).astype(BF16)
    return dict(
        wxm=w_in[:, :inner].astype(BF16), wo=w_in[:, inner:].astype(BF16),
        conv_w=conv_w, conv_b=conv_b.reshape(1, inner),
        w_q=w_q.astype(BF16), w_k=w_k.astype(BF16), w_kt=jnp.swapaxes(w_k, 1, 2).astype(BF16),
        w_v=w_v.astype(BF16),
        wgq=part(0), wgk=part(1), wgv=part(2),
        bg=_pad_lanes(jnp.concatenate([b_ig, b_fg]).reshape(1, 2 * ML_HEADS)),
        norm=norm_w.reshape(1, inner), skip=skip.reshape(1, inner), w_out=w_out.astype(BF16),
        eye_i=_one_hot_rows(ML_HEADS, 0), eye_f=_one_hot_rows(ML_HEADS, ML_HEADS))


def _trunk(x, mod, ssm_conv, ssm_state, ml_conv, ml_c, ml_n, ml_m, p):
    bsz, t, d = x.shape
    L = SCAN_CHUNK if t % SCAN_CHUNK == 0 else t
    R = 2 if bsz % 2 == 0 else 1
    tm = min(t, 512)
    n_heads = SSM_GROUPS * SSM_HPG
    h0 = ssm_state.reshape(bsz, SSM_GROUPS, SSM_HPG, SSM_HEAD_DIM, SSM_STATE)
    h0 = h0.transpose(0, 1, 4, 2, 3).reshape(bsz, SSM_GROUPS, SSM_STATE, SSM_HPG * SSM_HEAD_DIM)
    x, conv_s, h_s = _ssd_layer(x, mod[0], p['norm_mix'][0], ssm_conv, h0, p['ssm'], R, L)
    h_s = h_s.reshape(bsz, SSM_GROUPS, SSM_STATE, SSM_HPG, SSM_HEAD_DIM).transpose(0, 1, 3, 4, 2)
    h_s = h_s.reshape(bsz, n_heads, SSM_HEAD_DIM, SSM_STATE)
    x = _ffn_layer(x, mod[0], p['norm_ffn'][0], p['ffn'], tm)
    m0 = _pad_lanes(ml_m).reshape(bsz, 1, LANES)
    x, conv_m, c_m, n_m, m_m = _mlstm_layer(x, mod[1], p['norm_mix'][1], ml_conv, ml_c, ml_n, m0, p['ml'], R, L)
    y = _moe_layer(x, mod[1], p['norm_ffn'][1], p['moe'], p['norm_final'], tm)
    return (y, conv_s[None], h_s[None], conv_m[None], c_m[None], n_m[None], m_m[:, 0, :ML_HEADS][None])


def kernel(x_prompt, x_sample, c_prompt, c_sample, state_ssm_conv, state_ssm, state_mlstm_conv, state_mlstm_C, state_mlstm_n, state_mlstm_m, w_ada, b_ada, norm_mix, norm_ffn, norm_final, ssm_w_in, ssm_conv_w, ssm_conv_b, ssm_dt_bias, ssm_a_log, ssm_d, ssm_norm, ssm_w_out, ml_w_in, ml_conv_w, ml_conv_b, ml_w_q, ml_w_k, ml_w_v, ml_w_igate, ml_b_igate, ml_w_fgate, ml_b_fgate, ml_norm, ml_skip, ml_w_out, ffn_w_gate, ffn_w_up, ffn_w_down, moe_w_router, moe_b_router, moe_w_gate, moe_w_up, moe_w_down):
    depth, d, _ = w_ada.shape
    assert depth == 2 and state_ssm.shape[0] == 1 and state_mlstm_C.shape[0] == 1
    bp, bs = x_prompt.shape[0], x_sample.shape[0]
    p = dict(
        norm_mix=norm_mix.reshape(depth, 1, d), norm_ffn=norm_ffn.reshape(depth, 1, d),
        norm_final=norm_final.reshape(1, d),
        ssm=_prep_ssm(ssm_w_in[0], ssm_conv_w[0], ssm_conv_b[0], ssm_dt_bias[0], ssm_a_log[0], ssm_d[0],
                      ssm_norm[0], ssm_w_out[0]),
        ml=_prep_mlstm(ml_w_in[0], ml_conv_w[0], ml_conv_b[0], ml_w_q[0], ml_w_k[0], ml_w_v[0], ml_w_igate[0],
                       ml_b_igate[0], ml_w_fgate[0], ml_b_fgate[0], ml_norm[0], ml_skip[0], ml_w_out[0]),
        ffn=dict(w_gate=ffn_w_gate[0].astype(BF16), w_up=ffn_w_up[0].astype(BF16),
                 w_down=ffn_w_down[0].astype(BF16)),
        moe=dict(w_router=_pad_lanes(moe_w_router[0]), b_router=_pad_lanes(moe_b_router[0].reshape(1, -1)),
                 eye8=_one_hot_rows(8, 0),
                 w_gate=_to_bf16(moe_w_gate[0]), w_up=_to_bf16(moe_w_up[0]), w_down=_to_bf16(moe_w_down[0])))
    mod = _ada(jnp.concatenate([c_prompt, c_sample], axis=0), w_ada, b_ada)
    mod = mod.reshape(depth, bp + bs, 6, d)

    f = F32
    zeros = lambda a, b: jnp.zeros((b,) + a.shape[2:], f)
    out_p = _trunk(x_prompt, mod[:, :bp], zeros(state_ssm_conv, bp), zeros(state_ssm, bp),
                   zeros(state_mlstm_conv, bp), zeros(state_mlstm_C, bp), zeros(state_mlstm_n, bp),
                   zeros(state_mlstm_m, bp), p)
    out_s = _trunk(x_sample, mod[:, bp:], state_ssm_conv[0], state_ssm[0], state_mlstm_conv[0],
                   state_mlstm_C[0], state_mlstm_n[0], state_mlstm_m[0], p)
    return (out_p[0], out_s[0]) + tuple(out_p[1:]) + tuple(out_s[1:])
```

```python
import functools

import jax
import jax.numpy as jnp
from jax import lax
from jax.experimental import pallas as pl
from jax.experimental.pallas import tpu as pltpu

F32 = jnp.float32
BF16 = jnp.bfloat16
RMS_EPS = 1e-6
CONV_W = 4
LANES = 128
SCAN_CHUNK = LANES
TAIL0 = 8 - (CONV_W - 1)
VMEM_LIMIT = 60 * 1024 * 1024

SSM_GROUPS = 4
SSM_HPG = 8
SSM_HEAD_DIM = 64
SSM_STATE = 128
ML_HEADS = 8
N_EXPERTS = 8


def _dot(a, b):
    return jnp.dot(a, b, preferred_element_type=F32)


def _dot_nt(a, b):
    return lax.dot_general(a, b, (((1,), (1,)), ((), ())), preferred_element_type=F32)


def _dot_tn(a, b):
    return lax.dot_general(a, b, (((0,), (0,)), ((), ())), preferred_element_type=F32)


def _split3(x):
    h1 = x.astype(BF16)
    r = x - h1.astype(F32)
    h2 = r.astype(BF16)
    r = r - h2.astype(F32)
    return h1, h2, r.astype(BF16)


def _dot3_l(sel, x):
    return sum(_dot(sel, p) for p in _split3(x))


def _dot3_r(x, sel):
    return sum(_dot(p, sel) for p in _split3(x))


def _transpose_rows(x, eye):
    return sum(_dot_nt(eye, p) for p in _split3(x))


def _sigmoid(x):
    return 1.0 / (1.0 + jnp.exp(-x))


def _silu(x):
    return x * _sigmoid(x)


def _softplus(x):
    return jnp.maximum(x, 0.0) + jnp.log1p(jnp.exp(-jnp.abs(x)))


def _rms(x, g):
    return x * lax.rsqrt(jnp.mean(x * x, axis=-1, keepdims=True) + RMS_EPS) * g


def _tri_mask(n):
    r = lax.broadcasted_iota(jnp.int32, (n, n), 0)
    c = lax.broadcasted_iota(jnp.int32, (n, n), 1)
    return r >= c


def _causal_conv_silu(buf, dst, tt, cw_ref, cb_ref):
    xn = buf[8:8 + tt, :]
    x1 = pltpu.roll(xn, 1, axis=0)
    w = [cw_ref[k:k + 1, :] for k in range(CONV_W)]
    y = cb_ref[...] + xn * w[3] + x1 * w[2] + pltpu.roll(xn * w[1] + x1 * w[0], 2, axis=0)
    dst[...] = _silu(y)
    head = cb_ref[...]
    for k in range(CONV_W):
        head = head + buf[TAIL0 + k:TAIL0 + k + 8, :] * cw_ref[k:k + 1, :]
    dst[0:8, :] = _silu(head)


def _ada_kernel(c_ref, w_ref, b_ref, o_ref):
    ca = _silu(c_ref[...]).astype(BF16)
    o_ref[0] = _dot(ca, w_ref[0].astype(BF16)) + b_ref[0]


def _ada(c_all, w_ada, b_ada):
    depth, d, n = w_ada.shape
    bt = c_all.shape[0]
    tn = n // 4
    return pl.pallas_call(
        _ada_kernel,
        out_shape=jax.ShapeDtypeStruct((depth, bt, n), F32),
        grid=(depth, n // tn),
        in_specs=[pl.BlockSpec((bt, d), lambda i, j: (0, 0)),
                  pl.BlockSpec((1, d, tn), lambda i, j: (i, 0, j)),
                  pl.BlockSpec((1, 1, tn), lambda i, j: (i, 0, j))],
        out_specs=pl.BlockSpec((1, bt, tn), lambda i, j: (i, 0, j)),
        compiler_params=pltpu.CompilerParams(
            dimension_semantics=("arbitrary", "arbitrary"), vmem_limit_bytes=VMEM_LIMIT),
        name="ada",
    )(c_all, w_ada, b_ada.reshape(depth, 1, n))


def _const_spec(shape):
    nd = len(shape)
    return pl.BlockSpec(shape, lambda *_: (0,) * nd, pipeline_mode=pl.Buffered(1))


def _ssd_kernel(x_ref, mod_ref, nw_ref, conv0_ref, h0_ref, wz_ref, wx_ref, wdt_ref, cw_ref, cb_ref,
                dtb_ref, alog_ref, dfull_ref, gnw_ref, wout_ref, eye_ref, expand_ref,
                xo_ref, convo_ref, ho_ref,
                xbc_buf, act_buf, z_buf, y_buf, xw_buf, *, R, L):
    t = pl.program_id(1)
    inner = SSM_HPG * SSM_HEAD_DIM
    d_in = SSM_GROUPS * inner
    gn = SSM_GROUPS * SSM_STATE

    @pl.when(t == 0)
    def _():
        for r in range(R):
            xbc_buf[r, TAIL0:8, :] = conv0_ref[r]
        ho_ref[...] = h0_ref[...]

    nw = nw_ref[...]
    hn = jnp.concatenate(
        [(_rms(x_ref[r], nw) * (1.0 + mod_ref[r][1:2]) + mod_ref[r][0:1]).astype(BF16) for r in range(R)], axis=0)
    z_buf[...] = _dot(hn, wz_ref[...])
    xbc = _dot(hn, wx_ref[...])
    dt = _softplus(_dot(hn, wdt_ref[...]) + dtb_ref[...])
    a = -jnp.exp(alog_ref[...])
    tri = _tri_mask(L)
    tri_b = jnp.where(tri, 1.0, 0.0).astype(BF16)
    eye = eye_ref[...]
    expand = expand_ref[...]
    left = lax.broadcasted_iota(jnp.int32, (1, LANES), 1) < SSM_HEAD_DIM

    for r in range(R):
        xbc_buf[r, 8:8 + L, :] = xbc[r * L:(r + 1) * L]
        _causal_conv_silu(xbc_buf.at[r], act_buf.at[r], L, cw_ref, cb_ref)
        tail = xbc_buf[r, L + TAIL0:L + 8, :]
        xbc_buf[r, TAIL0:8, :] = tail
        convo_ref[r] = tail

    for r in range(R):
        dtc = dt[r * L:(r + 1) * L]
        cum = _dot3_l(tri_b, dtc * a)
        cum_t = _transpose_rows(cum, eye)
        dt_t = _transpose_rows(dtc, eye)
        cum_last = cum[L - 1:L, :]
        wend = (jnp.exp(cum_last - cum) * dtc).astype(BF16)
        xw_buf[r] = (act_buf[r, :, :d_in] * _dot(wend, expand)).astype(BF16)
        dec_last = jnp.exp(_dot3_r(cum[L - 8:L, :], expand)[7:8, :])
        for g in range(SSM_GROUPS):
            bg = act_buf[r, :, d_in + g * SSM_STATE:d_in + (g + 1) * SSM_STATE].astype(BF16)
            cg = act_buf[r, :, d_in + gn + g * SSM_STATE:d_in + gn + (g + 1) * SSM_STATE].astype(BF16)
            cb = _dot_nt(cg, bg)
            hg = ho_ref[r, g]
            y_int = _dot(cg, hg.astype(BF16))
            for pr in range(SSM_HPG // 2):
                hd0 = g * SSM_HPG + 2 * pr
                lo = hd0 * SSM_HEAD_DIM
                xp = act_buf[r, :, lo:lo + LANES]
                x2 = jnp.concatenate([jnp.where(left, xp, 0.0), jnp.where(left, 0.0, xp)], axis=0).astype(BF16)
                wms, es = [], []
                for hd in (hd0, hd0 + 1):
                    ccol = jnp.broadcast_to(cum[:, hd:hd + 1], (L, LANES))
                    dec = jnp.exp(jnp.where(tri, ccol[:, :L] - cum_t[hd:hd + 1, :], -jnp.inf))
                    wms.append((cb * dec * dt_t[hd:hd + 1, :]).astype(BF16))
                    es.append(jnp.exp(ccol))
                yi = y_int[:, pr * LANES:(pr + 1) * LANES]
                y_buf[r, :, lo:lo + LANES] = (_dot(jnp.concatenate(wms, axis=1), x2)
                                              + yi * jnp.where(left, es[0], es[1]))
            ho_ref[r, g] = (hg * dec_last[:, g * inner:(g + 1) * inner]
                           + _dot_tn(bg, xw_buf[r, :, g * inner:(g + 1) * inner]))

    gnw = gnw_ref[...]
    rows = []
    for r in range(R):
        y = y_buf[r] + dfull_ref[...] * act_buf[r, :, :d_in]
        y = y * _silu(z_buf[r * L:(r + 1) * L, :])
        rows.append(jnp.concatenate(
            [_rms(y[:, g * inner:(g + 1) * inner], gnw[:, g * inner:(g + 1) * inner]).astype(BF16)
             for g in range(SSM_GROUPS)], axis=-1))
    out = _dot(jnp.concatenate(rows, axis=0), wout_ref[...])
    for r in range(R):
        xo_ref[r] = x_ref[r] + mod_ref[r][2:3] * out[r * L:(r + 1) * L]


def _ssd_layer(x, mod, nw, conv0, h0, p, R, L):
    bsz, t, d = x.shape
    d_in = SSM_GROUPS * SSM_HPG * SSM_HEAD_DIM
    cdim = p['conv_w'].shape[1]
    blk = lambda b, i: (b, 0, 0)
    st_blk = (R, SSM_GROUPS, SSM_STATE, SSM_HPG * SSM_HEAD_DIM)
    names = ['wz', 'wx', 'wdt', 'conv_w', 'conv_b', 'dt_bias', 'a_log', 'd_full', 'norm', 'w_out', 'eye', 'expand']
    in_specs = [
        pl.BlockSpec((R, L, d), lambda b, i: (b, i, 0)),
        pl.BlockSpec((R, 6, d), blk),
        _const_spec((1, d)),
        pl.BlockSpec((R, CONV_W - 1, cdim), blk),
        pl.BlockSpec(st_blk, lambda b, i: (b, 0, 0, 0)),
    ] + [_const_spec(p[n].shape) for n in names]
    out_shape = (jax.ShapeDtypeStruct((bsz, t, d), F32),
                 jax.ShapeDtypeStruct((bsz, CONV_W - 1, cdim), F32),
                 jax.ShapeDtypeStruct(h0.shape, F32))
    out_specs = (pl.BlockSpec((R, L, d), lambda b, i: (b, i, 0)),
                 pl.BlockSpec((R, CONV_W - 1, cdim), blk),
                 pl.BlockSpec(st_blk, lambda b, i: (b, 0, 0, 0)))
    scratch = [pltpu.VMEM((R, 8 + L, cdim), F32), pltpu.VMEM((R, L, cdim), F32), pltpu.VMEM((R * L, d_in), F32),
               pltpu.VMEM((R, L, d_in), F32), pltpu.VMEM((R, L, d_in), BF16)]
    return pl.pallas_call(
        functools.partial(_ssd_kernel, R=R, L=L),
        out_shape=out_shape, grid=(bsz // R, t // L), in_specs=in_specs, out_specs=out_specs,
        scratch_shapes=scratch,
        compiler_params=pltpu.CompilerParams(
            dimension_semantics=("arbitrary", "arbitrary"), vmem_limit_bytes=VMEM_LIMIT),
        name="ssd_mixer",
    )(x, mod, nw, conv0, h0, *[p[n] for n in names])


def _ffn_kernel(x_ref, mod_ref, nw_ref, wg_ref, wu_ref, wd_ref, o_ref, *, nchunk):
    x = x_ref[0]
    mod = mod_ref[0]
    hn = (_rms(x, nw_ref[...]) * (1.0 + mod[4:5]) + mod[3:4]).astype(BF16)
    fc = wg_ref.shape[1] // nchunk
    acc = None
    for j in range(nchunk):
        g = _dot(hn, wg_ref[:, j * fc:(j + 1) * fc])
        u = _dot(hn, wu_ref[:, j * fc:(j + 1) * fc])
        part = _dot((_silu(g) * u).astype(BF16), wd_ref[j * fc:(j + 1) * fc, :])
        acc = part if acc is None else acc + part
    o_ref[0] = x + mod[5:6] * acc


def _ffn_layer(x, mod, nw, p, tm):
    bsz, t, d = x.shape
    tm = min(t, 2 * tm)
    return pl.pallas_call(
        functools.partial(_ffn_kernel, nchunk=p['w_gate'].shape[1] // (2 * LANES)),
        out_shape=jax.ShapeDtypeStruct(x.shape, F32),
        grid=(bsz, t // tm),
        in_specs=[pl.BlockSpec((1, tm, d), lambda b, i: (b, i, 0)),
                  pl.BlockSpec((1, 6, d), lambda b, i: (b, 0, 0)),
                  _const_spec((1, d)),
                  _const_spec(p['w_gate'].shape), _const_spec(p['w_up'].shape), _const_spec(p['w_down'].shape)],
        out_specs=pl.BlockSpec((1, tm, d), lambda b, i: (b, i, 0)),
        compiler_params=pltpu.CompilerParams(
            dimension_semantics=("arbitrary", "arbitrary"), vmem_limit_bytes=VMEM_LIMIT),
        name="ffn",
    )(x, mod, nw, p['w_gate'], p['w_up'], p['w_down'])


def _mlstm_kernel(x_ref, mod_ref, nw_ref, conv0_ref, c0_ref, n0_ref, m0_ref,
                  wxm_ref, wo_ref, cw_ref, cb_ref, wq_ref, wk_ref, wkt_ref, wv_ref, wgq_ref, wgk_ref, wgv_ref, bg_ref,
                  gnw_ref, skip_ref, wout_ref, eyei_ref, eyef_ref,
                  xo_ref, convo_ref, co_ref, no_ref, mo_ref,
                  xm_buf, xc_buf, q_buf, k_buf, kt_buf, v_buf, op_buf, hh_buf,
                  *, R, L):
    t = pl.program_id(1)
    hd_dim = wq_ref.shape[1]
    rep = hd_dim // LANES
    k_scale = hd_dim ** -0.5

    @pl.when(t == 0)
    def _():
        for r in range(R):
            xm_buf[r, TAIL0:8, :] = conv0_ref[r]
        co_ref[...] = c0_ref[...]
        no_ref[...] = n0_ref[...]
        mo_ref[...] = m0_ref[...]

    nw = nw_ref[...]
    hn = jnp.concatenate(
        [(_rms(x_ref[r], nw) * (1.0 + mod_ref[r][1:2]) + mod_ref[r][0:1]).astype(BF16) for r in range(R)], axis=0)
    xm = _dot(hn, wxm_ref[...])
    for r in range(R):
        xm_buf[r, 8:8 + L, :] = xm[r * L:(r + 1) * L]
    op_buf[...] = _dot(hn, wo_ref[...])
    for r in range(R):
        _causal_conv_silu(xm_buf.at[r], xc_buf.at[r], L, cw_ref, cb_ref)
        tail = xm_buf[r, L + TAIL0:L + 8, :]
        xm_buf[r, TAIL0:8, :] = tail
        convo_ref[r] = tail

    gates = bg_ref[...]
    for h in range(ML_HEADS):
        sl = slice(h * hd_dim, (h + 1) * hd_dim)
        xc_h = jnp.concatenate([xc_buf[r, :, sl].astype(BF16) for r in range(R)], axis=0)
        xm_h = jnp.concatenate([xm_buf[r, 8:8 + L, sl].astype(BF16) for r in range(R)], axis=0)
        q = _dot(xc_h, wq_ref[h])
        k = _dot(xc_h, wk_ref[h])
        v = _dot(xm_h, wv_ref[h])
        gates = gates + _dot(q.astype(BF16), wgq_ref[sl, :]) + _dot(k.astype(BF16), wgk_ref[sl, :]) \
            + _dot(v.astype(BF16), wgv_ref[sl, :])
        q_buf[:, sl] = q.astype(BF16)
        k_buf[:, sl] = (k * k_scale).astype(BF16)
        v_buf[:, sl] = v.astype(BF16)
        for r in range(R):
            kt_buf[r, h] = _dot_nt(wkt_ref[h], xc_h[r * L:(r + 1) * L]) * k_scale
    lf = jnp.minimum(gates, 0.0) - jnp.log1p(jnp.exp(-jnp.abs(gates)))
    tri = _tri_mask(L)
    tri_b = jnp.where(tri, 1.0, 0.0).astype(BF16)
    lane = lax.broadcasted_iota(jnp.int32, (1, LANES), 1)
    wide = lambda a: jnp.concatenate([a] * rep, axis=1)

    for r in range(R):
        rs = slice(r * L, (r + 1) * L)
        gi = gates[rs]
        bcum = _dot3_l(tri_b, lf[rs])
        li_t = _transpose_rows(gi, eyei_ref[...])
        b_t = _transpose_rows(bcum, eyef_ref[...])
        m_prev = mo_ref[r]
        m_next = m_prev
        for h in range(ML_HEADS):
            sl = slice(h * hd_dim, (h + 1) * hd_dim)
            bcol = jnp.broadcast_to(bcum[:, ML_HEADS + h:ML_HEADS + h + 1], (L, LANES))
            mp = jnp.broadcast_to(m_prev[:, h:h + 1], (1, LANES))
            logw_t = li_t[h:h + 1, :] - b_t[h:h + 1, :]
            dmat = jnp.where(tri, bcol[:, :L] + logw_t, -jnp.inf)
            a_inter = bcol + mp
            m_t = jnp.maximum(a_inter, jnp.max(dmat, axis=1, keepdims=True))
            qb = q_buf[rs, sl]
            vb = v_buf[rs, sl]
            s_mat = jnp.exp(dmat - m_t[:, :L]) * _dot_nt(qb, k_buf[rs, sl])
            w_inter = jnp.exp(a_inter - m_t)
            ch = co_ref[r, h]
            nh = no_ref[r, h:h + 1, :]
            num = _dot(s_mat.astype(BF16), vb) + wide(w_inter) * _dot(qb, ch.astype(BF16))
            qn = jnp.sum(qb.astype(F32) * nh, axis=1, keepdims=True)
            den = jnp.sum(s_mat, axis=1, keepdims=True) + w_inter * qn
            inv = 1.0 / jnp.maximum(jnp.abs(den), jnp.exp(-m_t))
            hh_buf[r, :, sl] = num * wide(inv)
            m_new = m_t[L - 1:L, :]
            b_last = bcol[L - 1:L, :]
            w_end = jnp.exp(b_last[:, :L] + logw_t - m_new[:, :L])
            w_old = wide(jnp.exp(b_last + mp - m_new))
            kw_t = (kt_buf[r, h] * w_end).astype(BF16)
            co_ref[r, h] = w_old * ch + _dot(kw_t, vb)
            w8 = jnp.broadcast_to(w_end, (8, L)).astype(BF16)
            no_ref[r, h:h + 1, :] = w_old * nh + _dot(w8, k_buf[rs, sl])[0:1, :]
            m_next = jnp.where(lane == h, m_new, m_next)
        mo_ref[r] = m_next

    gnw = gnw_ref[...]
    skip = skip_ref[...]
    rows = []
    for r in range(R):
        parts = []
        for h in range(ML_HEADS):
            sl = slice(h * hd_dim, (h + 1) * hd_dim)
            hn_h = _rms(hh_buf[r, :, sl], gnw[:, sl])
            parts.append(((hn_h + skip[:, sl] * xc_buf[r, :, sl])
                          * _sigmoid(op_buf[r * L:(r + 1) * L, sl])).astype(BF16))
        rows.append(jnp.concatenate(parts, axis=-1))
    out = _dot(jnp.concatenate(rows, axis=0), wout_ref[...])
    for r in range(R):
        xo_ref[r] = x_ref[r] + mod_ref[r][2:3] * out[r * L:(r + 1) * L]


def _mlstm_layer(x, mod, nw, conv0, c0, n0, m0, p, R, L):
    bsz, t, d = x.shape
    inner = p['conv_w'].shape[1]
    hd = inner // ML_HEADS
    blk = lambda b, i: (b, 0, 0)
    blk4 = lambda b, i: (b, 0, 0, 0)
    names = ['wxm', 'wo', 'conv_w', 'conv_b', 'w_q', 'w_k', 'w_kt', 'w_v', 'wgq', 'wgk', 'wgv', 'bg',
             'norm', 'skip', 'w_out', 'eye_i', 'eye_f']
    in_specs = [
        pl.BlockSpec((R, L, d), lambda b, i: (b, i, 0)),
        pl.BlockSpec((R, 6, d), blk),
        _const_spec((1, d)),
        pl.BlockSpec((R, CONV_W - 1, inner), blk),
        pl.BlockSpec((R, ML_HEADS, hd, hd), blk4, pipeline_mode=pl.Buffered(1)),
        pl.BlockSpec((R, ML_HEADS, hd), blk),
        pl.BlockSpec((R, 1, LANES), blk),
    ] + [_const_spec(p[n].shape) for n in names]
    out_shape = (jax.ShapeDtypeStruct((bsz, t, d), F32),
                 jax.ShapeDtypeStruct((bsz, CONV_W - 1, inner), F32),
                 jax.ShapeDtypeStruct((bsz, ML_HEADS, hd, hd), F32),
                 jax.ShapeDtypeStruct((bsz, ML_HEADS, hd), F32),
                 jax.ShapeDtypeStruct((bsz, 1, LANES), F32))
    out_specs = (pl.BlockSpec((R, L, d), lambda b, i: (b, i, 0)),
                 pl.BlockSpec((R, CONV_W - 1, inner), blk),
                 pl.BlockSpec((R, ML_HEADS, hd, hd), blk4),
                 pl.BlockSpec((R, ML_HEADS, hd), blk),
                 pl.BlockSpec((R, 1, LANES), blk))
    scratch = [pltpu.VMEM((R, 8 + L, inner), F32), pltpu.VMEM((R, L, inner), F32),
               pltpu.VMEM((R * L, inner), BF16), pltpu.VMEM((R * L, inner), BF16),
               pltpu.VMEM((R, ML_HEADS, hd, L), F32), pltpu.VMEM((R * L, inner), BF16),
               pltpu.VMEM((R * L, inner), F32), pltpu.VMEM((R, L, inner), F32)]
    return pl.pallas_call(
        functools.partial(_mlstm_kernel, R=R, L=L),
        out_shape=out_shape, grid=(bsz // R, t // L), in_specs=in_specs, out_specs=out_specs,
        scratch_shapes=scratch,
        compiler_params=pltpu.CompilerParams(
            dimension_semantics=("arbitrary", "arbitrary"), vmem_limit_bytes=VMEM_LIMIT),
        name="mlstm_mixer",
    )(x, mod, nw, conv0, c0, n0, m0, *[p[n] for n in names])


def _router_kernel(x_ref, mod_ref, nw_ref, wr_ref, br_ref, eye_ref, h_ref, info_ref, meta_ref, cnt_ref, *, tm):
    x = x_ref[0]
    mod = mod_ref[0]
    h = _rms(x, nw_ref[...]) * (1.0 + mod[4:5]) + mod[3:4]
    h_ref[0] = h
    h1, h2, _ = _split3(h)
    w1, w2, _ = _split3(wr_ref[...])
    logits = (_dot(h1, w1) + (_dot(h1, w2) + _dot(h2, w1))) + br_ref[...]
    lane = lax.broadcasted_iota(jnp.int32, (tm, LANES), 1)
    lg = jnp.where(lane < N_EXPERTS, logits, -jnp.inf)
    m1 = jnp.max(lg, axis=1, keepdims=True)
    i1 = jnp.min(jnp.where(lg == m1, lane, LANES), axis=1, keepdims=True)
    lg2 = jnp.where(lane == i1, -jnp.inf, lg)
    m2 = jnp.max(lg2, axis=1, keepdims=True)
    i2 = jnp.min(jnp.where(lg2 == m2, lane, LANES), axis=1, keepdims=True)
    e2 = jnp.exp(m2 - m1)
    w_top1 = 1.0 / (1.0 + e2)
    w_top2 = e2 / (1.0 + e2)
    sel = jnp.logical_or(lane == i1, lane == i2)
    mask = jnp.where(sel, 1.0, 0.0).astype(BF16)
    r = lax.broadcasted_iota(jnp.int32, (tm, tm), 0)
    c = lax.broadcasted_iota(jnp.int32, (tm, tm), 1)
    before = jnp.where(c < r, 1.0, 0.0).astype(BF16)
    rank = _dot(before, mask)
    r1 = jnp.sum(jnp.where(lane == i1, rank, 0.0), axis=1, keepdims=True)
    r2 = jnp.sum(jnp.where(lane == i2, rank, 0.0), axis=1, keepdims=True)
    cols = (i1.astype(F32), i2.astype(F32), r1, r2, w_top1, w_top2)
    info = jnp.zeros((tm, LANES), F32)
    for k, col in enumerate(cols):
        info = jnp.where(lane == k, col, info)
    info_ref[0] = info
    meta_ref[0, 0] = _transpose_rows(info, eye_ref[...])
    cnt = jnp.sum(mask.astype(F32), axis=0, keepdims=True)
    cnt_ref[0, 0] = jnp.broadcast_to(cnt, (8, LANES)).astype(jnp.int32)


def _router(x, mod, nw, p, tm):
    bsz, t, d = x.shape
    nt = t // tm
    ti = lambda b, i: (b, i, 0)
    t4 = lambda b, i: (b, i, 0, 0)
    out_shape = (jax.ShapeDtypeStruct((bsz, t, d), F32),
                 jax.ShapeDtypeStruct((bsz, t, LANES), F32),
                 jax.ShapeDtypeStruct((bsz, nt, 8, tm), F32),
                 jax.ShapeDtypeStruct((bsz, nt, 8, LANES), jnp.int32))
    out_specs = (pl.BlockSpec((1, tm, d), ti), pl.BlockSpec((1, tm, LANES), ti),
                 pl.BlockSpec((1, 1, 8, tm), t4), pl.BlockSpec((1, 1, 8, LANES), t4))
    return pl.pallas_call(
        functools.partial(_router_kernel, tm=tm),
        out_shape=out_shape, grid=(bsz, nt),
        in_specs=[pl.BlockSpec((1, tm, d), ti), pl.BlockSpec((1, 6, d), lambda b, i: (b, 0, 0)),
                  _const_spec((1, d)), _const_spec(p['w_router'].shape), _const_spec(p['b_router'].shape),
                  _const_spec(p['eye8'].shape)],
        out_specs=out_specs,
        compiler_params=pltpu.CompilerParams(
            dimension_semantics=("arbitrary", "arbitrary"), vmem_limit_bytes=VMEM_LIMIT),
        name="router",
    )(x, mod, nw, p['w_router'], p['b_router'], p['eye8'])


SLAB = 512
DMA_UNROLL = 8


def _dispatch_kernel(ids_ref, pos_ref, h_ref, xs_ref, zbuf, sem, *, tm, slab):
    @pl.when(jnp.logical_and(pl.program_id(0) == 0, pl.program_id(1) == 0))
    def _():
        zbuf[...] = jnp.zeros(zbuf.shape, F32)
        for i in range(2 * N_EXPERTS):
            c = pltpu.make_async_copy(zbuf, xs_ref.at[pl.ds(ids_ref[i] * slab, slab), :], sem)
            c.start()
            c.wait()

    def send(t, carry):
        for k in range(2):
            pltpu.make_async_copy(h_ref.at[0, pl.ds(t, 1), :],
                                  xs_ref.at[pl.ds(pos_ref[0, 0, k * tm + t], 1), :], sem).start(priority=k)
        return carry

    lax.fori_loop(0, tm, send, 0, unroll=DMA_UNROLL)
    for k in range(2):
        pltpu.make_async_copy(h_ref.at[0], xs_ref.at[pl.ds(0, tm), :], sem).wait()


def _dispatch(h, pos, slab_ids, n_rows, tm, slab):
    bsz, t, d = h.shape
    nt = t // tm
    return pl.pallas_call(
        functools.partial(_dispatch_kernel, tm=tm, slab=slab),
        out_shape=jax.ShapeDtypeStruct((n_rows, d), F32),
        grid_spec=pltpu.PrefetchScalarGridSpec(
            num_scalar_prefetch=1, grid=(bsz, nt),
            in_specs=[pl.BlockSpec((1, 1, 2 * tm), lambda b, i, ids: (b * nt + i, 0, 0), memory_space=pltpu.SMEM),
                      pl.BlockSpec((1, tm, d), lambda b, i, ids: (b, i, 0))],
            out_specs=pl.BlockSpec(memory_space=pl.ANY),
            scratch_shapes=[pltpu.VMEM((slab, d), F32), pltpu.SemaphoreType.DMA(())]),
        compiler_params=pltpu.CompilerParams(
            dimension_semantics=("arbitrary", "arbitrary"), vmem_limit_bytes=VMEM_LIMIT),
        name="dispatch",
    )(slab_ids, pos, h)


def _slab_ffn_kernel(se_ref, nu_ref, x_ref, wg_ref, wu_ref, wd_ref, y_ref, *, nchunk):
    s = pl.program_id(0)

    @pl.when(s < nu_ref[0])
    def _():
        xb = x_ref[...].astype(BF16)
        fc = wg_ref.shape[2] // nchunk
        acc = None
        for j in range(nchunk):
            g = _dot(xb, wg_ref[0, :, j * fc:(j + 1) * fc])
            u = _dot(xb, wu_ref[0, :, j * fc:(j + 1) * fc])
            part = _dot((_silu(g) * u).astype(BF16), wd_ref[0, j * fc:(j + 1) * fc, :])
            acc = part if acc is None else acc + part
        y_ref[...] = acc

    @pl.when(s >= nu_ref[0])
    def _():
        y_ref[...] = jnp.zeros(y_ref.shape, F32)


def _slab_ffn(xs, slab_expert, n_used, wg, wu, wd, slab):
    n_rows, d = xs.shape
    f = wg.shape[2]
    w_idx = lambda s, se, nu: (se[s], 0, 0)
    grid_spec = pltpu.PrefetchScalarGridSpec(
        num_scalar_prefetch=2, grid=(n_rows // slab,),
        in_specs=[pl.BlockSpec((slab, d), lambda s, se, nu: (s, 0)),
                  pl.BlockSpec((1, d, f), w_idx), pl.BlockSpec((1, d, f), w_idx), pl.BlockSpec((1, f, d), w_idx)],
        out_specs=pl.BlockSpec((slab, d), lambda s, se, nu: (s, 0)))
    return pl.pallas_call(
        functools.partial(_slab_ffn_kernel, nchunk=f // (2 * LANES)),
        out_shape=jax.ShapeDtypeStruct((n_rows, d), F32),
        grid_spec=grid_spec,
        compiler_params=pltpu.CompilerParams(dimension_semantics=("arbitrary",), vmem_limit_bytes=VMEM_LIMIT),
        name="slab_ffn",
    )(slab_expert, n_used, xs, wg, wu, wd)


def _combine_kernel(pos_ref, pos_next_ref, x_ref, info_ref, mod_ref, fw_ref, ys_ref, o_ref, buf, sem, *, tm):
    step = pl.program_id(0) * pl.num_programs(1) + pl.program_id(1)
    last = pl.num_programs(0) * pl.num_programs(1) - 1
    slot = lax.rem(step, 2)

    def request(p_ref, sl):
        def fetch(t, carry):
            for k in range(2):
                pltpu.make_async_copy(ys_ref.at[pl.ds(p_ref[0, 0, k * tm + t], 1), :],
                                      buf.at[sl, k, pl.ds(t, 1), :], sem.at[sl]).start(priority=k)
            return carry
        lax.fori_loop(0, tm, fetch, 0, unroll=DMA_UNROLL)

    @pl.when(step == 0)
    def _():
        request(pos_ref, 0)

    @pl.when(step < last)
    def _():
        request(pos_next_ref, 1 - slot)

    for k in range(2):
        pltpu.make_async_copy(ys_ref.at[pl.ds(0, tm), :], buf.at[slot, k], sem.at[slot]).wait()
    info = info_ref[0]
    y = x_ref[0] + mod_ref[0][5:6] * (info[:, 4:5] * buf[slot, 0] + info[:, 5:6] * buf[slot, 1])
    o_ref[0] = _rms(y, fw_ref[...])


def _combine(x, info, mod, fw, ys, pos, tm):
    bsz, t, d = x.shape
    nt = t // tm
    ti = lambda b, i: (b, i, 0)
    tiles = bsz * nt
    return pl.pallas_call(
        functools.partial(_combine_kernel, tm=tm),
        out_shape=jax.ShapeDtypeStruct(x.shape, F32),
        grid=(bsz, nt),
        in_specs=[pl.BlockSpec((1, 1, 2 * tm), lambda b, i: (b * nt + i, 0, 0), memory_space=pltpu.SMEM),
                  pl.BlockSpec((1, 1, 2 * tm), lambda b, i: (jnp.minimum(b * nt + i + 1, tiles - 1), 0, 0),
                               memory_space=pltpu.SMEM),
                  pl.BlockSpec((1, tm, d), ti), pl.BlockSpec((1, tm, LANES), ti),
                  pl.BlockSpec((1, 6, d), lambda b, i: (b, 0, 0)), _const_spec(fw.shape),
                  pl.BlockSpec(memory_space=pl.ANY)],
        out_specs=pl.BlockSpec((1, tm, d), ti),
        scratch_shapes=[pltpu.VMEM((2, 2, tm, d), F32), pltpu.SemaphoreType.DMA((2,))],
        compiler_params=pltpu.CompilerParams(
            dimension_semantics=("arbitrary", "arbitrary"), vmem_limit_bytes=VMEM_LIMIT),
        name="combine",
    )(pos, pos, x, info, mod, fw, ys)


def _moe_layer(x, mod, nw, p, fw, tm):
    bsz, t, _ = x.shape
    n_tok = bsz * t
    tiles = n_tok // tm
    h, info, meta, cnt = _router(x, mod, nw, p, tm)
    counts = cnt[:, :, 0, :N_EXPERTS].reshape(tiles, N_EXPERTS)
    base = jnp.cumsum(counts, axis=0) - counts
    slab = SLAB if 2 * n_tok >= N_EXPERTS * SLAB else LANES
    slabs = (jnp.sum(counts, axis=0) + (slab - 1)) // slab
    slab_end = jnp.cumsum(slabs)
    start = (slab_end - slabs) * slab
    n_slabs = (2 * n_tok) // slab + N_EXPERTS
    experts = jnp.arange(N_EXPERTS, dtype=jnp.int32)
    slab_expert = jnp.minimum(jnp.sum(jnp.arange(n_slabs, dtype=jnp.int32)[:, None] >= slab_end[None, :], axis=1),
                              N_EXPERTS - 1).astype(jnp.int32)
    meta = meta.reshape(tiles, 8, tm).astype(jnp.int32)
    offs = start[None, :] + base
    pos = []
    for k in range(2):
        e_k = meta[:, k, :]
        off_k = sum(jnp.where(e_k == e, offs[:, e:e + 1], 0) for e in range(N_EXPERTS))
        pos.append(off_k + meta[:, 2 + k, :])
    td = min(t, 4 * tm)
    pos_d = jnp.concatenate([q.reshape(n_tok // td, td) for q in pos], axis=1)
    pos_d = pos_d.reshape(n_tok // td, 1, 2 * td).astype(jnp.int32)
    pos = jnp.concatenate(pos, axis=1).reshape(tiles, 1, 2 * tm).astype(jnp.int32)
    partial = jnp.concatenate([jnp.maximum(slab_end - 1, 0), jnp.minimum(slab_end[-1] + experts, n_slabs - 1)])
    xs = _dispatch(h, pos_d, partial.astype(jnp.int32), n_slabs * slab, td, slab)
    ys = _slab_ffn(xs, slab_expert, slab_end[-1:].astype(jnp.int32), p['w_gate'], p['w_up'], p['w_down'], slab)
    return _combine(x, info, mod, fw, ys, pos, tm)


def _cast_kernel(x_ref, o_ref):
    o_ref[...] = x_ref[...].astype(BF16)


def _to_bf16(a, tr=1024):
    a2 = a.reshape(-1, a.shape[-1])
    rows, cols = a2.shape
    assert rows % tr == 0
    out = pl.pallas_call(
        _cast_kernel,
        out_shape=jax.ShapeDtypeStruct(a2.shape, BF16),
        grid=(rows // tr,),
        in_specs=[pl.BlockSpec((tr, cols), lambda i: (i, 0))],
        out_specs=pl.BlockSpec((tr, cols), lambda i: (i, 0)),
        compiler_params=pltpu.CompilerParams(
            dimension_semantics=("arbitrary",), vmem_limit_bytes=VMEM_LIMIT),
        name="to_bf16",
    )(a2)
    return out.reshape(a.shape)


def _pad_lanes(a):
    return jnp.pad(a, [(0, 0)] * (a.ndim - 1) + [(0, LANES - a.shape[-1])])


def _one_hot_rows(n, offset):
    r = lax.broadcasted_iota(jnp.int32, (n, LANES), 0)
    c = lax.broadcasted_iota(jnp.int32, (n, LANES), 1)
    return (c == r + offset).astype(BF16)


def _prep_ssm(w_in, conv_w, conv_b, dt_bias, a_log, d_skip, norm_w, w_out):
    d_in = SSM_GROUPS * SSM_HPG * SSM_HEAD_DIM
    cdim = conv_w.shape[1]
    heads = SSM_GROUPS * SSM_HPG
    r = lax.broadcasted_iota(jnp.int32, (LANES, d_in), 0)
    c = lax.broadcasted_iota(jnp.int32, (LANES, d_in), 1)
    return dict(
        wz=w_in[:, :d_in].astype(BF16), wx=w_in[:, d_in:d_in + cdim].astype(BF16),
        wdt=_pad_lanes(w_in[:, d_in + cdim:]).astype(BF16),
        conv_w=conv_w, conv_b=conv_b.reshape(1, cdim),
        dt_bias=_pad_lanes(dt_bias.reshape(1, heads)), a_log=_pad_lanes(a_log.reshape(1, heads)),
        d_full=jnp.repeat(d_skip, SSM_HEAD_DIM).reshape(1, d_in),
        norm=norm_w.reshape(1, d_in), w_out=w_out.astype(BF16),
        eye=_one_hot_rows(heads, 0), expand=(c // SSM_HEAD_DIM == r).astype(BF16))


def _prep_mlstm(w_in, conv_w, conv_b, w_q, w_k, w_v, w_ig, b_ig, w_fg, b_fg, norm_w, skip, w_out):
    inner = conv_w.shape[1]
    hd = inner // ML_HEADS
    wg = jnp.concatenate([w_ig, w_fg], axis=-1).reshape(ML_HEADS, 3, hd, 2 * ML_HEADS)
    part = lambda j: _pad_lanes(wg[:, j].reshape(inner, 2 * ML_HEADS)).astype(BF16)
    return dict(
        wxm=w_in[:, :inner].astype(BF16), wo=w_in[:, inner:].astype(BF16),
        conv_w=conv_w, conv_b=conv_b.reshape(1, inner),
        w_q=w_q.astype(BF16), w_k=w_k.astype(BF16), w_kt=jnp.swapaxes(w_k, 1, 2).astype(BF16),
        w_v=w_v.astype(BF16),
        wgq=part(0), wgk=part(1), wgv=part(2),
        bg=_pad_lanes(jnp.concatenate([b_ig, b_fg]).reshape(1, 2 * ML_HEADS)),
        norm=norm_w.reshape(1, inner), skip=skip.reshape(1, inner), w_out=w_out.astype(BF16),
        eye_i=_one_hot_rows(ML_HEADS, 0), eye_f=_one_hot_rows(ML_HEADS, ML_HEADS))


def _trunk(x, mod, ssm_conv, ssm_state, ml_conv, ml_c, ml_n, ml_m, p):
    bsz, t, d = x.shape
    L = SCAN_CHUNK if t % SCAN_CHUNK == 0 else t
    R = 2 if bsz % 2 == 0 else 1
    tm = min(t, 512)
    n_heads = SSM_GROUPS * SSM_HPG
    h0 = ssm_state.reshape(bsz, SSM_GROUPS, SSM_HPG, SSM_HEAD_DIM, SSM_STATE)
    h0 = h0.transpose(0, 1, 4, 2, 3).reshape(bsz, SSM_GROUPS, SSM_STATE, SSM_HPG * SSM_HEAD_DIM)
    x, conv_s, h_s = _ssd_layer(x, mod[0], p['norm_mix'][0], ssm_conv, h0, p['ssm'], R, L)
    h_s = h_s.reshape(bsz, SSM_GROUPS, SSM_STATE, SSM_HPG, SSM_HEAD_DIM).transpose(0, 1, 3, 4, 2)
    h_s = h_s.reshape(bsz, n_heads, SSM_HEAD_DIM, SSM_STATE)
    x = _ffn_layer(x, mod[0], p['norm_ffn'][0], p['ffn'], tm)
    m0 = _pad_lanes(ml_m).reshape(bsz, 1, LANES)
    x, conv_m, c_m, n_m, m_m = _mlstm_layer(x, mod[1], p['norm_mix'][1], ml_conv, ml_c, ml_n, m0, p['ml'], R, L)
    y = _moe_layer(x, mod[1], p['norm_ffn'][1], p['moe'], p['norm_final'], tm)
    return (y, conv_s[None], h_s[None], conv_m[None], c_m[None], n_m[None], m_m[:, 0, :ML_HEADS][None])


def kernel(x_prompt, x_sample, c_prompt, c_sample, state_ssm_conv, state_ssm, state_mlstm_conv, state_mlstm_C, state_mlstm_n, state_mlstm_m, w_ada, b_ada, norm_mix, norm_ffn, norm_final, ssm_w_in, ssm_conv_w, ssm_conv_b, ssm_dt_bias, ssm_a_log, ssm_d, ssm_norm, ssm_w_out, ml_w_in, ml_conv_w, ml_conv_b, ml_w_q, ml_w_k, ml_w_v, ml_w_igate, ml_b_igate, ml_w_fgate, ml_b_fgate, ml_norm, ml_skip, ml_w_out, ffn_w_gate, ffn_w_up, ffn_w_down, moe_w_router, moe_b_router, moe_w_gate, moe_w_up, moe_w_down):
    depth, d, _ = w_ada.shape
    assert depth == 2 and state_ssm.shape[0] == 1 and state_mlstm_C.shape[0] == 1
    bp, bs = x_prompt.shape[0], x_sample.shape[0]
    p = dict(
        norm_mix=norm_mix.reshape(depth, 1, d), norm_ffn=norm_ffn.reshape(depth, 1, d),
        norm_final=norm_final.reshape(1, d),
        ssm=_prep_ssm(ssm_w_in[0], ssm_conv_w[0], ssm_conv_b[0], ssm_dt_bias[0], ssm_a_log[0], ssm_d[0],
                      ssm_norm[0], ssm_w_out[0]),
        ml=_prep_mlstm(ml_w_in[0], ml_conv_w[0], ml_conv_b[0], ml_w_q[0], ml_w_k[0], ml_w_v[0], ml_w_igate[0],
                       ml_b_igate[0], ml_w_fgate[0], ml_b_fgate[0], ml_norm[0], ml_skip[0], ml_w_out[0]),
        ffn=dict(w_gate=ffn_w_gate[0].astype(BF16), w_up=ffn_w_up[0].astype(BF16),
                 w_down=ffn_w_down[0].astype(BF16)),
        moe=dict(w_router=_pad_lanes(moe_w_router[0]), b_router=_pad_lanes(moe_b_router[0].reshape(1, -1)),
                 eye8=_one_hot_rows(8, 0),
                 w_gate=_to_bf16(moe_w_gate[0]), w_up=_to_bf16(moe_w_up[0]), w_down=_to_bf16(moe_w_down[0])))
    mod = _ada(jnp.concatenate([c_prompt, c_sample], axis=0), w_ada, b_ada)
    mod = mod.reshape(depth, bp + bs, 6, d)

    f = F32
    zeros = lambda a, b: jnp.zeros((b,) + a.shape[2:], f)
    out_p = _trunk(x_prompt, mod[:, :bp], zeros(state_ssm_conv, bp), zeros(state_ssm, bp),
                   zeros(state_mlstm_conv, bp), zeros(state_mlstm_C, bp), zeros(state_mlstm_n, bp),
                   zeros(state_mlstm_m, bp), p)
    out_s = _trunk(x_sample, mod[:, bp:], state_ssm_conv[0], state_ssm[0], state_mlstm_conv[0],
                   state_mlstm_C[0], state_mlstm_n[0], state_mlstm_m[0], p)
    return (out_p[0], out_s[0]) + tuple(out_p[1:]) + tuple(out_s[1:])
```

```python
import functools

import jax
import jax.numpy as jnp
from jax import lax
from jax.experimental import pallas as pl
from jax.experimental.pallas import tpu as pltpu

F32 = jnp.float32
BF16 = jnp.bfloat16
RMS_EPS = 1e-6
CONV_W = 4
LANES = 128
SCAN_CHUNK = LANES
TAIL0 = 8 - (CONV_W - 1)
VMEM_LIMIT = 60 * 1024 * 1024

SSM_GROUPS = 4
SSM_HPG = 8
SSM_HEAD_DIM = 64
SSM_STATE = 128
ML_HEADS = 8
N_EXPERTS = 8


def _dot(a, b):
    return jnp.dot(a, b, preferred_element_type=F32)


def _dot_nt(a, b):
    return lax.dot_general(a, b, (((1,), (1,)), ((), ())), preferred_element_type=F32)


def _dot_tn(a, b):
    return lax.dot_general(a, b, (((0,), (0,)), ((), ())), preferred_element_type=F32)


def _split3(x):
    h1 = x.astype(BF16)
    r = x - h1.astype(F32)
    h2 = r.astype(BF16)
    r = r - h2.astype(F32)
    return h1, h2, r.astype(BF16)


def _dot3_l(sel, x):
    return sum(_dot(sel, p) for p in _split3(x))


def _dot3_r(x, sel):
    return sum(_dot(p, sel) for p in _split3(x))


def _transpose_rows(x, eye):
    return sum(_dot_nt(eye, p) for p in _split3(x))


def _sigmoid(x):
    return 1.0 / (1.0 + jnp.exp(-x))


def _silu(x):
    return x * _sigmoid(x)


def _softplus(x):
    return jnp.maximum(x, 0.0) + jnp.log1p(jnp.exp(-jnp.abs(x)))


def _rms(x, g):
    return x * lax.rsqrt(jnp.mean(x * x, axis=-1, keepdims=True) + RMS_EPS) * g


def _tri_mask(n):
    r = lax.broadcasted_iota(jnp.int32, (n, n), 0)
    c = lax.broadcasted_iota(jnp.int32, (n, n), 1)
    return r >= c


def _causal_conv_silu(buf, dst, tt, cw_ref, cb_ref):
    xn = buf[8:8 + tt, :]
    x1 = pltpu.roll(xn, 1, axis=0)
    w = [cw_ref[k:k + 1, :] for k in range(CONV_W)]
    y = cb_ref[...] + xn * w[3] + x1 * w[2] + pltpu.roll(xn * w[1] + x1 * w[0], 2, axis=0)
    dst[...] = _silu(y)
    head = cb_ref[...]
    for k in range(CONV_W):
        head = head + buf[TAIL0 + k:TAIL0 + k + 8, :] * cw_ref[k:k + 1, :]
    dst[0:8, :] = _silu(head)


def _ada_kernel(c_ref, w_ref, b_ref, o_ref):
    ca = _silu(c_ref[...]).astype(BF16)
    o_ref[0] = _dot(ca, w_ref[0].astype(BF16)) + b_ref[0]


def _ada(c_all, w_ada, b_ada):
    depth, d, n = w_ada.shape
    bt = c_all.shape[0]
    tn = n // 4
    return pl.pallas_call(
        _ada_kernel,
        out_shape=jax.ShapeDtypeStruct((depth, bt, n), F32),
        grid=(depth, n // tn),
        in_specs=[pl.BlockSpec((bt, d), lambda i, j: (0, 0)),
                  pl.BlockSpec((1, d, tn), lambda i, j: (i, 0, j)),
                  pl.BlockSpec((1, 1, tn), lambda i, j: (i, 0, j))],
        out_specs=pl.BlockSpec((1, bt, tn), lambda i, j: (i, 0, j)),
        compiler_params=pltpu.CompilerParams(
            dimension_semantics=("arbitrary", "arbitrary"), vmem_limit_bytes=VMEM_LIMIT),
        name="ada",
    )(c_all, w_ada, b_ada.reshape(depth, 1, n))


def _const_spec(shape):
    nd = len(shape)
    return pl.BlockSpec(shape, lambda *_: (0,) * nd, pipeline_mode=pl.Buffered(1))


def _ssd_kernel(x_ref, mod_ref, nw_ref, conv0_ref, h0_ref, wz_ref, wx_ref, wdt_ref, cw_ref, cb_ref,
                dtb_ref, alog_ref, dfull_ref, gnw_ref, wout_ref, eye_ref, expand_ref,
                xo_ref, convo_ref, ho_ref,
                xbc_buf, act_buf, z_buf, y_buf, xw_buf, *, R, L):
    t = pl.program_id(1)
    inner = SSM_HPG * SSM_HEAD_DIM
    d_in = SSM_GROUPS * inner
    gn = SSM_GROUPS * SSM_STATE

    @pl.when(t == 0)
    def _():
        for r in range(R):
            xbc_buf[r, TAIL0:8, :] = conv0_ref[r]
        ho_ref[...] = h0_ref[...]

    nw = nw_ref[...]
    hn = jnp.concatenate(
        [(_rms(x_ref[r], nw) * (1.0 + mod_ref[r][1:2]) + mod_ref[r][0:1]).astype(BF16) for r in range(R)], axis=0)
    z_buf[...] = _dot(hn, wz_ref[...])
    cdim = wx_ref.shape[1]
    cw_cols = 1024 if cdim % 1024 == 0 else cdim
    for j in range(cdim // cw_cols):
        part = _dot(hn, wx_ref[:, j * cw_cols:(j + 1) * cw_cols])
        for r in range(R):
            xbc_buf[r, 8:8 + L, j * cw_cols:(j + 1) * cw_cols] = part[r * L:(r + 1) * L]
    dt = _softplus(_dot(hn, wdt_ref[...]) + dtb_ref[...])
    a = -jnp.exp(alog_ref[...])
    tri = _tri_mask(L)
    tri_b = jnp.where(tri, 1.0, 0.0).astype(BF16)
    eye = eye_ref[...]
    expand = expand_ref[...]
    left = lax.broadcasted_iota(jnp.int32, (1, LANES), 1) < SSM_HEAD_DIM

    for r in range(R):
        _causal_conv_silu(xbc_buf.at[r], act_buf.at[r], L, cw_ref, cb_ref)
        tail = xbc_buf[r, L + TAIL0:L + 8, :]
        xbc_buf[r, TAIL0:8, :] = tail
        convo_ref[r] = tail

    for r in range(R):
        dtc = dt[r * L:(r + 1) * L]
        cum = _dot3_l(tri_b, dtc * a)
        cum_t = _transpose_rows(cum, eye)
        dt_t = _transpose_rows(dtc, eye)
        cum_last = cum[L - 1:L, :]
        wend = (jnp.exp(cum_last - cum) * dtc).astype(BF16)
        xw_buf[r] = (act_buf[r, :, :d_in] * _dot(wend, expand)).astype(BF16)
        dec_last = jnp.exp(_dot3_r(cum[L - 8:L, :], expand)[7:8, :])
        for g in range(SSM_GROUPS):
            bg = act_buf[r, :, d_in + g * SSM_STATE:d_in + (g + 1) * SSM_STATE].astype(BF16)
            cg = act_buf[r, :, d_in + gn + g * SSM_STATE:d_in + gn + (g + 1) * SSM_STATE].astype(BF16)
            cb = _dot_nt(cg, bg)
            hg = ho_ref[r, g]
            y_int = _dot(cg, hg.astype(BF16))
            for pr in range(SSM_HPG // 2):
                hd0 = g * SSM_HPG + 2 * pr
                lo = hd0 * SSM_HEAD_DIM
                xp = act_buf[r, :, lo:lo + LANES]
                x2 = jnp.concatenate([jnp.where(left, xp, 0.0), jnp.where(left, 0.0, xp)], axis=0).astype(BF16)
                wms, es = [], []
                for hd in (hd0, hd0 + 1):
                    ccol = jnp.broadcast_to(cum[:, hd:hd + 1], (L, LANES))
                    dec = jnp.exp(jnp.where(tri, ccol[:, :L] - cum_t[hd:hd + 1, :], -jnp.inf))
                    wms.append((cb * dec * dt_t[hd:hd + 1, :]).astype(BF16))
                    es.append(jnp.exp(ccol))
                yi = y_int[:, pr * LANES:(pr + 1) * LANES]
                y_buf[r, :, lo:lo + LANES] = (_dot(jnp.concatenate(wms, axis=1), x2)
                                              + yi * jnp.where(left, es[0], es[1]))
            ho_ref[r, g] = (hg * dec_last[:, g * inner:(g + 1) * inner]
                           + _dot_tn(bg, xw_buf[r, :, g * inner:(g + 1) * inner]))

    gnw = gnw_ref[...]
    rows = []
    for r in range(R):
        y = y_buf[r] + dfull_ref[...] * act_buf[r, :, :d_in]
        y = y * _silu(z_buf[r * L:(r + 1) * L, :])
        rows.append(jnp.concatenate(
            [_rms(y[:, g * inner:(g + 1) * inner], gnw[:, g * inner:(g + 1) * inner]).astype(BF16)
             for g in range(SSM_GROUPS)], axis=-1))
    out = _dot(jnp.concatenate(rows, axis=0), wout_ref[...])
    for r in range(R):
        xo_ref[r] = x_ref[r] + mod_ref[r][2:3] * out[r * L:(r + 1) * L]


def _ssd_layer(x, mod, nw, conv0, h0, p, R, L):
    bsz, t, d = x.shape
    d_in = SSM_GROUPS * SSM_HPG * SSM_HEAD_DIM
    cdim = p['conv_w'].shape[1]
    blk = lambda b, i: (b, 0, 0)
    st_blk = (R, SSM_GROUPS, SSM_STATE, SSM_HPG * SSM_HEAD_DIM)
    names = ['wz', 'wx', 'wdt', 'conv_w', 'conv_b', 'dt_bias', 'a_log', 'd_full', 'norm', 'w_out', 'eye', 'expand']
    in_specs = [
        pl.BlockSpec((R, L, d), lambda b, i: (b, i, 0)),
        pl.BlockSpec((R, 6, d), blk),
        _const_spec((1, d)),
        pl.BlockSpec((R, CONV_W - 1, cdim), blk),
        pl.BlockSpec(st_blk, lambda b, i: (b, 0, 0, 0)),
    ] + [_const_spec(p[n].shape) for n in names]
    out_shape = (jax.ShapeDtypeStruct((bsz, t, d), F32),
                 jax.ShapeDtypeStruct((bsz, CONV_W - 1, cdim), F32),
                 jax.ShapeDtypeStruct(h0.shape, F32))
    out_specs = (pl.BlockSpec((R, L, d), lambda b, i: (b, i, 0)),
                 pl.BlockSpec((R, CONV_W - 1, cdim), blk),
                 pl.BlockSpec(st_blk, lambda b, i: (b, 0, 0, 0)))
    scratch = [pltpu.VMEM((R, 8 + L, cdim), F32), pltpu.VMEM((R, L, cdim), F32), pltpu.VMEM((R * L, d_in), F32),
               pltpu.VMEM((R, L, d_in), F32), pltpu.VMEM((R, L, d_in), BF16)]
    return pl.pallas_call(
        functools.partial(_ssd_kernel, R=R, L=L),
        out_shape=out_shape, grid=(bsz // R, t // L), in_specs=in_specs, out_specs=out_specs,
        scratch_shapes=scratch,
        compiler_params=pltpu.CompilerParams(
            dimension_semantics=("arbitrary", "arbitrary"), vmem_limit_bytes=VMEM_LIMIT),
        name="ssd_mixer",
    )(x, mod, nw, conv0, h0, *[p[n] for n in names])


def _ffn_kernel(x_ref, mod_ref, nw_ref, wg_ref, wu_ref, wd_ref, o_ref, *, nchunk):
    x = x_ref[0]
    mod = mod_ref[0]
    hn = (_rms(x, nw_ref[...]) * (1.0 + mod[4:5]) + mod[3:4]).astype(BF16)
    fc = wg_ref.shape[1] // nchunk
    acc = None
    for j in range(nchunk):
        g = _dot(hn, wg_ref[:, j * fc:(j + 1) * fc])
        u = _dot(hn, wu_ref[:, j * fc:(j + 1) * fc])
        part = _dot((_silu(g) * u).astype(BF16), wd_ref[j * fc:(j + 1) * fc, :])
        acc = part if acc is None else acc + part
    o_ref[0] = x + mod[5:6] * acc


def _ffn_layer(x, mod, nw, p, tm):
    bsz, t, d = x.shape
    tm = min(t, 2 * tm)
    return pl.pallas_call(
        functools.partial(_ffn_kernel, nchunk=p['w_gate'].shape[1] // (2 * LANES)),
        out_shape=jax.ShapeDtypeStruct(x.shape, F32),
        grid=(bsz, t // tm),
        in_specs=[pl.BlockSpec((1, tm, d), lambda b, i: (b, i, 0)),
                  pl.BlockSpec((1, 6, d), lambda b, i: (b, 0, 0)),
                  _const_spec((1, d)),
                  _const_spec(p['w_gate'].shape), _const_spec(p['w_up'].shape), _const_spec(p['w_down'].shape)],
        out_specs=pl.BlockSpec((1, tm, d), lambda b, i: (b, i, 0)),
        compiler_params=pltpu.CompilerParams(
            dimension_semantics=("arbitrary", "arbitrary"), vmem_limit_bytes=VMEM_LIMIT),
        name="ffn",
    )(x, mod, nw, p['w_gate'], p['w_up'], p['w_down'])


def _mlstm_kernel(x_ref, mod_ref, nw_ref, conv0_ref, c0_ref, n0_ref, m0_ref,
                  wxm_ref, wo_ref, cw_ref, cb_ref, wq_ref, wk_ref, wkt_ref, wv_ref, wgq_ref, wgk_ref, wgv_ref, bg_ref,
                  gnw_ref, skip_ref, wout_ref, eyei_ref, eyef_ref,
                  xo_ref, convo_ref, co_ref, no_ref, mo_ref,
                  xm_buf, xc_buf, q_buf, k_buf, kt_buf, v_buf, op_buf, hh_buf,
                  *, R, L):
    t = pl.program_id(1)
    hd_dim = wq_ref.shape[1]
    rep = hd_dim // LANES
    k_scale = hd_dim ** -0.5

    @pl.when(t == 0)
    def _():
        for r in range(R):
            xm_buf[r, TAIL0:8, :] = conv0_ref[r]
        co_ref[...] = c0_ref[...]
        no_ref[...] = n0_ref[...]
        mo_ref[...] = m0_ref[...]

    nw = nw_ref[...]
    hn = jnp.concatenate(
        [(_rms(x_ref[r], nw) * (1.0 + mod_ref[r][1:2]) + mod_ref[r][0:1]).astype(BF16) for r in range(R)], axis=0)
    xm = _dot(hn, wxm_ref[...])
    for r in range(R):
        xm_buf[r, 8:8 + L, :] = xm[r * L:(r + 1) * L]
    op_buf[...] = _dot(hn, wo_ref[...])
    for r in range(R):
        _causal_conv_silu(xm_buf.at[r], xc_buf.at[r], L, cw_ref, cb_ref)
        tail = xm_buf[r, L + TAIL0:L + 8, :]
        xm_buf[r, TAIL0:8, :] = tail
        convo_ref[r] = tail

    gates = bg_ref[...]
    for h in range(ML_HEADS):
        sl = slice(h * hd_dim, (h + 1) * hd_dim)
        xc_h = jnp.concatenate([xc_buf[r, :, sl].astype(BF16) for r in range(R)], axis=0)
        xm_h = jnp.concatenate([xm_buf[r, 8:8 + L, sl].astype(BF16) for r in range(R)], axis=0)
        q = _dot(xc_h, wq_ref[h])
        k = _dot(xc_h, wk_ref[h])
        v = _dot(xm_h, wv_ref[h])
        gates = gates + _dot(q.astype(BF16), wgq_ref[sl, :]) + _dot(k.astype(BF16), wgk_ref[sl, :]) \
            + _dot(v.astype(BF16), wgv_ref[sl, :])
        q_buf[:, sl] = q.astype(BF16)
        k_buf[:, sl] = (k * k_scale).astype(BF16)
        v_buf[:, sl] = v.astype(BF16)
        for r in range(R):
            kt_buf[r, h] = _dot_nt(wkt_ref[h], xc_h[r * L:(r + 1) * L]) * k_scale
    lf = jnp.minimum(gates, 0.0) - jnp.log1p(jnp.exp(-jnp.abs(gates)))
    tri = _tri_mask(L)
    tri_b = jnp.where(tri, 1.0, 0.0).astype(BF16)
    lane = lax.broadcasted_iota(jnp.int32, (1, LANES), 1)
    wide = lambda a: jnp.concatenate([a] * rep, axis=1)

    for r in range(R):
        rs = slice(r * L, (r + 1) * L)
        gi = gates[rs]
        bcum = _dot3_l(tri_b, lf[rs])
        li_t = _transpose_rows(gi, eyei_ref[...])
        b_t = _transpose_rows(bcum, eyef_ref[...])
        m_prev = mo_ref[r]
        m_next = m_prev
        for h in range(ML_HEADS):
            sl = slice(h * hd_dim, (h + 1) * hd_dim)
            bcol = jnp.broadcast_to(bcum[:, ML_HEADS + h:ML_HEADS + h + 1], (L, LANES))
            mp = jnp.broadcast_to(m_prev[:, h:h + 1], (1, LANES))
            logw_t = li_t[h:h + 1, :] - b_t[h:h + 1, :]
            dmat = jnp.where(tri, bcol[:, :L] + logw_t, -jnp.inf)
            a_inter = bcol + mp
            m_t = jnp.maximum(a_inter, jnp.max(dmat, axis=1, keepdims=True))
            qb = q_buf[rs, sl]
            vb = v_buf[rs, sl]
            s_mat = jnp.exp(dmat - m_t[:, :L]) * _dot_nt(qb, k_buf[rs, sl])
            w_inter = jnp.exp(a_inter - m_t)
            ch = co_ref[r, h]
            nh = no_ref[r, h:h + 1, :]
            num = _dot(s_mat.astype(BF16), vb) + wide(w_inter) * _dot(qb, ch.astype(BF16))
            qn = jnp.sum(qb.astype(F32) * nh, axis=1, keepdims=True)
            den = jnp.sum(s_mat, axis=1, keepdims=True) + w_inter * qn
            inv = 1.0 / jnp.maximum(jnp.abs(den), jnp.exp(-m_t))
            hh_buf[r, :, sl] = num * wide(inv)
            m_new = m_t[L - 1:L, :]
            b_last = bcol[L - 1:L, :]
            w_end = jnp.exp(b_last[:, :L] + logw_t - m_new[:, :L])
            w_old = wide(jnp.exp(b_last + mp - m_new))
            kw_t = (kt_buf[r, h] * w_end).astype(BF16)
            co_ref[r, h] = w_old * ch + _dot(kw_t, vb)
            w8 = jnp.broadcast_to(w_end, (8, L)).astype(BF16)
            no_ref[r, h:h + 1, :] = w_old * nh + _dot(w8, k_buf[rs, sl])[0:1, :]
            m_next = jnp.where(lane == h, m_new, m_next)
        mo_ref[r] = m_next

    gnw = gnw_ref[...]
    skip = skip_ref[...]
    rows = []
    for r in range(R):
        parts = []
        for h in range(ML_HEADS):
            sl = slice(h * hd_dim, (h + 1) * hd_dim)
            hn_h = _rms(hh_buf[r, :, sl], gnw[:, sl])
            parts.append(((hn_h + skip[:, sl] * xc_buf[r, :, sl])
                          * _sigmoid(op_buf[r * L:(r + 1) * L, sl])).astype(BF16))
        rows.append(jnp.concatenate(parts, axis=-1))
    out = _dot(jnp.concatenate(rows, axis=0), wout_ref[...])
    for r in range(R):
        xo_ref[r] = x_ref[r] + mod_ref[r][2:3] * out[r * L:(r + 1) * L]


def _mlstm_layer(x, mod, nw, conv0, c0, n0, m0, p, R, L):
    bsz, t, d = x.shape
    inner = p['conv_w'].shape[1]
    hd = inner // ML_HEADS
    blk = lambda b, i: (b, 0, 0)
    blk4 = lambda b, i: (b, 0, 0, 0)
    names = ['wxm', 'wo', 'conv_w', 'conv_b', 'w_q', 'w_k', 'w_kt', 'w_v', 'wgq', 'wgk', 'wgv', 'bg',
             'norm', 'skip', 'w_out', 'eye_i', 'eye_f']
    in_specs = [
        pl.BlockSpec((R, L, d), lambda b, i: (b, i, 0)),
        pl.BlockSpec((R, 6, d), blk),
        _const_spec((1, d)),
        pl.BlockSpec((R, CONV_W - 1, inner), blk),
        pl.BlockSpec((R, ML_HEADS, hd, hd), blk4, pipeline_mode=pl.Buffered(1)),
        pl.BlockSpec((R, ML_HEADS, hd), blk),
        pl.BlockSpec((R, 1, LANES), blk),
    ] + [_const_spec(p[n].shape) for n in names]
    out_shape = (jax.ShapeDtypeStruct((bsz, t, d), F32),
                 jax.ShapeDtypeStruct((bsz, CONV_W - 1, inner), F32),
                 jax.ShapeDtypeStruct((bsz, ML_HEADS, hd, hd), F32),
                 jax.ShapeDtypeStruct((bsz, ML_HEADS, hd), F32),
                 jax.ShapeDtypeStruct((bsz, 1, LANES), F32))
    out_specs = (pl.BlockSpec((R, L, d), lambda b, i: (b, i, 0)),
                 pl.BlockSpec((R, CONV_W - 1, inner), blk),
                 pl.BlockSpec((R, ML_HEADS, hd, hd), blk4),
                 pl.BlockSpec((R, ML_HEADS, hd), blk),
                 pl.BlockSpec((R, 1, LANES), blk))
    scratch = [pltpu.VMEM((R, 8 + L, inner), F32), pltpu.VMEM((R, L, inner), F32),
               pltpu.VMEM((R * L, inner), BF16), pltpu.VMEM((R * L, inner), BF16),
               pltpu.VMEM((R, ML_HEADS, hd, L), F32), pltpu.VMEM((R * L, inner), BF16),
               pltpu.VMEM((R * L, inner), F32), pltpu.VMEM((R, L, inner), F32)]
    return pl.pallas_call(
        functools.partial(_mlstm_kernel, R=R, L=L),
        out_shape=out_shape, grid=(bsz // R, t // L), in_specs=in_specs, out_specs=out_specs,
        scratch_shapes=scratch,
        compiler_params=pltpu.CompilerParams(
            dimension_semantics=("arbitrary", "arbitrary"), vmem_limit_bytes=VMEM_LIMIT),
        name="mlstm_mixer",
    )(x, mod, nw, conv0, c0, n0, m0, *[p[n] for n in names])


def _router_kernel(x_ref, mod_ref, nw_ref, wr_ref, br_ref, eye_ref, h_ref, info_ref, meta_ref, cnt_ref, *, tm):
    x = x_ref[0]
    mod = mod_ref[0]
    h = _rms(x, nw_ref[...]) * (1.0 + mod[4:5]) + mod[3:4]
    h_ref[0] = h
    h1, h2, _ = _split3(h)
    w1, w2, _ = _split3(wr_ref[...])
    logits = (_dot(h1, w1) + (_dot(h1, w2) + _dot(h2, w1))) + br_ref[...]
    lane = lax.broadcasted_iota(jnp.int32, (tm, LANES), 1)
    lg = jnp.where(lane < N_EXPERTS, logits, -jnp.inf)
    m1 = jnp.max(lg, axis=1, keepdims=True)
    i1 = jnp.min(jnp.where(lg == m1, lane, LANES), axis=1, keepdims=True)
    lg2 = jnp.where(lane == i1, -jnp.inf, lg)
    m2 = jnp.max(lg2, axis=1, keepdims=True)
    i2 = jnp.min(jnp.where(lg2 == m2, lane, LANES), axis=1, keepdims=True)
    e2 = jnp.exp(m2 - m1)
    w_top1 = 1.0 / (1.0 + e2)
    w_top2 = e2 / (1.0 + e2)
    sel = jnp.logical_or(lane == i1, lane == i2)
    mask = jnp.where(sel, 1.0, 0.0).astype(BF16)
    r = lax.broadcasted_iota(jnp.int32, (tm, tm), 0)
    c = lax.broadcasted_iota(jnp.int32, (tm, tm), 1)
    before = jnp.where(c < r, 1.0, 0.0).astype(BF16)
    rank = _dot(before, mask)
    r1 = jnp.sum(jnp.where(lane == i1, rank, 0.0), axis=1, keepdims=True)
    r2 = jnp.sum(jnp.where(lane == i2, rank, 0.0), axis=1, keepdims=True)
    cols = (i1.astype(F32), i2.astype(F32), r1, r2, w_top1, w_top2)
    info = jnp.zeros((tm, LANES), F32)
    for k, col in enumerate(cols):
        info = jnp.where(lane == k, col, info)
    info_ref[0] = info
    meta_ref[0, 0] = _transpose_rows(info, eye_ref[...])
    cnt = jnp.sum(mask.astype(F32), axis=0, keepdims=True)
    cnt_ref[0, 0] = jnp.broadcast_to(cnt, (8, LANES)).astype(jnp.int32)


def _router(x, mod, nw, p, tm):
    bsz, t, d = x.shape
    nt = t // tm
    ti = lambda b, i: (b, i, 0)
    t4 = lambda b, i: (b, i, 0, 0)
    out_shape = (jax.ShapeDtypeStruct((bsz, t, d), F32),
                 jax.ShapeDtypeStruct((bsz, t, LANES), F32),
                 jax.ShapeDtypeStruct((bsz, nt, 8, tm), F32),
                 jax.ShapeDtypeStruct((bsz, nt, 8, LANES), jnp.int32))
    out_specs = (pl.BlockSpec((1, tm, d), ti), pl.BlockSpec((1, tm, LANES), ti),
                 pl.BlockSpec((1, 1, 8, tm), t4), pl.BlockSpec((1, 1, 8, LANES), t4))
    return pl.pallas_call(
        functools.partial(_router_kernel, tm=tm),
        out_shape=out_shape, grid=(bsz, nt),
        in_specs=[pl.BlockSpec((1, tm, d), ti), pl.BlockSpec((1, 6, d), lambda b, i: (b, 0, 0)),
                  _const_spec((1, d)), _const_spec(p['w_router'].shape), _const_spec(p['b_router'].shape),
                  _const_spec(p['eye8'].shape)],
        out_specs=out_specs,
        compiler_params=pltpu.CompilerParams(
            dimension_semantics=("arbitrary", "arbitrary"), vmem_limit_bytes=VMEM_LIMIT),
        name="router",
    )(x, mod, nw, p['w_router'], p['b_router'], p['eye8'])


SLAB = 512
DMA_UNROLL = 8


def _dispatch_kernel(ids_ref, pos_ref, h_ref, xs_ref, zbuf, sem, *, tm, slab):
    @pl.when(jnp.logical_and(pl.program_id(0) == 0, pl.program_id(1) == 0))
    def _():
        zbuf[...] = jnp.zeros(zbuf.shape, F32)
        for i in range(2 * N_EXPERTS):
            c = pltpu.make_async_copy(zbuf, xs_ref.at[pl.ds(ids_ref[i] * slab, slab), :], sem)
            c.start()
            c.wait()

    def send(t, carry):
        for k in range(2):
            pltpu.make_async_copy(h_ref.at[0, pl.ds(t, 1), :],
                                  xs_ref.at[pl.ds(pos_ref[0, 0, k * tm + t], 1), :], sem).start(priority=k)
        return carry

    lax.fori_loop(0, tm, send, 0, unroll=DMA_UNROLL)
    for k in range(2):
        pltpu.make_async_copy(h_ref.at[0], xs_ref.at[pl.ds(0, tm), :], sem).wait()


def _dispatch(h, pos, slab_ids, n_rows, tm, slab):
    bsz, t, d = h.shape
    nt = t // tm
    return pl.pallas_call(
        functools.partial(_dispatch_kernel, tm=tm, slab=slab),
        out_shape=jax.ShapeDtypeStruct((n_rows, d), F32),
        grid_spec=pltpu.PrefetchScalarGridSpec(
            num_scalar_prefetch=1, grid=(bsz, nt),
            in_specs=[pl.BlockSpec((1, 1, 2 * tm), lambda b, i, ids: (b * nt + i, 0, 0), memory_space=pltpu.SMEM),
                      pl.BlockSpec((1, tm, d), lambda b, i, ids: (b, i, 0))],
            out_specs=pl.BlockSpec(memory_space=pl.ANY),
            scratch_shapes=[pltpu.VMEM((slab, d), F32), pltpu.SemaphoreType.DMA(())]),
        compiler_params=pltpu.CompilerParams(
            dimension_semantics=("arbitrary", "arbitrary"), vmem_limit_bytes=VMEM_LIMIT),
        name="dispatch",
    )(slab_ids, pos, h)


def _slab_ffn_kernel(se_ref, nu_ref, x_ref, wg_ref, wu_ref, wd_ref, y_ref, *, nchunk):
    s = pl.program_id(0)

    @pl.when(s < nu_ref[0])
    def _():
        xb = x_ref[...].astype(BF16)
        fc = wg_ref.shape[2] // nchunk
        acc = None
        for j in range(nchunk):
            g = _dot(xb, wg_ref[0, :, j * fc:(j + 1) * fc])
            u = _dot(xb, wu_ref[0, :, j * fc:(j + 1) * fc])
            part = _dot((_silu(g) * u).astype(BF16), wd_ref[0, j * fc:(j + 1) * fc, :])
            acc = part if acc is None else acc + part
        y_ref[...] = acc

    @pl.when(s >= nu_ref[0])
    def _():
        y_ref[...] = jnp.zeros(y_ref.shape, F32)


def _slab_ffn(xs, slab_expert, n_used, wg, wu, wd, slab):
    n_rows, d = xs.shape
    f = wg.shape[2]
    w_idx = lambda s, se, nu: (se[s], 0, 0)
    grid_spec = pltpu.PrefetchScalarGridSpec(
        num_scalar_prefetch=2, grid=(n_rows // slab,),
        in_specs=[pl.BlockSpec((slab, d), lambda s, se, nu: (s, 0)),
                  pl.BlockSpec((1, d, f), w_idx), pl.BlockSpec((1, d, f), w_idx), pl.BlockSpec((1, f, d), w_idx)],
        out_specs=pl.BlockSpec((slab, d), lambda s, se, nu: (s, 0)))
    return pl.pallas_call(
        functools.partial(_slab_ffn_kernel, nchunk=f // (2 * LANES)),
        out_shape=jax.ShapeDtypeStruct((n_rows, d), F32),
        grid_spec=grid_spec,
        compiler_params=pltpu.CompilerParams(dimension_semantics=("arbitrary",), vmem_limit_bytes=VMEM_LIMIT),
        name="slab_ffn",
    )(slab_expert, n_used, xs, wg, wu, wd)


def _combine_kernel(pos_ref, pos_next_ref, x_ref, info_ref, mod_ref, fw_ref, ys_ref, o_ref, buf, sem, *, tm):
    step = pl.program_id(0) * pl.num_programs(1) + pl.program_id(1)
    last = pl.num_programs(0) * pl.num_programs(1) - 1
    slot = lax.rem(step, 2)

    def request(p_ref, sl):
        def fetch(t, carry):
            for k in range(2):
                pltpu.make_async_copy(ys_ref.at[pl.ds(p_ref[0, 0, k * tm + t], 1), :],
                                      buf.at[sl, k, pl.ds(t, 1), :], sem.at[sl]).start(priority=k)
            return carry
        lax.fori_loop(0, tm, fetch, 0, unroll=DMA_UNROLL)

    @pl.when(step == 0)
    def _():
        request(pos_ref, 0)

    @pl.when(step < last)
    def _():
        request(pos_next_ref, 1 - slot)

    for k in range(2):
        pltpu.make_async_copy(ys_ref.at[pl.ds(0, tm), :], buf.at[slot, k], sem.at[slot]).wait()
    info = info_ref[0]
    y = x_ref[0] + mod_ref[0][5:6] * (info[:, 4:5] * buf[slot, 0] + info[:, 5:6] * buf[slot, 1])
    o_ref[0] = _rms(y, fw_ref[...])


def _combine(x, info, mod, fw, ys, pos, tm):
    bsz, t, d = x.shape
    nt = t // tm
    ti = lambda b, i: (b, i, 0)
    tiles = bsz * nt
    return pl.pallas_call(
        functools.partial(_combine_kernel, tm=tm),
        out_shape=jax.ShapeDtypeStruct(x.shape, F32),
        grid=(bsz, nt),
        in_specs=[pl.BlockSpec((1, 1, 2 * tm), lambda b, i: (b * nt + i, 0, 0), memory_space=pltpu.SMEM),
                  pl.BlockSpec((1, 1, 2 * tm), lambda b, i: (jnp.minimum(b * nt + i + 1, tiles - 1), 0, 0),
                               memory_space=pltpu.SMEM),
                  pl.BlockSpec((1, tm, d), ti), pl.BlockSpec((1, tm, LANES), ti),
                  pl.BlockSpec((1, 6, d), lambda b, i: (b, 0, 0)), _const_spec(fw.shape),
                  pl.BlockSpec(memory_space=pl.ANY)],
        out_specs=pl.BlockSpec((1, tm, d), ti),
        scratch_shapes=[pltpu.VMEM((2, 2, tm, d), F32), pltpu.SemaphoreType.DMA((2,))],
        compiler_params=pltpu.CompilerParams(
            dimension_semantics=("arbitrary", "arbitrary"), vmem_limit_bytes=VMEM_LIMIT),
        name="combine",
    )(pos, pos, x, info, mod, fw, ys)


def _moe_layer(x, mod, nw, p, fw, tm):
    bsz, t, _ = x.shape
    n_tok = bsz * t
    tiles = n_tok // tm
    h, info, meta, cnt = _router(x, mod, nw, p, tm)
    counts = cnt[:, :, 0, :N_EXPERTS].reshape(tiles, N_EXPERTS)
    base = jnp.cumsum(counts, axis=0) - counts
    slab = SLAB if 2 * n_tok >= N_EXPERTS * SLAB else LANES
    slabs = (jnp.sum(counts, axis=0) + (slab - 1)) // slab
    slab_end = jnp.cumsum(slabs)
    start = (slab_end - slabs) * slab
    n_slabs = (2 * n_tok) // slab + N_EXPERTS
    experts = jnp.arange(N_EXPERTS, dtype=jnp.int32)
    slab_expert = jnp.minimum(jnp.sum(jnp.arange(n_slabs, dtype=jnp.int32)[:, None] >= slab_end[None, :], axis=1),
                              N_EXPERTS - 1).astype(jnp.int32)
    meta = meta.reshape(tiles, 8, tm).astype(jnp.int32)
    offs = start[None, :] + base
    pos = []
    for k in range(2):
        e_k = meta[:, k, :]
        off_k = sum(jnp.where(e_k == e, offs[:, e:e + 1], 0) for e in range(N_EXPERTS))
        pos.append(off_k + meta[:, 2 + k, :])
    td = min(t, 4 * tm)
    pos_d = jnp.concatenate([q.reshape(n_tok // td, td) for q in pos], axis=1)
    pos_d = pos_d.reshape(n_tok // td, 1, 2 * td).astype(jnp.int32)
    pos = jnp.concatenate(pos, axis=1).reshape(tiles, 1, 2 * tm).astype(jnp.int32)
    partial = jnp.concatenate([jnp.maximum(slab_end - 1, 0), jnp.minimum(slab_end[-1] + experts, n_slabs - 1)])
    xs = _dispatch(h, pos_d, partial.astype(jnp.int32), n_slabs * slab, td, slab)
    ys = _slab_ffn(xs, slab_expert, slab_end[-1:].astype(jnp.int32), p['w_gate'], p['w_up'], p['w_down'], slab)
    return _combine(x, info, mod, fw, ys, pos, tm)


def _cast_kernel(x_ref, o_ref):
    o_ref[...] = x_ref[...].astype(BF16)


def _to_bf16(a, tr=1024):
    a2 = a.reshape(-1, a.shape[-1])
    rows, cols = a2.shape
    assert rows % tr == 0
    out = pl.pallas_call(
        _cast_kernel,
        out_shape=jax.ShapeDtypeStruct(a2.shape, BF16),
        grid=(rows // tr,),
        in_specs=[pl.BlockSpec((tr, cols), lambda i: (i, 0))],
        out_specs=pl.BlockSpec((tr, cols), lambda i: (i, 0)),
        compiler_params=pltpu.CompilerParams(
            dimension_semantics=("arbitrary",), vmem_limit_bytes=VMEM_LIMIT),
        name="to_bf16",
    )(a2)
    return out.reshape(a.shape)


def _pad_lanes(a):
    return jnp.pad(a, [(0, 0)] * (a.ndim - 1) + [(0, LANES - a.shape[-1])])


def _one_hot_rows(n, offset):
    r = lax.broadcasted_iota(jnp.int32, (n, LANES), 0)
    c = lax.broadcasted_iota(jnp.int32, (n, LANES), 1)
    return (c == r + offset).astype(BF16)


def _prep_ssm(w_in, conv_w, conv_b, dt_bias, a_log, d_skip, norm_w, w_out):
    d_in = SSM_GROUPS * SSM_HPG * SSM_HEAD_DIM
    cdim = conv_w.shape[1]
    heads = SSM_GROUPS * SSM_HPG
    r = lax.broadcasted_iota(jnp.int32, (LANES, d_in), 0)
    c = lax.broadcasted_iota(jnp.int32, (LANES, d_in), 1)
    return dict(
        wz=w_in[:, :d_in].astype(BF16), wx=w_in[:, d_in:d_in + cdim].astype(BF16),
        wdt=_pad_lanes(w_in[:, d_in + cdim:]).astype(BF16),
        conv_w=conv_w, conv_b=conv_b.reshape(1, cdim),
        dt_bias=_pad_lanes(dt_bias.reshape(1, heads)), a_log=_pad_lanes(a_log.reshape(1, heads)),
        d_full=jnp.repeat(d_skip, SSM_HEAD_DIM).reshape(1, d_in),
        norm=norm_w.reshape(1, d_in), w_out=w_out.astype(BF16),
        eye=_one_hot_rows(heads, 0), expand=(c // SSM_HEAD_DIM == r).astype(BF16))


def _prep_mlstm(w_in, conv_w, conv_b, w_q, w_k, w_v, w_ig, b_ig, w_fg, b_fg, norm_w, skip, w_out):
    inner = conv_w.shape[1]
    hd = inner // ML_HEADS
    wg = jnp.concatenate([w_ig, w_fg], axis=-1).reshape(ML_HEADS, 3, hd, 2 * ML_HEADS)
    part = lambda j: _pad_lanes(wg[:, j].reshape(inner, 2 * ML_HEADS)).astype(BF16)
    return dict(
        wxm=w_in[:, :inner].astype(BF16), wo=w_in[:, inner:].astype(BF16),
        conv_w=conv_w, conv_b=conv_b.reshape(1, inner),
        w_q=w_q.astype(BF16), w_k=w_k.astype(BF16), w_kt=jnp.swapaxes(w_k, 1, 2).astype(BF16),
        w_v=w_v.astype(BF16),
        wgq=part(0), wgk=part(1), wgv=part(2),
        bg=_pad_lanes(jnp.concatenate([b_ig, b_fg]).reshape(1, 2 * ML_HEADS)),
        norm=norm_w.reshape(1, inner), skip=skip.reshape(1, inner), w_out=w_out.astype(BF16),
        eye_i=_one_hot_rows(ML_HEADS, 0), eye_f=_one_hot_rows(ML_HEADS, ML_HEADS))


def _trunk(x, mod, ssm_conv, ssm_state, ml_conv, ml_c, ml_n, ml_m, p):
    bsz, t, d = x.shape
    L = SCAN_CHUNK if t % SCAN_CHUNK == 0 else t
    R = 2 if bsz % 2 == 0 else 1
    tm = min(t, 512)
    n_heads = SSM_GROUPS * SSM_HPG
    h0 = ssm_state.reshape(bsz, SSM_GROUPS, SSM_HPG, SSM_HEAD_DIM, SSM_STATE)
    h0 = h0.transpose(0, 1, 4, 2, 3).reshape(bsz, SSM_GROUPS, SSM_STATE, SSM_HPG * SSM_HEAD_DIM)
    x, conv_s, h_s = _ssd_layer(x, mod[0], p['norm_mix'][0], ssm_conv, h0, p['ssm'], R, L)
    h_s = h_s.reshape(bsz, SSM_GROUPS, SSM_STATE, SSM_HPG, SSM_HEAD_DIM).transpose(0, 1, 3, 4, 2)
    h_s = h_s.reshape(bsz, n_heads, SSM_HEAD_DIM, SSM_STATE)
    x = _ffn_layer(x, mod[0], p['norm_ffn'][0], p['ffn'], tm)
    m0 = _pad_lanes(ml_m).reshape(bsz, 1, LANES)
    x, conv_m, c_m, n_m, m_m = _mlstm_layer(x, mod[1], p['norm_mix'][1], ml_conv, ml_c, ml_n, m0, p['ml'], R, L)
    y = _moe_layer(x, mod[1], p['norm_ffn'][1], p['moe'], p['norm_final'], tm)
    return (y, conv_s[None], h_s[None], conv_m[None], c_m[None], n_m[None], m_m[:, 0, :ML_HEADS][None])


def kernel(x_prompt, x_sample, c_prompt, c_sample, state_ssm_conv, state_ssm, state_mlstm_conv, state_mlstm_C, state_mlstm_n, state_mlstm_m, w_ada, b_ada, norm_mix, norm_ffn, norm_final, ssm_w_in, ssm_conv_w, ssm_conv_b, ssm_dt_bias, ssm_a_log, ssm_d, ssm_norm, ssm_w_out, ml_w_in, ml_conv_w, ml_conv_b, ml_w_q, ml_w_k, ml_w_v, ml_w_igate, ml_b_igate, ml_w_fgate, ml_b_fgate, ml_norm, ml_skip, ml_w_out, ffn_w_gate, ffn_w_up, ffn_w_down, moe_w_router, moe_b_router, moe_w_gate, moe_w_up, moe_w_down):
    depth, d, _ = w_ada.shape
    assert depth == 2 and state_ssm.shape[0] == 1 and state_mlstm_C.shape[0] == 1
    bp, bs = x_prompt.shape[0], x_sample.shape[0]
    p = dict(
        norm_mix=norm_mix.reshape(depth, 1, d), norm_ffn=norm_ffn.reshape(depth, 1, d),
        norm_final=norm_final.reshape(1, d),
        ssm=_prep_ssm(ssm_w_in[0], ssm_conv_w[0], ssm_conv_b[0], ssm_dt_bias[0], ssm_a_log[0], ssm_d[0],
                      ssm_norm[0], ssm_w_out[0]),
        ml=_prep_mlstm(ml_w_in[0], ml_conv_w[0], ml_conv_b[0], ml_w_q[0], ml_w_k[0], ml_w_v[0], ml_w_igate[0],
                       ml_b_igate[0], ml_w_fgate[0], ml_b_fgate[0], ml_norm[0], ml_skip[0], ml_w_out[0]),
        ffn=dict(w_gate=ffn_w_gate[0].astype(BF16), w_up=ffn_w_up[0].astype(BF16),
                 w_down=ffn_w_down[0].astype(BF16)),
        moe=dict(w_router=_pad_lanes(moe_w_router[0]), b_router=_pad_lanes(moe_b_router[0].reshape(1, -1)),
                 eye8=_one_hot_rows(8, 0),
                 w_gate=_to_bf16(moe_w_gate[0]), w_up=_to_bf16(moe_w_up[0]), w_down=_to_bf16(moe_w_down[0])))
    mod = _ada(jnp.concatenate([c_prompt, c_sample], axis=0), w_ada, b_ada)
    mod = mod.reshape(depth, bp + bs, 6, d)

    f = F32
    zeros = lambda a, b: jnp.zeros((b,) + a.shape[2:], f)
    out_p = _trunk(x_prompt, mod[:, :bp], zeros(state_ssm_conv, bp), zeros(state_ssm, bp),
                   zeros(state_mlstm_conv, bp), zeros(state_mlstm_C, bp), zeros(state_mlstm_n, bp),
                   zeros(state_mlstm_m, bp), p)
    out_s = _trunk(x_sample, mod[:, bp:], state_ssm_conv[0], state_ssm[0], state_mlstm_conv[0],
                   state_mlstm_C[0], state_mlstm_n[0], state_mlstm_m[0], p)
    return (out_p[0], out_s[0]) + tuple(out_p[1:]) + tuple(out_s[1:])
```
